```python
import math, functools
import jax, jax.numpy as jnp
from jax import lax
import numpy as np

D_MODEL = 1024
BATCH = 4
SEQ = 4096
DEPTH = 2

N_MIXERS = 2
N_A = (DEPTH + 1) // 2
N_B = DEPTH // 2
D_RNN = D_MODEL
RG_BLOCKS = 4
RG_BLOCK = D_RNN // RG_BLOCKS
RG_CONV = 4
RG_C = 8.0
GLA_HEADS = 4
GLA_DK = D_MODEL // 2 // GLA_HEADS
GLA_DV = D_MODEL // GLA_HEADS
GLA_RANK = 16
GLA_TAU = 16.0
GLA_CHUNK = 64
GLA_QK = GLA_HEADS * GLA_DK
GLA_IN = 2 * GLA_QK + 2 * D_MODEL + GLA_RANK
D_FF = ((8 * D_MODEL // 3 + 127) // 128) * 128
FFN_CONV = 3
EPS = 1e-6

kernel_name = "hybrid_rglru_gla_convffn_adaln"


def rmsnorm(x, g):
    x32 = x.astype(jnp.float32)
    y = x32 * lax.rsqrt(jnp.mean(x32 * x32, axis=-1, keepdims=True) + EPS)
    return (y * g.astype(jnp.float32)).astype(x.dtype)


def causal_dwconv(x, w, b):
    k_width = w.shape[0]
    s = x.shape[1]
    xp = jnp.pad(x, ((0, 0), (k_width - 1, 0), (0, 0)))
    y = b
    for k in range(k_width):
        y = y + xp[:, k:k + s] * w[k]
    return y


def _lru_combine(left, right):
    a1, b1 = left
    a2, b2 = right
    return a1 * a2, a2 * b1 + b2


def rglru_mixer(h, w_in, conv_w, conv_b, wa, ba, wx, bx, lam, w_out):
    bsz, s, _ = h.shape
    gate_br, x_br = jnp.split(h @ w_in, 2, axis=-1)
    x_br = causal_dwconv(x_br, conv_w, conv_b)
    xb = x_br.reshape(bsz, s, RG_BLOCKS, RG_BLOCK)
    r = jax.nn.sigmoid(jnp.einsum('bsgi,gij->bsgj', xb, wa).reshape(bsz, s, D_RNN) + ba)
    i_g = jax.nn.sigmoid(jnp.einsum('bsgi,gij->bsgj', xb, wx).reshape(bsz, s, D_RNN) + bx)
    log_a = -RG_C * r.astype(jnp.float32) * jax.nn.softplus(-lam.astype(jnp.float32))
    a = jnp.exp(log_a)
    mult = jnp.sqrt(-jnp.expm1(2.0 * log_a))
    u = mult * (i_g * x_br).astype(jnp.float32)
    _, hs = lax.associative_scan(_lru_combine, (a, u), axis=1)
    y = jax.nn.gelu(gate_br) * hs.astype(h.dtype)
    return y @ w_out


def gla_chunk_scan(q, k, v, g):
    n_c, bsz, nh, cl, dk = q.shape
    dv = v.shape[-1]
    mask = jnp.tril(jnp.ones((cl, cl), dtype=bool))[:, :, None]

    def step(state, inp):
        qc, kc, vc, gc = inp
        big_g = jnp.cumsum(gc, axis=2)
        o_inter = jnp.einsum('bhcd,bhde->bhce', qc * jnp.exp(big_g), state)
        diff = big_g[:, :, :, None, :] - big_g[:, :, None, :, :]
        decay = jnp.exp(jnp.where(mask, diff, -jnp.inf))
        attn = jnp.einsum('bhid,bhjd,bhijd->bhij', qc, kc, decay)
        o_intra = jnp.einsum('bhij,bhje->bhie', attn, vc)
        g_last = big_g[:, :, -1:, :]
        state = jnp.exp(g_last[:, :, 0, :])[..., None] * state + jnp.einsum(
            'bhjd,bhje->bhde', kc * jnp.exp(g_last - big_g), vc)
        return state, o_inter + o_intra

    state0 = jnp.zeros((bsz, nh, dk, dv), jnp.float32)
    _, o = lax.scan(step, state0, (q, k, v, g))
    return o


def gla_mixer(h, w_in, w_alpha, b_alpha, norm_g, w_out):
    bsz, s, _ = h.shape
    proj = h @ w_in
    q, k, v, r, z = jnp.split(proj, [GLA_QK, 2 * GLA_QK, 2 * GLA_QK + D_MODEL,
                                     2 * GLA_QK + 2 * D_MODEL], axis=-1)
    log_alpha = jax.nn.log_sigmoid((z @ w_alpha + b_alpha).astype(jnp.float32)) / GLA_TAU
    n_c = s // GLA_CHUNK

    def to_chunks(t, d):
        return t.reshape(bsz, n_c, GLA_CHUNK, GLA_HEADS, d).transpose(1, 0, 3, 2, 4).astype(jnp.float32)

    o = gla_chunk_scan(to_chunks(q * (GLA_DK ** -0.5), GLA_DK), to_chunks(k, GLA_DK),
                       to_chunks(v, GLA_DV), to_chunks(log_alpha, GLA_DK))
    o = o.transpose(1, 0, 3, 2, 4).reshape(bsz, s, GLA_HEADS, GLA_DV)
    o = rmsnorm(o, norm_g).reshape(bsz, s, D_MODEL).astype(h.dtype)
    return (o * jax.nn.silu(r)) @ w_out


def conv_ffn(h, w_up, conv_w, conv_b, w_down):
    u = causal_dwconv(h @ w_up, conv_w, conv_b)
    g, val = jnp.split(u, 2, axis=-1)
    return (jax.nn.gelu(g) * val) @ w_down


def setup_inputs(seed: int = 0) -> dict:
    key = jax.random.key(seed)
    ks = iter(jax.random.split(key, 32))

    def nrm(shape, scale):
        return jax.random.normal(next(ks), shape, jnp.float32) * scale

    d = D_MODEL
    x = nrm((BATCH, SEQ, d), 1.0)
    c = nrm((BATCH, d), 1.0)
    ada_w = nrm((DEPTH, d, 6 * d), d ** -0.5)
    ada_b = nrm((DEPTH, 6 * d), 0.01)
    norm_g = 1.0 + nrm((DEPTH, 4, d), 0.05)
    ffn_w_up = nrm((DEPTH, d, 2 * D_FF), d ** -0.5)
    ffn_conv_w = nrm((DEPTH, FFN_CONV, 2 * D_FF), FFN_CONV ** -0.5)
    ffn_conv_b = nrm((DEPTH, 2 * D_FF), 0.01)
    ffn_w_down = nrm((DEPTH, D_FF, d), D_FF ** -0.5)
    rg_w_in = nrm((N_A, d, 2 * D_RNN), d ** -0.5)
    rg_conv_w = nrm((N_A, RG_CONV, D_RNN), RG_CONV ** -0.5)
    rg_conv_b = nrm((N_A, D_RNN), 0.01)
    rg_wa = nrm((N_A, RG_BLOCKS, RG_BLOCK, RG_BLOCK), RG_BLOCK ** -0.5)
    rg_ba = nrm((N_A, D_RNN), 0.01)
    rg_wx = nrm((N_A, RG_BLOCKS, RG_BLOCK, RG_BLOCK), RG_BLOCK ** -0.5)
    rg_bx = nrm((N_A, D_RNN), 0.01)
    a_pow = jax.random.uniform(next(ks), (N_A, D_RNN), jnp.float32, 0.9, 0.999)
    a_base = a_pow ** (1.0 / RG_C)
    rg_lambda = jnp.log(a_base) - jnp.log1p(-a_base)
    rg_w_out = nrm((N_A, D_RNN, d), D_RNN ** -0.5)
    gla_w_in = nrm((N_B, d, GLA_IN), d ** -0.5)
    gla_w_alpha = nrm((N_B, GLA_RANK, GLA_QK), GLA_RANK ** -0.5)
    gla_b_alpha = nrm((N_B, GLA_QK), 0.01)
    gla_norm_g = 1.0 + nrm((N_B, GLA_DV), 0.05)
    gla_w_out = nrm((N_B, d, d), d ** -0.5)
    return {"x": x, "c": c, "ada_w": ada_w, "ada_b": ada_b, "norm_g": norm_g,
            "ffn_w_up": ffn_w_up, "ffn_conv_w": ffn_conv_w, "ffn_conv_b": ffn_conv_b,
            "ffn_w_down": ffn_w_down, "rg_w_in": rg_w_in, "rg_conv_w": rg_conv_w,
            "rg_conv_b": rg_conv_b, "rg_wa": rg_wa, "rg_ba": rg_ba, "rg_wx": rg_wx,
            "rg_bx": rg_bx, "rg_lambda": rg_lambda, "rg_w_out": rg_w_out,
            "gla_w_in": gla_w_in, "gla_w_alpha": gla_w_alpha, "gla_b_alpha": gla_b_alpha,
            "gla_norm_g": gla_norm_g, "gla_w_out": gla_w_out}


def reference(x, c, ada_w, ada_b, norm_g, ffn_w_up, ffn_conv_w, ffn_conv_b, ffn_w_down,
              rg_w_in, rg_conv_w, rg_conv_b, rg_wa, rg_ba, rg_wx, rg_bx, rg_lambda, rg_w_out,
              gla_w_in, gla_w_alpha, gla_b_alpha, gla_norm_g, gla_w_out):
    c_act = jax.nn.silu(c)
    for i in range(DEPTH):
        mod = (c_act @ ada_w[i] + ada_b[i])[:, None, :]
        sh_m, sc_m, gt_m, sh_f, sc_f, gt_f = jnp.split(mod, 6, axis=-1)
        j = i // N_MIXERS
        h = rmsnorm(x, norm_g[i, 0]) * (1.0 + sc_m) + sh_m
        if i % N_MIXERS == 0:
            y = rglru_mixer(h, rg_w_in[j], rg_conv_w[j], rg_conv_b[j], rg_wa[j], rg_ba[j],
                            rg_wx[j], rg_bx[j], rg_lambda[j], rg_w_out[j])
        else:
            y = gla_mixer(h, gla_w_in[j], gla_w_alpha[j], gla_b_alpha[j], gla_norm_g[j], gla_w_out[j])
        x = x + gt_m * rmsnorm(y, norm_g[i, 1])
        h = rmsnorm(x, norm_g[i, 2]) * (1.0 + sc_f) + sh_f
        y = conv_ffn(h, ffn_w_up[i], ffn_conv_w[i], ffn_conv_b[i], ffn_w_down[i])
        x = x + gt_f * rmsnorm(y, norm_g[i, 3])
    return x
```

```python
import functools
import math

import jax
import jax.numpy as jnp
import numpy as np
from jax import lax
from jax.experimental import pallas as pl
from jax.experimental.pallas import tpu as pltpu

F32 = jnp.float32
BF16 = jnp.bfloat16

EPS = 1e-6
RG_BLOCKS = 4
RG_C = 8.0
GLA_HEADS = 4
GLA_TAU = 16.0
GLA_RANK_PAD = 128
SUBLANES = 8

ROW_TILE = 256
GLA_CHUNK = 256
FFN_COL_CHUNK = 256
VMEM_LIMIT = 56 * 1024 * 1024


def _dot(a, b):
    return jnp.dot(a, b, preferred_element_type=F32)


def _rms(x, g):
    return x * lax.rsqrt(jnp.mean(x * x, axis=-1, keepdims=True) + EPS) * g


def _shift_rows(ext, k):
    return pltpu.roll(ext, k, 0)[SUBLANES:, :]


def _params(*sem):
    return pltpu.CompilerParams(dimension_semantics=sem, vmem_limit_bytes=VMEM_LIMIT)


def _const_spec(shape):
    n = len(shape)
    return pl.BlockSpec(shape, lambda *_: (0,) * n, pipeline_mode=pl.Buffered(1))


def _ada_kernel(c_ref, w_ref, b_ref, o_ref):
    c = c_ref[...]
    c_act = (c * jax.nn.sigmoid(c)).astype(BF16)
    o_ref[...] = _dot(c_act, w_ref[...].astype(BF16)) + b_ref[...]


def _ada_modulation(c, ada_w, ada_b):
    depth, d, n = ada_w.shape
    bsz = c.shape[0]
    c_pad = jnp.zeros((SUBLANES, d), F32).at[:bsz].set(c)
    tn = 1024
    out = pl.pallas_call(
        _ada_kernel,
        grid=(depth, n // tn),
        in_specs=[
            pl.BlockSpec((SUBLANES, d), lambda l, j: (0, 0)),
            pl.BlockSpec((None, d, tn), lambda l, j: (l, 0, j)),
            pl.BlockSpec((None, 1, tn), lambda l, j: (l, 0, j)),
        ],
        out_specs=pl.BlockSpec((None, SUBLANES, tn), lambda l, j: (l, 0, j)),
        out_shape=jax.ShapeDtypeStruct((depth, SUBLANES, n), F32),
        compiler_params=_params("arbitrary", "arbitrary"),
        name="ada_mod",
    )(c_pad, ada_w, ada_b.reshape(depth, 1, n))
    return out[:, :bsz].reshape(depth, bsz, 6, d)


def _rglru_kernel(x_ref, mod_ref, ng_ref, win_ref, cw_ref, vec_ref, wg_ref, wout_ref, o_ref,
                  tail_ref, hc_ref, a_ref, b_ref, hs_ref):
    ts, d = x_ref.shape
    blk = d // RG_BLOCKS

    @pl.when(pl.program_id(1) == 0)
    def _():
        tail_ref[...] = jnp.zeros_like(tail_ref)
        hc_ref[...] = jnp.zeros_like(hc_ref)

    x = x_ref[...]
    h = _rms(x, ng_ref[0:1, :]) * (1.0 + mod_ref[1:2, :]) + mod_ref[0:1, :]
    proj = _dot(h.astype(BF16), win_ref[...])
    gate_br = proj[:, :d]
    xb = proj[:, d:]

    ext = jnp.concatenate([tail_ref[...], xb], axis=0)
    tail_ref[...] = xb[ts - SUBLANES:, :]
    xc = vec_ref[0:1, :] + _shift_rows(ext, 3) * cw_ref[0:1, :]
    xc = xc + _shift_rows(ext, 2) * cw_ref[1:2, :]
    xc = xc + _shift_rows(ext, 1) * cw_ref[2:3, :]
    xc = xc + xb * cw_ref[3:4, :]

    xcb = xc.astype(BF16)
    ra, ia = [], []
    for g in range(RG_BLOCKS):
        p = _dot(xcb[:, g * blk:(g + 1) * blk], wg_ref[g])
        ra.append(p[:, :blk])
        ia.append(p[:, blk:])
    r = jax.nn.sigmoid(jnp.concatenate(ra, axis=1) + vec_ref[1:2, :])
    i_g = jax.nn.sigmoid(jnp.concatenate(ia, axis=1) + vec_ref[2:3, :])

    nl = -vec_ref[3:4, :]
    softplus_nl = jnp.maximum(nl, 0.0) + jnp.log1p(jnp.exp(-jnp.abs(nl)))
    log_a = (-RG_C) * r * softplus_nl
    a = jnp.exp(log_a)
    t = jnp.tanh(log_a)
    mult = jnp.sqrt(-2.0 * t / (1.0 - t))
    u = mult * (i_g * xc)

    row = lax.broadcasted_iota(jnp.int32, (ts, 1), 0) % SUBLANES
    for k in (1, 2, 4):
        keep = row >= k
        a_sh = jnp.where(keep, pltpu.roll(a, k, 0), 1.0)
        u_sh = jnp.where(keep, pltpu.roll(u, k, 0), 0.0)
        u = a * u_sh + u
        a = a * a_sh
    a_ref[...] = a
    b_ref[...] = u

    def group(j, carry):
        rows = pl.ds(pl.multiple_of(j * SUBLANES, SUBLANES), SUBLANES)
        hg = a_ref[rows, :] * carry + b_ref[rows, :]
        hs_ref[rows, :] = hg
        return hg[SUBLANES - 1:SUBLANES, :]

    hc_ref[...] = lax.fori_loop(0, ts // SUBLANES, group, hc_ref[...])

    y = jax.nn.gelu(gate_br) * hs_ref[...]
    out = _dot(y.astype(BF16), wout_ref[...])
    o_ref[...] = x + mod_ref[2:3, :] * _rms(out, ng_ref[1:2, :])


def _rglru_layer(x, mod, ng, w_in, conv_w, vec, w_gate, w_out):
    bsz, s, d = x.shape
    ts = ROW_TILE
    tile = pl.BlockSpec((None, ts, d), lambda b, i: (b, i, 0))
    return pl.pallas_call(
        _rglru_kernel,
        grid=(bsz, s // ts),
        in_specs=[
            tile,
            pl.BlockSpec((None, 6, d), lambda b, i: (b, 0, 0)),
            _const_spec(ng.shape),
            _const_spec(w_in.shape),
            _const_spec(conv_w.shape),
            _const_spec(vec.shape),
            _const_spec(w_gate.shape),
            _const_spec(w_out.shape),
        ],
        out_specs=tile,
        out_shape=jax.ShapeDtypeStruct(x.shape, F32),
        scratch_shapes=[
            pltpu.VMEM((SUBLANES, d), F32),
            pltpu.VMEM((1, d), F32),
            pltpu.VMEM((ts, d), F32),
            pltpu.VMEM((ts, d), F32),
            pltpu.VMEM((ts, d), F32),
        ],
        compiler_params=_params("arbitrary", "arbitrary"),
        name="rglru_layer",
    )(x, mod, ng, w_in, conv_w, vec, w_gate, w_out)


def _ffn_kernel(x_ref, mod_ref, ng_ref, wup_ref, cw_ref, cb_ref, wdn_ref, o_ref, tail_ref):
    ts, d = x_ref.shape
    f = wdn_ref.shape[0]
    fc = FFN_COL_CHUNK

    @pl.when(pl.program_id(1) == 0)
    def _():
        tail_ref[...] = jnp.zeros_like(tail_ref)

    x = x_ref[...]
    h = (_rms(x, ng_ref[2:3, :]) * (1.0 + mod_ref[4:5, :]) + mod_ref[3:4, :]).astype(BF16)

    def conv_cols(c0):
        cols = slice(c0, c0 + fc)
        up = _dot(h, wup_ref[:, cols])
        ext = jnp.concatenate([tail_ref[:, cols], up], axis=0)
        tail_ref[:, cols] = up[ts - SUBLANES:, :]
        y = cb_ref[:, cols] + _shift_rows(ext, 2) * cw_ref[0:1, cols]
        y = y + _shift_rows(ext, 1) * cw_ref[1:2, cols]
        return y + up * cw_ref[2:3, cols]

    acc = jnp.zeros((ts, d), F32)
    for c in range(f // fc):
        g = conv_cols(c * fc)
        val = conv_cols(f + c * fc)
        act = (jax.nn.gelu(g) * val).astype(BF16)
        acc = acc + _dot(act, wdn_ref[c * fc:(c + 1) * fc, :])
    o_ref[...] = x + mod_ref[5:6, :] * _rms(acc, ng_ref[3:4, :])


def _ffn_layer(x, mod, ng, w_up, conv_w, conv_b, w_down):
    bsz, s, d = x.shape
    ts = ROW_TILE
    tile = pl.BlockSpec((None, ts, d), lambda b, i: (b, i, 0))
    return pl.pallas_call(
        _ffn_kernel,
        grid=(bsz, s // ts),
        in_specs=[
            tile,
            pl.BlockSpec((None, 6, d), lambda b, i: (b, 0, 0)),
            _const_spec(ng.shape),
            _const_spec(w_up.shape),
            _const_spec(conv_w.shape),
            _const_spec(conv_b.shape),
            _const_spec(w_down.shape),
        ],
        out_specs=tile,
        out_shape=jax.ShapeDtypeStruct(x.shape, F32),
        scratch_shapes=[pltpu.VMEM((SUBLANES, w_up.shape[1]), F32)],
        compiler_params=_params("arbitrary", "arbitrary"),
        name="ffn_layer",
    )(x, mod, ng, w_up, conv_w, conv_b, w_down)


def _gla_proj_kernel(x_ref, mod_ref, ng_ref, wm_ref, wz_ref, wal_ref, bal_ref,
                     q_ref, k_ref, g_ref, v_ref, r_ref):
    d = x_ref.shape[1]
    dk = q_ref.shape[2]
    dv = v_ref.shape[2]
    qk = GLA_HEADS * dk
    x = x_ref[...]
    h = (_rms(x, ng_ref[0:1, :]) * (1.0 + mod_ref[1:2, :]) + mod_ref[0:1, :]).astype(BF16)
    proj = _dot(h, wm_ref[...])
    z = _dot(h, wz_ref[...])
    ga = _dot(z.astype(BF16), wal_ref[...]) + bal_ref[...]
    glog = (jnp.minimum(ga, 0.0) - jnp.log1p(jnp.exp(-jnp.abs(ga)))) / GLA_TAU
    scale = dk ** -0.5
    for hh in range(GLA_HEADS):
        q_ref[hh] = proj[:, hh * dk:(hh + 1) * dk] * scale
        k_ref[hh] = proj[:, qk + hh * dk:qk + (hh + 1) * dk]
        g_ref[hh] = glog[:, hh * dk:(hh + 1) * dk]
        v_ref[hh] = proj[:, 2 * qk + hh * dv:2 * qk + (hh + 1) * dv]
        r_ref[hh] = proj[:, 2 * qk + d + hh * dv:2 * qk + d + (hh + 1) * dv]


def _gla_proj(x, mod, ng, w_main, w_z, w_alpha, b_alpha):
    bsz, s, d = x.shape
    ts = ROW_TILE
    dk = w_alpha.shape[1] // GLA_HEADS
    dv = d // GLA_HEADS
    head_spec = lambda w: pl.BlockSpec((None, GLA_HEADS, ts, w), lambda b, i: (b, 0, i, 0))
    head_shape = lambda w: jax.ShapeDtypeStruct((bsz, GLA_HEADS, s, w), F32)
    return pl.pallas_call(
        _gla_proj_kernel,
        grid=(bsz, s // ts),
        in_specs=[
            pl.BlockSpec((None, ts, d), lambda b, i: (b, i, 0)),
            pl.BlockSpec((None, 6, d), lambda b, i: (b, 0, 0)),
            _const_spec(ng.shape),
            _const_spec(w_main.shape),
            _const_spec(w_z.shape),
            _const_spec(w_alpha.shape),
            _const_spec(b_alpha.shape),
        ],
        out_specs=[head_spec(dk), head_spec(dk), head_spec(dk), head_spec(dv), head_spec(dv)],
        out_shape=[head_shape(dk), head_shape(dk), head_shape(dk), head_shape(dv), head_shape(dv)],
        compiler_params=_params("arbitrary", "arbitrary"),
        name="gla_proj",
    )(x, mod, ng, w_main, w_z, w_alpha, b_alpha)


def _gla_levels(c):
    levels = []
    b = c // 2
    while b >= 1:
        levels.append(b)
        b //= 2
    return levels


def _gla_sum_matrix(c):
    i = np.arange(c)[:, None]
    l = np.arange(c)[None, :]
    blocks = [l <= i, l > i]
    for b in _gla_levels(c):
        mid = (i // (2 * b)) * (2 * b) + b
        blocks.append(np.where(i >= mid, (l >= mid) & (l <= i), (l > i) & (l < mid)))
    return np.concatenate(blocks, axis=0).astype(np.float32)


def _gla_pair_masks(c):
    i = np.arange(c)[:, None]
    j = np.arange(c)[None, :]
    out = []
    for b in _gla_levels(c):
        out.append((i // (2 * b) == j // (2 * b)) & (i // b > j // b))
    return np.stack(out).astype(np.float32)


def _gla_core_kernel(q_ref, k_ref, g_ref, v_ref, r_ref, x_ref, mod_ref, ng_ref, hn_ref, sm_ref, pm_ref,
                     wout_ref, o_ref, e_ref, state_ref, acc_ref):
    c, dk = q_ref.shape
    dv = v_ref.shape[1]
    hd = pl.program_id(2)
    levels = _gla_levels(c)

    @pl.when(hd == 0)
    def _():
        g = jnp.concatenate([g_ref[hh] for hh in range(GLA_HEADS)], axis=1)
        g_hi = g.astype(BF16)
        g_lo = (g - g_hi.astype(F32)).astype(BF16)
        sm = sm_ref[...]
        e = jnp.exp(_dot(sm, g_hi) + _dot(sm, g_lo))
        for hh in range(GLA_HEADS):
            e_ref[hh] = e[:, hh * dk:(hh + 1) * dk]
        acc_ref[...] = jnp.zeros_like(acc_ref)

    @pl.when((hd == 0) & (pl.program_id(1) == 0))
    def _():
        state_ref[...] = jnp.zeros_like(state_ref)

    q = q_ref[...]
    k = k_ref[...]
    v = v_ref[...].astype(BF16)
    state = state_ref[hd]

    e_cum = e_ref[hd, 0:c, :]
    e_rev = e_ref[hd, c:2 * c, :]
    o = _dot((q * e_cum).astype(BF16), state.astype(BF16))

    row = lax.broadcasted_iota(jnp.int32, (c, 1), 0)
    attn = jnp.zeros((c, c), F32)
    for li, b in enumerate(levels):
        e_b = e_ref[hd, (2 + li) * c:(3 + li) * c, :]
        upper = (row // b) % 2 == 1
        q_b = jnp.where(upper, q * e_b, 0.0).astype(BF16)
        k_b = jnp.where(upper, 0.0, k * e_b).astype(BF16)
        p = lax.dot_general(q_b, k_b, (((1,), (1,)), ((), ())), preferred_element_type=F32)
        attn = attn + p * pm_ref[li]
    o = o + _dot(attn.astype(BF16), v)
    o = o + jnp.sum(q * k, axis=-1, keepdims=True) * v_ref[...]

    k_rev = (k * e_rev).astype(BF16)
    upd = lax.dot_general(k_rev, v, (((0,), (0,)), ((), ())), preferred_element_type=F32)
    decay_rows = jnp.broadcast_to(e_cum[c - 1:c, :], (dk, dk))
    decay_col = jnp.transpose(decay_rows)[:, 0:1]
    state_ref[hd] = decay_col * state + upd

    r = r_ref[...]
    o = _rms(o, hn_ref[...]) * (r * jax.nn.sigmoid(r))
    w_rows = pl.ds(pl.multiple_of(hd * dv, dv), dv)
    acc_ref[...] += _dot(o.astype(BF16), wout_ref[w_rows, :])

    @pl.when(hd == GLA_HEADS - 1)
    def _():
        o_ref[...] = x_ref[...] + mod_ref[2:3, :] * _rms(acc_ref[...], ng_ref[1:2, :])


def _gla_core(q, k, g, v, r, x, mod, ng, head_norm, w_out):
    bsz, s, d = x.shape
    c = GLA_CHUNK
    dk = q.shape[3]
    dv = v.shape[3]
    sum_mat = jnp.asarray(_gla_sum_matrix(c), BF16)
    pair_masks = jnp.asarray(_gla_pair_masks(c), F32)
    head_spec = lambda w: pl.BlockSpec((None, None, c, w), lambda b, i, h: (b, h, i, 0))
    tile = pl.BlockSpec((None, c, d), lambda b, i, h: (b, i, 0))
    return pl.pallas_call(
        _gla_core_kernel,
        grid=(bsz, s // c, GLA_HEADS),
        in_specs=[
            head_spec(dk),
            head_spec(dk),
            pl.BlockSpec((None, GLA_HEADS, c, dk), lambda b, i, h: (b, 0, i, 0)),
            head_spec(dv),
            head_spec(dv),
            tile,
            pl.BlockSpec((None, 6, d), lambda b, i, h: (b, 0, 0)),
            _const_spec(ng.shape),
            _const_spec(head_norm.shape),
            _const_spec(sum_mat.shape),
            _const_spec(pair_masks.shape),
            _const_spec(w_out.shape),
        ],
        out_specs=tile,
        out_shape=jax.ShapeDtypeStruct(x.shape, F32),
        scratch_shapes=[
            pltpu.VMEM((GLA_HEADS, sum_mat.shape[0], dk), F32),
            pltpu.VMEM((GLA_HEADS, dk, dv), F32),
            pltpu.VMEM((c, d), F32),
        ],
        compiler_params=_params("arbitrary", "arbitrary", "arbitrary"),
        name="gla_core",
    )(q, k, g, v, r, x, mod, ng, head_norm, sum_mat, pair_masks, w_out)


def kernel(x, c, ada_w, ada_b, norm_g, ffn_w_up, ffn_conv_w, ffn_conv_b, ffn_w_down, rg_w_in, rg_conv_w,
           rg_conv_b, rg_wa, rg_ba, rg_wx, rg_bx, rg_lambda, rg_w_out, gla_w_in, gla_w_alpha, gla_b_alpha,
           gla_norm_g, gla_w_out):
    depth = ada_w.shape[0]
    d = x.shape[2]
    mod = _ada_modulation(c, ada_w, ada_b)
    for i in range(depth):
        j = i // 2
        if i % 2 == 0:
            vec = jnp.stack([rg_conv_b[j], rg_ba[j], rg_bx[j], rg_lambda[j]])
            w_gate = jnp.concatenate([rg_wa[j], rg_wx[j]], axis=-1).astype(BF16)
            x = _rglru_layer(x, mod[i], norm_g[i], rg_w_in[j].astype(BF16), rg_conv_w[j], vec, w_gate,
                             rg_w_out[j].astype(BF16))
        else:
            qk = gla_w_alpha.shape[2]
            n_main = 2 * qk + 2 * d
            rank = gla_w_in.shape[2] - n_main
            w_main = gla_w_in[j][:, :n_main].astype(BF16)
            w_z = jnp.zeros((d, GLA_RANK_PAD), BF16).at[:, :rank].set(gla_w_in[j][:, n_main:].astype(BF16))
            w_alpha = jnp.zeros((GLA_RANK_PAD, qk), BF16).at[:rank].set(gla_w_alpha[j].astype(BF16))
            q, k, g, v, r = _gla_proj(x, mod[i], norm_g[i], w_main, w_z, w_alpha, gla_b_alpha[j][None, :])
            x = _gla_core(q, k, g, v, r, x, mod[i], norm_g[i], gla_norm_g[j][None, :],
                          gla_w_out[j].astype(BF16))
        x = _ffn_layer(x, mod[i], norm_g[i], ffn_w_up[i].astype(BF16), ffn_conv_w[i],
                       ffn_conv_b[i][None, :], ffn_w_down[i].astype(BF16))
    return x
```

```python
import functools
import math

import jax
import jax.numpy as jnp
import numpy as np
from jax import lax
from jax.experimental import pallas as pl
from jax.experimental.pallas import tpu as pltpu

F32 = jnp.float32
BF16 = jnp.bfloat16

EPS = 1e-6
RG_BLOCKS = 4
RG_C = 8.0
RG_CONV = 4
FFN_CONV = 3
GLA_HEADS = 4
GLA_TAU = 16.0
GLA_RANK_PAD = 128
SUBLANES = 8
LANES = 128

ROW_TILE = 256
TILE_VREGS = ROW_TILE // SUBLANES
FFN_COL_CHUNK = 256
VMEM_LIMIT = 56 * 1024 * 1024

_GELU_C0 = 2.0 * math.sqrt(2.0 / math.pi)
_GELU_C1 = _GELU_C0 * 0.044715


def _dot(a, b):
    return jnp.dot(a, b, preferred_element_type=F32)


def _rms(x, g):
    return x * lax.rsqrt(jnp.mean(x * x, axis=-1, keepdims=True) + EPS) * g


def _gelu_times(x, v):
    neg_2z = x * (-_GELU_C0 - _GELU_C1 * (x * x))
    return (x * v) / (1.0 + jnp.exp(neg_2z))


def _sublane(shape):
    return lax.broadcasted_iota(jnp.int32, shape, 0) % SUBLANES


def _load_tile_order(ref):
    nc = ref.shape[0] // ROW_TILE
    rows = []
    for j in range(TILE_VREGS):
        rows.append(jnp.concatenate(
            [ref[pl.ds(j * nc + c, SUBLANES, stride=TILE_VREGS * nc), :] for c in range(nc)], axis=1))
    return jnp.concatenate(rows, axis=0)


def _store_natural_order(ref, val):
    nc = ref.shape[0] // ROW_TILE
    for j in range(TILE_VREGS):
        for c in range(nc):
            ref[pl.ds(j * nc + c, SUBLANES, stride=TILE_VREGS * nc), :] = (
                val[j * SUBLANES:(j + 1) * SUBLANES, c * LANES:(c + 1) * LANES])


def _delay(x, tail, k):
    ts = x.shape[0]
    n_tail = tail.shape[0] // SUBLANES
    last = _sublane((SUBLANES, 1)) == SUBLANES - 1
    head = []
    for j in range(k):
        src = TILE_VREGS - k + j
        cur = x[src * SUBLANES:(src + 1) * SUBLANES]
        prv = tail[(n_tail - k + j) * SUBLANES:(n_tail - k + j + 1) * SUBLANES]
        head.append(pltpu.roll(jnp.where(last, prv, cur), 1, 0))
    return jnp.concatenate(head + [x[:ts - k * SUBLANES]], axis=0)


def _params(*sem):
    return pltpu.CompilerParams(dimension_semantics=sem, vmem_limit_bytes=VMEM_LIMIT)


def _const_spec(shape):
    n = len(shape)
    return pl.BlockSpec(shape, lambda *_: (0,) * n, pipeline_mode=pl.Buffered(1))


def _ada_kernel(c_ref, w_ref, b_ref, o_ref):
    c = c_ref[...]
    c_act = (c * jax.nn.sigmoid(c)).astype(BF16)
    o_ref[...] = _dot(c_act, w_ref[...].astype(BF16)) + b_ref[...]


def _ada_modulation(c, ada_w, ada_b):
    depth, d, n = ada_w.shape
    bsz = c.shape[0]
    c_pad = jnp.zeros((SUBLANES, d), F32).at[:bsz].set(c)
    tn = 1024
    out = pl.pallas_call(
        _ada_kernel,
        grid=(depth, n // tn),
        in_specs=[
            pl.BlockSpec((SUBLANES, d), lambda l, j: (0, 0)),
            pl.BlockSpec((None, d, tn), lambda l, j: (l, 0, j)),
            pl.BlockSpec((None, 1, tn), lambda l, j: (l, 0, j)),
        ],
        out_specs=pl.BlockSpec((None, SUBLANES, tn), lambda l, j: (l, 0, j)),
        out_shape=jax.ShapeDtypeStruct((depth, SUBLANES, n), F32),
        compiler_params=_params("arbitrary", "arbitrary"),
        name="ada_mod",
    )(c_pad, ada_w, ada_b.reshape(depth, 1, n))
    return out[:, :bsz].reshape(depth, bsz, 6, d)


def _stream_scan(a, u, carry):
    vreg = lambda x, j: x[j * SUBLANES:(j + 1) * SUBLANES]
    hl = [vreg(u, 0)]
    al = [vreg(a, 0)]
    for j in range(1, TILE_VREGS):
        hl.append(vreg(a, j) * hl[-1] + vreg(u, j))
        al.append(vreg(a, j) * al[-1])
    sub = _sublane((SUBLANES, 1))
    ea, eh = al[-1], hl[-1]
    for k in (1, 2, 4):
        keep = sub >= k
        ea_sh = jnp.where(keep, pltpu.roll(ea, k, 0), 1.0)
        eh_sh = jnp.where(keep, pltpu.roll(eh, k, 0), 0.0)
        eh = ea * eh_sh + eh
        ea = ea * ea_sh
    after = ea * carry + eh
    before = jnp.where(sub == 0, carry, pltpu.roll(after, 1, 0))
    h = jnp.concatenate([hl[j] + al[j] * before for j in range(TILE_VREGS)], axis=0)
    return h, after


def _rglru_kernel(natural_in, x_ref, mod_ref, ng_ref, win_ref, cw_ref, vec_ref, wg_ref, wout_ref, o_ref,
                  tail_ref, hc_ref):
    ts, d = o_ref.shape
    blk = d // RG_BLOCKS
    n_tail = RG_CONV - 1

    @pl.when(pl.program_id(1) == 0)
    def _():
        tail_ref[...] = jnp.zeros_like(tail_ref)
        hc_ref[...] = jnp.zeros_like(hc_ref)

    x = _load_tile_order(x_ref) if natural_in else x_ref[...]
    h = (_rms(x, ng_ref[0:1, :]) * (1.0 + mod_ref[1:2, :]) + mod_ref[0:1, :]).astype(BF16)

    def project(g):
        return (_dot(h, win_ref[:, g * blk:(g + 1) * blk]),
                _dot(h, win_ref[:, d + g * blk:d + (g + 1) * blk]))

    def mix(g, gate_br, xb):
        cols = slice(g * blk, (g + 1) * blk)
        tail = tail_ref[:, cols]
        tail_ref[:, cols] = xb[ts - n_tail * SUBLANES:, :]
        xc = vec_ref[0:1, cols] + _delay(xb, tail, 3) * cw_ref[0:1, cols]
        xc = xc + _delay(xb, tail, 2) * cw_ref[1:2, cols]
        xc = xc + _delay(xb, tail, 1) * cw_ref[2:3, cols]
        xc = xc + xb * cw_ref[3:4, cols]
        p = _dot(xc.astype(BF16), wg_ref[g])
        r = jax.nn.sigmoid(p[:, :blk] + vec_ref[1:2, cols])
        i_g = jax.nn.sigmoid(p[:, blk:] + vec_ref[2:3, cols])
        nl = -vec_ref[3:4, cols]
        softplus_nl = jnp.maximum(nl, 0.0) + jnp.log1p(jnp.exp(-jnp.abs(nl)))
        log_a = (-RG_C) * r * softplus_nl
        a = jnp.exp(log_a)
        t = jnp.tanh(log_a)
        u = jnp.sqrt(-2.0 * t / (1.0 - t)) * (i_g * xc)
        hs, after = _stream_scan(a, u, hc_ref[SUBLANES - 1:SUBLANES, cols])
        hc_ref[:, cols] = after
        return _gelu_times(gate_br, hs).astype(BF16)

    acc = jnp.zeros((ts, d), F32)
    cur = project(0)
    y = None
    for g in range(RG_BLOCKS):
        nxt = project(g + 1) if g + 1 < RG_BLOCKS else None
        if y is not None:
            acc = acc + _dot(y, wout_ref[(g - 1) * blk:g * blk, :])
        y = mix(g, *cur)
        cur = nxt
    acc = acc + _dot(y, wout_ref[(RG_BLOCKS - 1) * blk:, :])
    o_ref[...] = x + mod_ref[2:3, :] * _rms(acc, ng_ref[1:2, :])


def _rglru_layer(x, mod, ng, w_in, conv_w, vec, w_gate, w_out, natural_in):
    bsz, s, d = x.shape
    ts = ROW_TILE
    tile = pl.BlockSpec((None, ts, d), lambda b, i: (b, i, 0))
    x_tile = tile
    if natural_in:
        nc = d // LANES
        x = x.reshape(bsz, s * nc, LANES)
        x_tile = pl.BlockSpec((None, ts * nc, LANES), lambda b, i: (b, i, 0))
    return pl.pallas_call(
        functools.partial(_rglru_kernel, natural_in),
        grid=(bsz, s // ts),
        in_specs=[
            x_tile,
            pl.BlockSpec((None, 6, d), lambda b, i: (b, 0, 0)),
            _const_spec(ng.shape),
            _const_spec(w_in.shape),
            _const_spec(conv_w.shape),
            _const_spec(vec.shape),
            _const_spec(w_gate.shape),
            _const_spec(w_out.shape),
        ],
        out_specs=tile,
        out_shape=jax.ShapeDtypeStruct((bsz, s, d), F32),
        scratch_shapes=[
            pltpu.VMEM(((RG_CONV - 1) * SUBLANES, d), F32),
            pltpu.VMEM((SUBLANES, d), F32),
        ],
        compiler_params=_params("arbitrary", "arbitrary"),
        name="rglru_layer",
    )(x, mod, ng, w_in, conv_w, vec, w_gate, w_out)


def _ffn_kernel(natural_out, x_ref, mod_ref, ng_ref, wup_ref, cw_ref, cb_ref, wdn_ref, o_ref, tail_ref):
    ts, d = x_ref.shape
    f = wdn_ref.shape[0]
    fc = FFN_COL_CHUNK
    n_tail = FFN_CONV - 1

    @pl.when(pl.program_id(1) == 0)
    def _():
        tail_ref[...] = jnp.zeros_like(tail_ref)

    x = x_ref[...]
    h = (_rms(x, ng_ref[2:3, :]) * (1.0 + mod_ref[4:5, :]) + mod_ref[3:4, :]).astype(BF16)

    def up_cols(c0):
        return _dot(h, wup_ref[:, c0:c0 + fc])

    def conv_cols(up, c0):
        cols = slice(c0, c0 + fc)
        tail = tail_ref[:, cols]
        tail_ref[:, cols] = up[ts - n_tail * SUBLANES:, :]
        y = cb_ref[:, cols] + _delay(up, tail, 2) * cw_ref[0:1, cols]
        y = y + _delay(up, tail, 1) * cw_ref[1:2, cols]
        return y + up * cw_ref[2:3, cols]

    n_chunks = f // fc
    acc = jnp.zeros((ts, d), F32)
    ups = (up_cols(0), up_cols(f))
    act = None
    for c in range(n_chunks):
        nxt = (up_cols((c + 1) * fc), up_cols(f + (c + 1) * fc)) if c + 1 < n_chunks else None
        if act is not None:
            acc = acc + _dot(act, wdn_ref[(c - 1) * fc:c * fc, :])
        g = conv_cols(ups[0], c * fc)
        val = conv_cols(ups[1], f + c * fc)
        act = _gelu_times(g, val).astype(BF16)
        ups = nxt
    acc = acc + _dot(act, wdn_ref[(n_chunks - 1) * fc:n_chunks * fc, :])
    out = x + mod_ref[5:6, :] * _rms(acc, ng_ref[3:4, :])
    if natural_out:
        _store_natural_order(o_ref, out)
    else:
        o_ref[...] = out


def _ffn_layer(x, mod, ng, w_up, conv_w, conv_b, w_down, natural_out):
    bsz, s, d = x.shape
    ts = ROW_TILE
    tile = pl.BlockSpec((None, ts, d), lambda b, i: (b, i, 0))
    out_tile, out_shape = tile, (bsz, s, d)
    if natural_out:
        nc = d // LANES
        out_tile = pl.BlockSpec((None, ts * nc, LANES), lambda b, i: (b, i, 0))
        out_shape = (bsz, s * nc, LANES)
    out = pl.pallas_call(
        functools.partial(_ffn_kernel, natural_out),
        grid=(bsz, s // ts),
        in_specs=[
            tile,
            pl.BlockSpec((None, 6, d), lambda b, i: (b, 0, 0)),
            _const_spec(ng.shape),
            _const_spec(w_up.shape),
            _const_spec(conv_w.shape),
            _const_spec(conv_b.shape),
            _const_spec(w_down.shape),
        ],
        out_specs=out_tile,
        out_shape=jax.ShapeDtypeStruct(out_shape, F32),
        scratch_shapes=[
            pltpu.VMEM(((FFN_CONV - 1) * SUBLANES, w_up.shape[1]), F32),
        ],
        compiler_params=_params("arbitrary", "arbitrary"),
        name="ffn_layer",
    )(x, mod, ng, w_up, conv_w, conv_b, w_down)
    return out.reshape(bsz, s, d)


def _gla_proj_kernel(x_ref, mod_ref, ng_ref, wm_ref, wz_ref, wal_ref, bal_ref,
                     q_ref, k_ref, g_ref, v_ref, r_ref):
    d = x_ref.shape[1]
    dk = q_ref.shape[2]
    dv = v_ref.shape[2]
    qk = GLA_HEADS * dk
    x = x_ref[...]
    h = (_rms(x, ng_ref[0:1, :]) * (1.0 + mod_ref[1:2, :]) + mod_ref[0:1, :]).astype(BF16)
    proj = _dot(h, wm_ref[...])
    z = _dot(h, wz_ref[...])
    ga = _dot(z.astype(BF16), wal_ref[...]) + bal_ref[...]
    glog = (jnp.minimum(ga, 0.0) - jnp.log1p(jnp.exp(-jnp.abs(ga)))) / GLA_TAU
    scale = dk ** -0.5
    for hh in range(GLA_HEADS):
        q_ref[hh] = proj[:, hh * dk:(hh + 1) * dk] * scale
        k_ref[hh] = proj[:, qk + hh * dk:qk + (hh + 1) * dk]
        g_ref[hh] = glog[:, hh * dk:(hh + 1) * dk]
        v_ref[hh] = proj[:, 2 * qk + hh * dv:2 * qk + (hh + 1) * dv]
        r_ref[hh] = proj[:, 2 * qk + d + hh * dv:2 * qk + d + (hh + 1) * dv]


def _gla_proj(x, mod, ng, w_main, w_z, w_alpha, b_alpha):
    bsz, s, d = x.shape
    ts = ROW_TILE
    dk = w_alpha.shape[1] // GLA_HEADS
    dv = d // GLA_HEADS
    head_spec = lambda w: pl.BlockSpec((None, GLA_HEADS, ts, w), lambda b, i: (b, 0, i, 0))
    head_shape = lambda w: jax.ShapeDtypeStruct((bsz, GLA_HEADS, s, w), F32)
    return pl.pallas_call(
        _gla_proj_kernel,
        grid=(bsz, s // ts),
        in_specs=[
            pl.BlockSpec((None, ts, d), lambda b, i: (b, i, 0)),
            pl.BlockSpec((None, 6, d), lambda b, i: (b, 0, 0)),
            _const_spec(ng.shape),
            _const_spec(w_main.shape),
            _const_spec(w_z.shape),
            _const_spec(w_alpha.shape),
            _const_spec(b_alpha.shape),
        ],
        out_specs=[head_spec(dk), head_spec(dk), head_spec(dk), head_spec(dv), head_spec(dv)],
        out_shape=[head_shape(dk), head_shape(dk), head_shape(dk), head_shape(dv), head_shape(dv)],
        compiler_params=_params("arbitrary", "arbitrary"),
        name="gla_proj",
    )(x, mod, ng, w_main, w_z, w_alpha, b_alpha)


_STREAM_LEVELS = tuple(TILE_VREGS << i for i in (2, 1, 0))
_VREG_LEVELS = tuple(TILE_VREGS >> i for i in range(1, 6))
_MASKED_STREAM_LEVELS = _STREAM_LEVELS[1:]
_EXCLUDE = 1e30


def _tile_order_times(c):
    p = np.arange(c)
    return (p % SUBLANES) * (c // SUBLANES) + p // SUBLANES


def _gla_cumsum_matrix(c):
    t = _tile_order_times(c)
    return (t[None, :] <= t[:, None]).astype(np.float32)


def _gla_stream_masks(c):
    t = _tile_order_times(c)
    return np.stack([(t[:, None] // (2 * b) == t[None, :] // (2 * b)) for b in _MASKED_STREAM_LEVELS]
                    ).astype(np.float32)


def _vreg_level_rows(b):
    upper = [j for j in range(TILE_VREGS) if (j // b) % 2 == 1]
    lower = [j for j in range(TILE_VREGS) if (j // b) % 2 == 0]
    return upper, lower


def _gla_vreg_masks():
    out = []
    s = np.arange(SUBLANES)
    for b in _VREG_LEVELS:
        upper, lower = _vreg_level_rows(b)
        ju = np.repeat(np.array(upper), SUBLANES)[:, None]
        jl = np.repeat(np.array(lower), SUBLANES)[None, :]
        su = np.tile(s, len(upper))[:, None]
        sl = np.tile(s, len(lower))[None, :]
        out.append((su == sl) & (ju // (2 * b) == jl // (2 * b)))
    return np.stack(out).astype(np.float32)


def _dot_nt(a, b):
    return lax.dot_general(a, b, (((1,), (1,)), ((), ())), preferred_element_type=F32)


def _gla_core_kernel(q_ref, k_ref, g_ref, v_ref, r_ref, x_ref, mod_ref, ng_ref, hn_ref, cs_ref, sm_ref, vm_ref,
                     wout_ref, o_ref, gcum_ref, state_ref, acc_ref):
    c, dk = q_ref.shape
    dv = v_ref.shape[1]
    hd = pl.program_id(2)
    vreg = lambda a, j: a[j * SUBLANES:(j + 1) * SUBLANES]

    @pl.when(hd == 0)
    def _():
        g = jnp.concatenate([g_ref[hh] for hh in range(GLA_HEADS)], axis=1)
        g_hi = g.astype(BF16)
        g_lo = (g - g_hi.astype(F32)).astype(BF16)
        cs = cs_ref[...]
        gc_all = _dot(cs, g_hi) + _dot(cs, g_lo)
        for hh in range(GLA_HEADS):
            gcum_ref[hh] = gc_all[:, hh * dk:(hh + 1) * dk]
        acc_ref[...] = jnp.zeros_like(acc_ref)

    @pl.when((hd == 0) & (pl.program_id(1) == 0))
    def _():
        state_ref[...] = jnp.zeros_like(state_ref)

    q = q_ref[...]
    k = k_ref[...]
    v32 = v_ref[...]
    v = v32.astype(BF16)
    state = state_ref[hd]
    gc = gcum_ref[hd]
    g_last = gc[c - 1:c, :]

    o = _dot((q * jnp.exp(gc)).astype(BF16), state.astype(BF16))

    sub = _sublane((SUBLANES, 1))
    g_end = vreg(gc, TILE_VREGS - 1)
    tile_rows = lambda a: jnp.concatenate([a] * TILE_VREGS, axis=0)
    attn = None
    for b in _STREAM_LEVELS:
        w = b // TILE_VREGS
        g_mid = None
        for first in range(0, SUBLANES, 2 * w):
            row = jnp.broadcast_to(g_end[first + w - 1:first + w, :], (SUBLANES, dk))
            g_mid = row if g_mid is None else jnp.where(sub >= first, row, g_mid)
        upper = (lax.shift_right_logical(sub, w.bit_length() - 1) & 1) == 1
        g_q = tile_rows(jnp.where(upper, g_mid, _EXCLUDE))
        g_k = tile_rows(jnp.where(upper, -_EXCLUDE, g_mid))
        q_b = (q * jnp.exp(gc - g_q)).astype(BF16)
        k_b = (k * jnp.exp(g_k - gc)).astype(BF16)
        p = _dot_nt(q_b, k_b)
        if b in _MASKED_STREAM_LEVELS:
            p = p * sm_ref[_MASKED_STREAM_LEVELS.index(b)]
        attn = p if attn is None else attn + p
    o = o + _dot(attn.astype(BF16), v)
    o = o + jnp.sum(q * k, axis=-1, keepdims=True) * v32

    o_rows = [vreg(o, j) for j in range(TILE_VREGS)]
    for li, b in enumerate(_VREG_LEVELS):
        upper_rows, lower_rows = _vreg_level_rows(b)
        q_rows, k_rows = [], []
        for j in range(TILE_VREGS):
            jm = (j // (2 * b)) * (2 * b) + b - 1
            if j in upper_rows:
                q_rows.append(vreg(q, j) * jnp.exp(vreg(gc, j) - vreg(gc, jm)))
            elif j == jm:
                k_rows.append(vreg(k, j))
            else:
                k_rows.append(vreg(k, j) * jnp.exp(vreg(gc, jm) - vreg(gc, j)))
        p = _dot_nt(jnp.concatenate(q_rows, axis=0).astype(BF16), jnp.concatenate(k_rows, axis=0).astype(BF16))
        p = (p * vm_ref[li]).astype(BF16)
        if b >= 2:
            v_low = jnp.concatenate(
                [v[j0 * SUBLANES:(j0 + b) * SUBLANES] for j0 in range(0, TILE_VREGS, 2 * b)], axis=0)
        else:
            v_low = jnp.concatenate([vreg(v32, j) for j in lower_rows], axis=0).astype(BF16)
        o_up = _dot(p, v_low)
        for idx, j in enumerate(upper_rows):
            o_rows[j] = o_rows[j] + vreg(o_up, idx)
    o = jnp.concatenate(o_rows, axis=0)

    k_rev = (k * jnp.exp(g_last - gc)).astype(BF16)
    upd = lax.dot_general(k_rev, v, (((0,), (0,)), ((), ())), preferred_element_type=F32)
    decay_rows = jnp.broadcast_to(jnp.exp(g_last), (dk, dk))
    decay_col = jnp.transpose(decay_rows)[:, 0:1]
    state_ref[hd] = decay_col * state + upd

    r = r_ref[...]
    o = _rms(o, hn_ref[...]) * (r * jax.nn.sigmoid(r))
    w_rows = pl.ds(pl.multiple_of(hd * dv, dv), dv)
    acc_ref[...] += _dot(o.astype(BF16), wout_ref[w_rows, :])

    @pl.when(hd == GLA_HEADS - 1)
    def _():
        o_ref[...] = x_ref[...] + mod_ref[2:3, :] * _rms(acc_ref[...], ng_ref[1:2, :])


def _gla_core(q, k, g, v, r, x, mod, ng, head_norm, w_out):
    bsz, s, d = x.shape
    c = ROW_TILE
    dk = q.shape[3]
    dv = v.shape[3]
    cumsum_mat = jnp.asarray(_gla_cumsum_matrix(c), BF16)
    stream_masks = jnp.asarray(_gla_stream_masks(c), F32)
    vreg_masks = jnp.asarray(_gla_vreg_masks(), F32)
    head_spec = lambda w: pl.BlockSpec((None, None, c, w), lambda b, i, h: (b, h, i, 0))
    tile = pl.BlockSpec((None, c, d), lambda b, i, h: (b, i, 0))
    return pl.pallas_call(
        _gla_core_kernel,
        grid=(bsz, s // c, GLA_HEADS),
        in_specs=[
            head_spec(dk),
            head_spec(dk),
            pl.BlockSpec((None, GLA_HEADS, c, dk), lambda b, i, h: (b, 0, i, 0)),
            head_spec(dv),
            head_spec(dv),
            tile,
            pl.BlockSpec((None, 6, d), lambda b, i, h: (b, 0, 0)),
            _const_spec(ng.shape),
            _const_spec(head_norm.shape),
            _const_spec(cumsum_mat.shape),
            _const_spec(stream_masks.shape),
            _const_spec(vreg_masks.shape),
            _const_spec(w_out.shape),
        ],
        out_specs=tile,
        out_shape=jax.ShapeDtypeStruct(x.shape, F32),
        scratch_shapes=[
            pltpu.VMEM((GLA_HEADS, c, dk), F32),
            pltpu.VMEM((GLA_HEADS, dk, dv), F32),
            pltpu.VMEM((c, d), F32),
        ],
        compiler_params=_params("arbitrary", "arbitrary", "arbitrary"),
        name="gla_core",
    )(q, k, g, v, r, x, mod, ng, head_norm, cumsum_mat, stream_masks, vreg_masks, w_out)


def kernel(x, c, ada_w, ada_b, norm_g, ffn_w_up, ffn_conv_w, ffn_conv_b, ffn_w_down, rg_w_in, rg_conv_w,
           rg_conv_b, rg_wa, rg_ba, rg_wx, rg_bx, rg_lambda, rg_w_out, gla_w_in, gla_w_alpha, gla_b_alpha,
           gla_norm_g, gla_w_out):
    depth = ada_w.shape[0]
    d = x.shape[2]
    assert depth >= 1 and x.shape[1] % ROW_TILE == 0
    mod = _ada_modulation(c, ada_w, ada_b)
    for i in range(depth):
        j = i // 2
        if i % 2 == 0:
            vec = jnp.stack([rg_conv_b[j], rg_ba[j], rg_bx[j], rg_lambda[j]])
            w_gate = jnp.concatenate([rg_wa[j], rg_wx[j]], axis=-1).astype(BF16)
            x = _rglru_layer(x, mod[i], norm_g[i], rg_w_in[j].astype(BF16), rg_conv_w[j], vec, w_gate,
                             rg_w_out[j].astype(BF16), natural_in=(i == 0))
        else:
            qk = gla_w_alpha.shape[2]
            n_main = 2 * qk + 2 * d
            rank = gla_w_in.shape[2] - n_main
            w_main = gla_w_in[j][:, :n_main].astype(BF16)
            w_z = jnp.zeros((d, GLA_RANK_PAD), BF16).at[:, :rank].set(gla_w_in[j][:, n_main:].astype(BF16))
            w_alpha = jnp.zeros((GLA_RANK_PAD, qk), BF16).at[:rank].set(gla_w_alpha[j].astype(BF16))
            q, k, g, v, r = _gla_proj(x, mod[i], norm_g[i], w_main, w_z, w_alpha, gla_b_alpha[j][None, :])
            x = _gla_core(q, k, g, v, r, x, mod[i], norm_g[i], gla_norm_g[j][None, :],
                          gla_w_out[j].astype(BF16))
        x = _ffn_layer(x, mod[i], norm_g[i], ffn_w_up[i].astype(BF16), ffn_conv_w[i],
                       ffn_conv_b[i][None, :], ffn_w_down[i].astype(BF16), natural_out=(i == depth - 1))
    return x
```

```python
import functools
import math

import jax
import jax.numpy as jnp
import numpy as np
from jax import lax
from jax.experimental import pallas as pl
from jax.experimental.pallas import tpu as pltpu

F32 = jnp.float32
BF16 = jnp.bfloat16

EPS = 1e-6
RG_BLOCKS = 4
RG_C = 8.0
RG_CONV = 4
FFN_CONV = 3
GLA_HEADS = 4
GLA_TAU = 16.0
GLA_RANK_PAD = 128
SUBLANES = 8
LANES = 128

ROW_TILE = 256
TILE_VREGS = ROW_TILE // SUBLANES
FFN_COL_CHUNK = 256
VMEM_LIMIT = 56 * 1024 * 1024

_GELU_C0 = 2.0 * math.sqrt(2.0 / math.pi)
_GELU_C1 = _GELU_C0 * 0.044715


def _dot(a, b):
    return jnp.dot(a, b, preferred_element_type=F32)


def _rms(x, g):
    return x * lax.rsqrt(jnp.mean(x * x, axis=-1, keepdims=True) + EPS) * g


def _gelu_times(x, v):
    neg_2z = x * (-_GELU_C0 - _GELU_C1 * (x * x))
    return (x * v) / (1.0 + jnp.exp(neg_2z))


def _sublane(shape):
    return lax.broadcasted_iota(jnp.int32, shape, 0) % SUBLANES


def _natural_tile_spec(d):
    return pl.BlockSpec((None, None, SUBLANES, TILE_VREGS, d), lambda b, i: (b, i, 0, 0, 0))


def _load_tile_order(ref):
    return jnp.concatenate([ref[:, j, :] for j in range(TILE_VREGS)], axis=0)


def _store_natural_order(ref, val):
    for j in range(TILE_VREGS):
        ref[:, j, :] = val[j * SUBLANES:(j + 1) * SUBLANES, :]


def _delay(x, tail, k):
    ts = x.shape[0]
    n_tail = tail.shape[0] // SUBLANES
    last = _sublane((SUBLANES, 1)) == SUBLANES - 1
    head = []
    for j in range(k):
        src = TILE_VREGS - k + j
        cur = x[src * SUBLANES:(src + 1) * SUBLANES]
        prv = tail[(n_tail - k + j) * SUBLANES:(n_tail - k + j + 1) * SUBLANES]
        head.append(pltpu.roll(jnp.where(last, prv, cur), 1, 0))
    return jnp.concatenate(head + [x[:ts - k * SUBLANES]], axis=0)


def _params(*sem):
    return pltpu.CompilerParams(dimension_semantics=sem, vmem_limit_bytes=VMEM_LIMIT)


def _const_spec(shape):
    n = len(shape)
    return pl.BlockSpec(shape, lambda *_: (0,) * n, pipeline_mode=pl.Buffered(1))


def _ada_kernel(c_ref, w_ref, b_ref, o_ref):
    c = c_ref[...]
    c_act = (c * jax.nn.sigmoid(c)).astype(BF16)
    o_ref[...] = _dot(c_act, w_ref[...].astype(BF16)) + b_ref[...]


def _ada_modulation(c, ada_w, ada_b):
    depth, d, n = ada_w.shape
    bsz = c.shape[0]
    c_pad = jnp.zeros((SUBLANES, d), F32).at[:bsz].set(c)
    tn = 1024
    out = pl.pallas_call(
        _ada_kernel,
        grid=(depth, n // tn),
        in_specs=[
            pl.BlockSpec((SUBLANES, d), lambda l, j: (0, 0)),
            pl.BlockSpec((None, d, tn), lambda l, j: (l, 0, j)),
            pl.BlockSpec((None, 1, tn), lambda l, j: (l, 0, j)),
        ],
        out_specs=pl.BlockSpec((None, SUBLANES, tn), lambda l, j: (l, 0, j)),
        out_shape=jax.ShapeDtypeStruct((depth, SUBLANES, n), F32),
        compiler_params=_params("arbitrary", "arbitrary"),
        name="ada_mod",
    )(c_pad, ada_w, ada_b.reshape(depth, 1, n))
    return out[:, :bsz].reshape(depth, bsz, 6, d)


def _stream_scan(a, u, carry):
    vreg = lambda x, j: x[j * SUBLANES:(j + 1) * SUBLANES]
    hl = [vreg(u, 0)]
    al = [vreg(a, 0)]
    for j in range(1, TILE_VREGS):
        hl.append(vreg(a, j) * hl[-1] + vreg(u, j))
        al.append(vreg(a, j) * al[-1])
    sub = _sublane((SUBLANES, 1))
    ea, eh = al[-1], hl[-1]
    for k in (1, 2, 4):
        keep = sub >= k
        ea_sh = jnp.where(keep, pltpu.roll(ea, k, 0), 1.0)
        eh_sh = jnp.where(keep, pltpu.roll(eh, k, 0), 0.0)
        eh = ea * eh_sh + eh
        ea = ea * ea_sh
    after = ea * carry + eh
    before = jnp.where(sub == 0, carry, pltpu.roll(after, 1, 0))
    h = jnp.concatenate([hl[j] + al[j] * before for j in range(TILE_VREGS)], axis=0)
    return h, after


def _rglru_kernel(natural_in, x_ref, mod_ref, ng_ref, win_ref, cw_ref, vec_ref, wg_ref, wout_ref, o_ref,
                  tail_ref, hc_ref):
    ts, d = o_ref.shape
    blk = d // RG_BLOCKS
    n_tail = RG_CONV - 1

    @pl.when(pl.program_id(1) == 0)
    def _():
        tail_ref[...] = jnp.zeros_like(tail_ref)
        hc_ref[...] = jnp.zeros_like(hc_ref)

    x = _load_tile_order(x_ref) if natural_in else x_ref[...]
    h = (_rms(x, ng_ref[0:1, :]) * (1.0 + mod_ref[1:2, :]) + mod_ref[0:1, :]).astype(BF16)

    def project(g):
        return (_dot(h, win_ref[:, g * blk:(g + 1) * blk]),
                _dot(h, win_ref[:, d + g * blk:d + (g + 1) * blk]))

    def mix(g, gate_br, xb):
        cols = slice(g * blk, (g + 1) * blk)
        tail = tail_ref[:, cols]
        tail_ref[:, cols] = xb[ts - n_tail * SUBLANES:, :]
        xc = vec_ref[0:1, cols] + _delay(xb, tail, 3) * cw_ref[0:1, cols]
        xc = xc + _delay(xb, tail, 2) * cw_ref[1:2, cols]
        xc = xc + _delay(xb, tail, 1) * cw_ref[2:3, cols]
        xc = xc + xb * cw_ref[3:4, cols]
        p = _dot(xc.astype(BF16), wg_ref[g])
        r = jax.nn.sigmoid(p[:, :blk] + vec_ref[1:2, cols])
        i_g = jax.nn.sigmoid(p[:, blk:] + vec_ref[2:3, cols])
        nl = -vec_ref[3:4, cols]
        softplus_nl = jnp.maximum(nl, 0.0) + jnp.log1p(jnp.exp(-jnp.abs(nl)))
        log_a = (-RG_C) * r * softplus_nl
        a = jnp.exp(log_a)
        t = jnp.tanh(log_a)
        u = jnp.sqrt(-2.0 * t / (1.0 - t)) * (i_g * xc)
        hs, after = _stream_scan(a, u, hc_ref[SUBLANES - 1:SUBLANES, cols])
        hc_ref[:, cols] = after
        return _gelu_times(gate_br, hs).astype(BF16)

    acc = jnp.zeros((ts, d), F32)
    cur = project(0)
    y = None
    for g in range(RG_BLOCKS):
        nxt = project(g + 1) if g + 1 < RG_BLOCKS else None
        if y is not None:
            acc = acc + _dot(y, wout_ref[(g - 1) * blk:g * blk, :])
        y = mix(g, *cur)
        cur = nxt
    acc = acc + _dot(y, wout_ref[(RG_BLOCKS - 1) * blk:, :])
    o_ref[...] = x + mod_ref[2:3, :] * _rms(acc, ng_ref[1:2, :])


def _rglru_layer(x, mod, ng, w_in, conv_w, vec, w_gate, w_out, natural_in):
    bsz, s, d = x.shape
    ts = ROW_TILE
    tile = pl.BlockSpec((None, ts, d), lambda b, i: (b, i, 0))
    x_tile = tile
    if natural_in:
        x = x.reshape(bsz, s // ts, SUBLANES, TILE_VREGS, d)
        x_tile = _natural_tile_spec(d)
    return pl.pallas_call(
        functools.partial(_rglru_kernel, natural_in),
        grid=(bsz, s // ts),
        in_specs=[
            x_tile,
            pl.BlockSpec((None, 6, d), lambda b, i: (b, 0, 0)),
            _const_spec(ng.shape),
            _const_spec(w_in.shape),
            _const_spec(conv_w.shape),
            _const_spec(vec.shape),
            _const_spec(w_gate.shape),
            _const_spec(w_out.shape),
        ],
        out_specs=tile,
        out_shape=jax.ShapeDtypeStruct((bsz, s, d), F32),
        scratch_shapes=[
            pltpu.VMEM(((RG_CONV - 1) * SUBLANES, d), F32),
            pltpu.VMEM((SUBLANES, d), F32),
        ],
        compiler_params=_params("arbitrary", "arbitrary"),
        name="rglru_layer",
    )(x, mod, ng, w_in, conv_w, vec, w_gate, w_out)


def _ffn_kernel(natural_out, x_ref, mod_ref, ng_ref, wup_ref, cw_ref, cb_ref, wdn_ref, o_ref, tail_ref):
    ts, d = x_ref.shape
    f = wdn_ref.shape[0]
    fc = FFN_COL_CHUNK
    n_tail = FFN_CONV - 1

    @pl.when(pl.program_id(1) == 0)
    def _():
        tail_ref[...] = jnp.zeros_like(tail_ref)

    x = x_ref[...]
    h = (_rms(x, ng_ref[2:3, :]) * (1.0 + mod_ref[4:5, :]) + mod_ref[3:4, :]).astype(BF16)

    def up_cols(c0):
        return _dot(h, wup_ref[:, c0:c0 + fc])

    def conv_cols(up, c0):
        cols = slice(c0, c0 + fc)
        tail = tail_ref[:, cols]
        tail_ref[:, cols] = up[ts - n_tail * SUBLANES:, :]
        y = cb_ref[:, cols] + _delay(up, tail, 2) * cw_ref[0:1, cols]
        y = y + _delay(up, tail, 1) * cw_ref[1:2, cols]
        return y + up * cw_ref[2:3, cols]

    n_chunks = f // fc
    acc = jnp.zeros((ts, d), F32)
    ups = (up_cols(0), up_cols(f))
    act = None
    for c in range(n_chunks):
        nxt = (up_cols((c + 1) * fc), up_cols(f + (c + 1) * fc)) if c + 1 < n_chunks else None
        if act is not None:
            acc = acc + _dot(act, wdn_ref[(c - 1) * fc:c * fc, :])
        g = conv_cols(ups[0], c * fc)
        val = conv_cols(ups[1], f + c * fc)
        act = _gelu_times(g, val).astype(BF16)
        ups = nxt
    acc = acc + _dot(act, wdn_ref[(n_chunks - 1) * fc:n_chunks * fc, :])
    out = x + mod_ref[5:6, :] * _rms(acc, ng_ref[3:4, :])
    if natural_out:
        _store_natural_order(o_ref, out)
    else:
        o_ref[...] = out


def _ffn_layer(x, mod, ng, w_up, conv_w, conv_b, w_down, natural_out):
    bsz, s, d = x.shape
    ts = ROW_TILE
    tile = pl.BlockSpec((None, ts, d), lambda b, i: (b, i, 0))
    out_tile, out_shape = tile, (bsz, s, d)
    if natural_out:
        out_tile = _natural_tile_spec(d)
        out_shape = (bsz, s // ts, SUBLANES, TILE_VREGS, d)
    out = pl.pallas_call(
        functools.partial(_ffn_kernel, natural_out),
        grid=(bsz, s // ts),
        in_specs=[
            tile,
            pl.BlockSpec((None, 6, d), lambda b, i: (b, 0, 0)),
            _const_spec(ng.shape),
            _const_spec(w_up.shape),
            _const_spec(conv_w.shape),
            _const_spec(conv_b.shape),
            _const_spec(w_down.shape),
        ],
        out_specs=out_tile,
        out_shape=jax.ShapeDtypeStruct(out_shape, F32),
        scratch_shapes=[
            pltpu.VMEM(((FFN_CONV - 1) * SUBLANES, w_up.shape[1]), F32),
        ],
        compiler_params=_params("arbitrary", "arbitrary"),
        name="ffn_layer",
    )(x, mod, ng, w_up, conv_w, conv_b, w_down)
    return out.reshape(bsz, s, d)


_STREAM_LEVELS = tuple(TILE_VREGS << i for i in (2, 1, 0))
_VREG_LEVELS = tuple(TILE_VREGS >> i for i in range(1, 6))
_MASKED_STREAM_LEVELS = _STREAM_LEVELS[1:]
_EXCLUDE = 1e30


def _tile_order_times(c):
    p = np.arange(c)
    return (p % SUBLANES) * (c // SUBLANES) + p // SUBLANES


def _gla_cumsum_matrix(c):
    t = _tile_order_times(c)
    return (t[None, :] <= t[:, None]).astype(np.float32)


def _gla_stream_masks(c):
    t = _tile_order_times(c)
    return np.stack([(t[:, None] // (2 * b) == t[None, :] // (2 * b)) for b in _MASKED_STREAM_LEVELS]
                    ).astype(np.float32)


def _vreg_level_rows(b):
    upper = [j for j in range(TILE_VREGS) if (j // b) % 2 == 1]
    lower = [j for j in range(TILE_VREGS) if (j // b) % 2 == 0]
    return upper, lower


def _gla_vreg_masks():
    out = []
    s = np.arange(SUBLANES)
    for b in _VREG_LEVELS:
        upper, lower = _vreg_level_rows(b)
        ju = np.repeat(np.array(upper), SUBLANES)[:, None]
        jl = np.repeat(np.array(lower), SUBLANES)[None, :]
        su = np.tile(s, len(upper))[:, None]
        sl = np.tile(s, len(lower))[None, :]
        out.append((su == sl) & (ju // (2 * b) == jl // (2 * b)))
    return np.stack(out).astype(np.float32)


def _dot_nt(a, b):
    return lax.dot_general(a, b, (((1,), (1,)), ((), ())), preferred_element_type=F32)


def _vreg(a, j):
    return a[j * SUBLANES:(j + 1) * SUBLANES]


def _gla_scores(proj, gc, state):
    c, dk = gc.shape
    dv = state.shape[1]
    vreg = _vreg
    q = proj[:, :dk] * (dk ** -0.5)
    k = proj[:, dk:2 * dk]
    v32 = proj[:, 2 * dk:2 * dk + dv]
    v = v32.astype(BF16)
    g_last = gc[c - 1:c, :]

    o_inter = _dot((q * jnp.exp(gc)).astype(BF16), state.astype(BF16))

    sub = _sublane((SUBLANES, 1))
    g_end = vreg(gc, TILE_VREGS - 1)
    tile_rows = lambda a: jnp.concatenate([a] * TILE_VREGS, axis=0)
    p_stream = []
    for b in _STREAM_LEVELS:
        w = b // TILE_VREGS
        g_mid = None
        for first in range(0, SUBLANES, 2 * w):
            row = jnp.broadcast_to(g_end[first + w - 1:first + w, :], (SUBLANES, dk))
            g_mid = row if g_mid is None else jnp.where(sub >= first, row, g_mid)
        upper = (lax.shift_right_logical(sub, w.bit_length() - 1) & 1) == 1
        g_q = tile_rows(jnp.where(upper, g_mid, _EXCLUDE))
        g_k = tile_rows(jnp.where(upper, -_EXCLUDE, g_mid))
        q_b = (q * jnp.exp(gc - g_q)).astype(BF16)
        k_b = (k * jnp.exp(g_k - gc)).astype(BF16)
        p_stream.append(_dot_nt(q_b, k_b))

    p_vreg = []
    for b in _VREG_LEVELS:
        upper_rows, _ = _vreg_level_rows(b)
        q_rows, k_rows = [], []
        for j in range(TILE_VREGS):
            jm = (j // (2 * b)) * (2 * b) + b - 1
            if j in upper_rows:
                q_rows.append(vreg(q, j) * jnp.exp(vreg(gc, j) - vreg(gc, jm)))
            elif j == jm:
                k_rows.append(vreg(k, j))
            else:
                k_rows.append(vreg(k, j) * jnp.exp(vreg(gc, jm) - vreg(gc, j)))
        p_vreg.append(_dot_nt(jnp.concatenate(q_rows, axis=0).astype(BF16),
                              jnp.concatenate(k_rows, axis=0).astype(BF16)))

    k_rev = (k * jnp.exp(g_last - gc)).astype(BF16)
    upd = lax.dot_general(k_rev, v, (((0,), (0,)), ((), ())), preferred_element_type=F32)
    decay_rows = jnp.broadcast_to(jnp.exp(g_last), (dk, dk))
    decay_col = jnp.transpose(decay_rows)[:, 0:1]
    new_state = decay_col * state + upd

    diag = jnp.sum(q * k, axis=-1, keepdims=True)
    return (o_inter, p_stream, p_vreg, diag), new_state


def _gla_combine(scores, proj, hn, sm_ref, vm_ref):
    o, p_stream, p_vreg, diag = scores
    dv = o.shape[1]
    dk = (proj.shape[1] - 2 * dv) // 2
    v32 = proj[:, 2 * dk:2 * dk + dv]
    r = proj[:, 2 * dk + dv:]
    v = v32.astype(BF16)

    attn = None
    for b, p in zip(_STREAM_LEVELS, p_stream):
        if b in _MASKED_STREAM_LEVELS:
            p = p * sm_ref[_MASKED_STREAM_LEVELS.index(b)]
        attn = p if attn is None else attn + p
    o = o + _dot(attn.astype(BF16), v) + diag * v32

    o_rows = [_vreg(o, j) for j in range(TILE_VREGS)]
    for li, (b, p) in enumerate(zip(_VREG_LEVELS, p_vreg)):
        upper_rows, lower_rows = _vreg_level_rows(b)
        p = (p * vm_ref[li]).astype(BF16)
        if b >= 2:
            v_low = jnp.concatenate(
                [v[j0 * SUBLANES:(j0 + b) * SUBLANES] for j0 in range(0, TILE_VREGS, 2 * b)], axis=0)
        else:
            v_low = jnp.concatenate([_vreg(v32, j) for j in lower_rows], axis=0).astype(BF16)
        o_up = _dot(p, v_low)
        for idx, j in enumerate(upper_rows):
            o_rows[j] = o_rows[j] + _vreg(o_up, idx)
    o = jnp.concatenate(o_rows, axis=0)
    return (_rms(o, hn) * (r * jax.nn.sigmoid(r))).astype(BF16)


def _gla_kernel(x_ref, mod_ref, ng_ref, wh_ref, wz_ref, wal_ref, bal_ref, hn_ref, cs_ref, sm_ref, vm_ref,
                wout_ref, o_ref, state_ref):
    c, d = x_ref.shape
    dk, dv = state_ref.shape[1:]

    @pl.when(pl.program_id(1) == 0)
    def _():
        state_ref[...] = jnp.zeros_like(state_ref)

    x = x_ref[...]
    h = (_rms(x, ng_ref[0:1, :]) * (1.0 + mod_ref[1:2, :]) + mod_ref[0:1, :]).astype(BF16)

    z = _dot(h, wz_ref[...])
    ga = _dot(z.astype(BF16), wal_ref[...]) + bal_ref[...]
    g = (jnp.minimum(ga, 0.0) - jnp.log1p(jnp.exp(-jnp.abs(ga)))) / GLA_TAU
    g_hi = g.astype(BF16)
    g_lo = (g - g_hi.astype(F32)).astype(BF16)
    cs = cs_ref[...]
    gc_all = _dot(cs, g_hi) + _dot(cs, g_lo)

    def project(hh):
        return _dot(h, wh_ref[hh])

    def scores(hh, proj):
        out, state_ref[hh] = _gla_scores(proj, gc_all[:, hh * dk:(hh + 1) * dk], state_ref[hh])
        return out

    projs = {0: project(0)}
    acc = jnp.zeros((c, d), F32)
    y = None
    for hh in range(GLA_HEADS):
        if hh + 1 < GLA_HEADS:
            projs[hh + 1] = project(hh + 1)
        if y is not None:
            acc = acc + _dot(y, wout_ref[(hh - 1) * dv:hh * dv, :])
        y = _gla_combine(scores(hh, projs[hh]), projs.pop(hh), hn_ref[...], sm_ref, vm_ref)
    acc = acc + _dot(y, wout_ref[(GLA_HEADS - 1) * dv:, :])
    o_ref[...] = x + mod_ref[2:3, :] * _rms(acc, ng_ref[1:2, :])


def _gla_layer(x, mod, ng, w_heads, w_z, w_alpha, b_alpha, head_norm, w_out):
    bsz, s, d = x.shape
    c = ROW_TILE
    dk = w_alpha.shape[1] // GLA_HEADS
    dv = d // GLA_HEADS
    cumsum_mat = jnp.asarray(_gla_cumsum_matrix(c), BF16)
    stream_masks = jnp.asarray(_gla_stream_masks(c), F32)
    vreg_masks = jnp.asarray(_gla_vreg_masks(), F32)
    tile = pl.BlockSpec((None, c, d), lambda b, i: (b, i, 0))
    return pl.pallas_call(
        _gla_kernel,
        grid=(bsz, s // c),
        in_specs=[
            tile,
            pl.BlockSpec((None, 6, d), lambda b, i: (b, 0, 0)),
            _const_spec(ng.shape),
            _const_spec(w_heads.shape),
            _const_spec(w_z.shape),
            _const_spec(w_alpha.shape),
            _const_spec(b_alpha.shape),
            _const_spec(head_norm.shape),
            _const_spec(cumsum_mat.shape),
            _const_spec(stream_masks.shape),
            _const_spec(vreg_masks.shape),
            _const_spec(w_out.shape),
        ],
        out_specs=tile,
        out_shape=jax.ShapeDtypeStruct(x.shape, F32),
        scratch_shapes=[pltpu.VMEM((GLA_HEADS, dk, dv), F32)],
        compiler_params=_params("arbitrary", "arbitrary"),
        name="gla_layer",
    )(x, mod, ng, w_heads, w_z, w_alpha, b_alpha, head_norm, cumsum_mat, stream_masks, vreg_masks, w_out)


def kernel(x, c, ada_w, ada_b, norm_g, ffn_w_up, ffn_conv_w, ffn_conv_b, ffn_w_down, rg_w_in, rg_conv_w,
           rg_conv_b, rg_wa, rg_ba, rg_wx, rg_bx, rg_lambda, rg_w_out, gla_w_in, gla_w_alpha, gla_b_alpha,
           gla_norm_g, gla_w_out):
    depth = ada_w.shape[0]
    d = x.shape[2]
    assert depth >= 1 and x.shape[1] % ROW_TILE == 0
    mod = _ada_modulation(c, ada_w, ada_b)
    for i in range(depth):
        j = i // 2
        if i % 2 == 0:
            vec = jnp.stack([rg_conv_b[j], rg_ba[j], rg_bx[j], rg_lambda[j]])
            w_gate = jnp.concatenate([rg_wa[j], rg_wx[j]], axis=-1).astype(BF16)
            x = _rglru_layer(x, mod[i], norm_g[i], rg_w_in[j].astype(BF16), rg_conv_w[j], vec, w_gate,
                             rg_w_out[j].astype(BF16), natural_in=(i == 0))
        else:
            qk = gla_w_alpha.shape[2]
            n_main = 2 * qk + 2 * d
            rank = gla_w_in.shape[2] - n_main
            w_in = gla_w_in[j].astype(BF16)
            dk, dv = qk // GLA_HEADS, d // GLA_HEADS
            w_heads = jnp.stack([
                jnp.concatenate([w_in[:, hh * dk:(hh + 1) * dk],
                                 w_in[:, qk + hh * dk:qk + (hh + 1) * dk],
                                 w_in[:, 2 * qk + hh * dv:2 * qk + (hh + 1) * dv],
                                 w_in[:, 2 * qk + d + hh * dv:2 * qk + d + (hh + 1) * dv]], axis=1)
                for hh in range(GLA_HEADS)])
            w_z = jnp.zeros((d, GLA_RANK_PAD), BF16).at[:, :rank].set(w_in[:, n_main:])
            w_alpha = jnp.zeros((GLA_RANK_PAD, qk), BF16).at[:rank].set(gla_w_alpha[j].astype(BF16))
            x = _gla_layer(x, mod[i], norm_g[i], w_heads, w_z, w_alpha, gla_b_alpha[j][None, :],
                           gla_norm_g[j][None, :], gla_w_out[j].astype(BF16))
        x = _ffn_layer(x, mod[i], norm_g[i], ffn_w_up[i].astype(BF16), ffn_conv_w[i],
                       ffn_conv_b[i][None, :], ffn_w_down[i].astype(BF16), natural_out=(i == depth - 1))
    return x
```

```python
import functools
import math

import jax
import jax.numpy as jnp
import numpy as np
from jax import lax
from jax.experimental import pallas as pl
from jax.experimental.pallas import tpu as pltpu

F32 = jnp.float32
BF16 = jnp.bfloat16

EPS = 1e-6
RG_BLOCKS = 4
RG_C = 8.0
RG_CONV = 4
FFN_CONV = 3
GLA_HEADS = 4
GLA_TAU = 16.0
GLA_RANK_PAD = 128
SUBLANES = 8
LANES = 128

ROW_TILE = 256
TILE_VREGS = ROW_TILE // SUBLANES
FFN_COL_CHUNK = 256
FFN_TILES_PER_STEP = 2
RGLRU_TILES_PER_STEP = 2
GLA_TILES_PER_STEP = 2
VMEM_LIMIT = 56 * 1024 * 1024

_GELU_C0 = 2.0 * math.sqrt(2.0 / math.pi)
_GELU_C1 = _GELU_C0 * 0.044715


def _dot(a, b):
    return jnp.dot(a, b, preferred_element_type=F32)


def _rms(x, g):
    return x * lax.rsqrt(jnp.mean(x * x, axis=-1, keepdims=True) + EPS) * g


def _gelu_times(x, v):
    neg_2z = x * (-_GELU_C0 - _GELU_C1 * (x * x))
    return (x * v) / (1.0 + jnp.exp(neg_2z))


def _sublane(shape):
    return lax.broadcasted_iota(jnp.int32, shape, 0) % SUBLANES


def _natural_tile_spec(d):
    return pl.BlockSpec((None, None, SUBLANES, TILE_VREGS, d), lambda b, i: (b, i, 0, 0, 0))


def _load_tile_order(ref):
    return jnp.concatenate([ref[:, j, :] for j in range(TILE_VREGS)], axis=0)


def _store_natural_order(ref, val):
    for j in range(TILE_VREGS):
        ref[:, j, :] = val[j * SUBLANES:(j + 1) * SUBLANES, :]


def _delay(x, tail, k):
    ts = x.shape[0]
    n_tail = tail.shape[0] // SUBLANES
    last = _sublane((SUBLANES, 1)) == SUBLANES - 1
    head = []
    for j in range(k):
        src = TILE_VREGS - k + j
        cur = x[src * SUBLANES:(src + 1) * SUBLANES]
        prv = tail[(n_tail - k + j) * SUBLANES:(n_tail - k + j + 1) * SUBLANES]
        head.append(pltpu.roll(jnp.where(last, prv, cur), 1, 0))
    return jnp.concatenate(head + [x[:ts - k * SUBLANES]], axis=0)


def _params(*sem):
    return pltpu.CompilerParams(dimension_semantics=sem, vmem_limit_bytes=VMEM_LIMIT)


def _const_spec(shape):
    n = len(shape)
    return pl.BlockSpec(shape, lambda *_: (0,) * n, pipeline_mode=pl.Buffered(1))


def _layer_spec(shape, layer):
    n = len(shape) - 1
    return pl.BlockSpec((None,) + tuple(shape[1:]), lambda *_: (layer,) + (0,) * n,
                        pipeline_mode=pl.Buffered(1))


def _ada_kernel(c_ref, w_ref, b_ref, o_ref):
    c = c_ref[...]
    c_act = (c * jax.nn.sigmoid(c)).astype(BF16)
    o_ref[...] = _dot(c_act, w_ref[...].astype(BF16)) + b_ref[...]


def _ada_modulation(c, ada_w, ada_b):
    depth, d, n = ada_w.shape
    bsz = c.shape[0]
    c_pad = jnp.zeros((SUBLANES, d), F32).at[:bsz].set(c)
    tn = 1024
    out = pl.pallas_call(
        _ada_kernel,
        grid=(depth, n // tn),
        in_specs=[
            pl.BlockSpec((SUBLANES, d), lambda l, j: (0, 0)),
            pl.BlockSpec((None, d, tn), lambda l, j: (l, 0, j)),
            pl.BlockSpec((None, 1, tn), lambda l, j: (l, 0, j)),
        ],
        out_specs=pl.BlockSpec((None, SUBLANES, tn), lambda l, j: (l, 0, j)),
        out_shape=jax.ShapeDtypeStruct((depth, SUBLANES, n), F32),
        compiler_params=_params("arbitrary", "arbitrary"),
        name="ada_mod",
    )(c_pad, ada_w, ada_b.reshape(depth, 1, n))
    return out[:, :bsz].reshape(depth, bsz, 6, d)


def _stream_scan(a, u, carry):
    vreg = lambda x, j: x[j * SUBLANES:(j + 1) * SUBLANES]
    hl = [vreg(u, 0)]
    al = [vreg(a, 0)]
    for j in range(1, TILE_VREGS):
        hl.append(vreg(a, j) * hl[-1] + vreg(u, j))
        al.append(vreg(a, j) * al[-1])
    sub = _sublane((SUBLANES, 1))
    ea, eh = al[-1], hl[-1]
    for k in (1, 2, 4):
        keep = sub >= k
        ea_sh = jnp.where(keep, pltpu.roll(ea, k, 0), 1.0)
        eh_sh = jnp.where(keep, pltpu.roll(eh, k, 0), 0.0)
        eh = ea * eh_sh + eh
        ea = ea * ea_sh
    after = ea * carry + eh
    before = jnp.where(sub == 0, carry, pltpu.roll(after, 1, 0))
    h = jnp.concatenate([hl[j] + al[j] * before for j in range(TILE_VREGS)], axis=0)
    return h, after


def _rglru_kernel(natural_in, x_ref, mod_ref, ng_ref, win_ref, cw_ref, vec_ref, wg_ref, wout_ref, o_ref,
                  tail_ref, hc_ref):
    ts = ROW_TILE
    d = o_ref.shape[1]
    n_tiles = o_ref.shape[0] // ts
    blk = d // RG_BLOCKS
    n_tail = RG_CONV - 1

    @pl.when(pl.program_id(1) == 0)
    def _():
        tail_ref[...] = jnp.zeros_like(tail_ref)
        hc_ref[...] = jnp.zeros_like(hc_ref)

    def prologue(t):
        x = _load_tile_order(x_ref.at[t]) if natural_in else x_ref[t * ts:(t + 1) * ts, :]
        return x, (_rms(x, ng_ref[0:1, :]) * (1.0 + mod_ref[1:2, :]) + mod_ref[0:1, :]).astype(BF16)

    def epilogue(t, x, acc):
        o_ref[t * ts:(t + 1) * ts, :] = x + mod_ref[2:3, :] * _rms(acc, ng_ref[1:2, :])

    def project(t, g):
        return (_dot(hs[t], win_ref[:, g * blk:(g + 1) * blk]),
                _dot(hs[t], win_ref[:, d + g * blk:d + (g + 1) * blk]))

    def mix(g, gate_br, xb):
        cols = slice(g * blk, (g + 1) * blk)
        tail = tail_ref[:, cols]
        tail_ref[:, cols] = xb[ts - n_tail * SUBLANES:, :]
        xc = vec_ref[0:1, cols] + _delay(xb, tail, 3) * cw_ref[0:1, cols]
        xc = xc + _delay(xb, tail, 2) * cw_ref[1:2, cols]
        xc = xc + _delay(xb, tail, 1) * cw_ref[2:3, cols]
        xc = xc + xb * cw_ref[3:4, cols]
        p = _dot(xc.astype(BF16), wg_ref[g])
        r = jax.nn.sigmoid(p[:, :blk] + vec_ref[1:2, cols])
        i_g = jax.nn.sigmoid(p[:, blk:] + vec_ref[2:3, cols])
        nl = -vec_ref[3:4, cols]
        softplus_nl = jnp.maximum(nl, 0.0) + jnp.log1p(jnp.exp(-jnp.abs(nl)))
        log_a = r * ((-RG_C) * softplus_nl)
        a = jnp.exp(log_a)
        m2 = jnp.tanh(log_a) * (-1.0 - a * a)
        u = jnp.where(m2 > 0.0, m2 * lax.rsqrt(m2), 0.0) * (i_g * xc)
        hs, after = _stream_scan(a, u, hc_ref[SUBLANES - 1:SUBLANES, cols])
        hc_ref[:, cols] = after
        return _gelu_times(gate_br, hs).astype(BF16)

    items = [(t, g) for t in range(n_tiles) for g in range(RG_BLOCKS)]
    xs, hs, accs = {}, {}, {}
    xs[0], hs[0] = prologue(0)

    def out_project(t, g, y):
        part = _dot(y, wout_ref[g * blk:(g + 1) * blk, :])
        accs[t] = part if g == 0 else accs[t] + part
        if g == RG_BLOCKS - 1:
            epilogue(t, xs.pop(t), accs.pop(t))

    cur = project(0, 0)
    prev = None
    for idx, (t, g) in enumerate(items):
        if g == 1 and t + 1 < n_tiles:
            xs[t + 1], hs[t + 1] = prologue(t + 1)
        nxt = project(*items[idx + 1]) if idx + 1 < len(items) else None
        if prev is not None:
            out_project(*prev)
        prev = (t, g, mix(g, *cur))
        cur = nxt
    out_project(*prev)


def _rglru_layer(x, mod, ng, w_in, conv_w, vec, w_gate, w_out, natural_in):
    bsz, s, d = x.shape
    ts = ROW_TILE
    n_tiles = RGLRU_TILES_PER_STEP
    tile = pl.BlockSpec((None, n_tiles * ts, d), lambda b, i: (b, i, 0))
    x_tile = tile
    if natural_in:
        x = x.reshape(bsz, s // ts, SUBLANES, TILE_VREGS, d)
        x_tile = pl.BlockSpec((None, n_tiles, SUBLANES, TILE_VREGS, d), lambda b, i: (b, i, 0, 0, 0))
    return pl.pallas_call(
        functools.partial(_rglru_kernel, natural_in),
        grid=(bsz, s // (n_tiles * ts)),
        in_specs=[
            x_tile,
            pl.BlockSpec((None, 6, d), lambda b, i: (b, 0, 0)),
            _const_spec(ng.shape),
            _const_spec(w_in.shape),
            _const_spec(conv_w.shape),
            _const_spec(vec.shape),
            _const_spec(w_gate.shape),
            _const_spec(w_out.shape),
        ],
        out_specs=tile,
        out_shape=jax.ShapeDtypeStruct((bsz, s, d), F32),
        scratch_shapes=[
            pltpu.VMEM(((RG_CONV - 1) * SUBLANES, d), F32),
            pltpu.VMEM((SUBLANES, d), F32),
        ],
        compiler_params=_params("arbitrary", "arbitrary"),
        name="rglru_layer",
    )(x, mod, ng, w_in, conv_w, vec, w_gate, w_out)


def _ffn_kernel(natural_out, x_ref, mod_ref, ng_ref, wup_ref, cw_ref, cb_ref, wdn_ref, o_ref, tail_ref):
    ts = ROW_TILE
    d = x_ref.shape[1]
    n_tiles = x_ref.shape[0] // ts
    f = wdn_ref.shape[0]
    fc = FFN_COL_CHUNK
    n_chunks = f // fc
    n_tail = FFN_CONV - 1

    @pl.when(pl.program_id(1) == 0)
    def _():
        tail_ref[...] = jnp.zeros_like(tail_ref)

    def prologue(t):
        x = x_ref[t * ts:(t + 1) * ts, :]
        return x, (_rms(x, ng_ref[2:3, :]) * (1.0 + mod_ref[4:5, :]) + mod_ref[3:4, :]).astype(BF16)

    def epilogue(t, x, acc):
        out = x + mod_ref[5:6, :] * _rms(acc, ng_ref[3:4, :])
        if natural_out:
            _store_natural_order(o_ref.at[t], out)
        else:
            o_ref[t * ts:(t + 1) * ts, :] = out

    def conv_cols(up, c0):
        cols = slice(c0, c0 + fc)
        tail = tail_ref[:, cols]
        tail_ref[:, cols] = up[ts - n_tail * SUBLANES:, :]
        y = cb_ref[:, cols] + _delay(up, tail, 2) * cw_ref[0:1, cols]
        y = y + _delay(up, tail, 1) * cw_ref[1:2, cols]
        return y + up * cw_ref[2:3, cols]

    items = [(t, c) for t in range(n_tiles) for c in range(n_chunks)]
    xs, hs, accs = {}, {}, {}
    xs[0], hs[0] = prologue(0)

    def up_pair(t, c):
        return (_dot(hs[t], wup_ref[:, c * fc:(c + 1) * fc]), _dot(hs[t], wup_ref[:, f + c * fc:f + (c + 1) * fc]))

    def down(t, c, act):
        part = _dot(act, wdn_ref[c * fc:(c + 1) * fc, :])
        accs[t] = part if c == 0 else accs[t] + part
        if c == n_chunks - 1:
            epilogue(t, xs.pop(t), accs.pop(t))

    ups = up_pair(0, 0)
    prev = None
    for idx, (t, c) in enumerate(items):
        if c == n_chunks - 4 and t + 1 < n_tiles:
            xs[t + 1], hs[t + 1] = prologue(t + 1)
        nxt = up_pair(*items[idx + 1]) if idx + 1 < len(items) else None
        if prev is not None:
            down(*prev)
        g = conv_cols(ups[0], c * fc)
        val = conv_cols(ups[1], f + c * fc)
        prev = (t, c, _gelu_times(g, val).astype(BF16))
        ups = nxt
    down(*prev)


def _ffn_layer(x, mod, ng, w_up, conv_w, conv_b, w_down, layer, natural_out):
    bsz, s, d = x.shape
    ts = ROW_TILE
    n_tiles = FFN_TILES_PER_STEP
    rows = pl.BlockSpec((None, n_tiles * ts, d), lambda b, i: (b, i, 0))
    out_tile, out_shape = rows, (bsz, s, d)
    if natural_out:
        out_tile = pl.BlockSpec((None, n_tiles, SUBLANES, TILE_VREGS, d), lambda b, i: (b, i, 0, 0, 0))
        out_shape = (bsz, s // ts, SUBLANES, TILE_VREGS, d)
    out = pl.pallas_call(
        functools.partial(_ffn_kernel, natural_out),
        grid=(bsz, s // (n_tiles * ts)),
        in_specs=[
            rows,
            pl.BlockSpec((None, None, 6, d), lambda b, i: (layer, b, 0, 0)),
            _layer_spec(ng.shape, layer),
            _layer_spec(w_up.shape, layer),
            _layer_spec(conv_w.shape, layer),
            _layer_spec(conv_b.shape, layer),
            _layer_spec(w_down.shape, layer),
        ],
        out_specs=out_tile,
        out_shape=jax.ShapeDtypeStruct(out_shape, F32),
        scratch_shapes=[
            pltpu.VMEM(((FFN_CONV - 1) * SUBLANES, w_up.shape[2]), F32),
        ],
        compiler_params=_params("arbitrary", "arbitrary"),
        name="ffn_layer",
    )(x, mod, ng, w_up, conv_w, conv_b, w_down)
    return out.reshape(bsz, s, d)


_STREAM_LEVELS = tuple(TILE_VREGS << i for i in (2, 1, 0))
_VREG_LEVELS = tuple(TILE_VREGS >> i for i in range(1, 6))
_MASKED_STREAM_LEVELS = _STREAM_LEVELS[1:]
_EXCLUDE = 1e30


def _tile_order_times(c):
    p = np.arange(c)
    return (p % SUBLANES) * (c // SUBLANES) + p // SUBLANES


def _gla_cumsum_matrix(c):
    t = _tile_order_times(c)
    return (t[None, :] <= t[:, None]).astype(np.float32)


def _gla_stream_masks(c):
    t = _tile_order_times(c)
    return np.stack([(t[:, None] // (2 * b) == t[None, :] // (2 * b)) for b in _MASKED_STREAM_LEVELS]
                    ).astype(np.float32)


def _vreg_level_rows(b):
    upper = [j for j in range(TILE_VREGS) if (j // b) % 2 == 1]
    lower = [j for j in range(TILE_VREGS) if (j // b) % 2 == 0]
    return upper, lower


def _gla_vreg_masks():
    out = []
    s = np.arange(SUBLANES)
    for b in _VREG_LEVELS:
        upper, lower = _vreg_level_rows(b)
        ju = np.repeat(np.array(upper), SUBLANES)[:, None]
        jl = np.repeat(np.array(lower), SUBLANES)[None, :]
        su = np.tile(s, len(upper))[:, None]
        sl = np.tile(s, len(lower))[None, :]
        out.append((su == sl) & (ju // (2 * b) == jl // (2 * b)))
    return np.stack(out).astype(np.float32)


def _dot_nt(a, b):
    return lax.dot_general(a, b, (((1,), (1,)), ((), ())), preferred_element_type=F32)


def _vreg(a, j):
    return a[j * SUBLANES:(j + 1) * SUBLANES]


def _gla_scores(proj, gc, state):
    c, dk = gc.shape
    dv = state.shape[1]
    vreg = _vreg
    q = proj[:, :dk] * (dk ** -0.5)
    k = proj[:, dk:2 * dk]
    v32 = proj[:, 2 * dk:2 * dk + dv]
    v = v32.astype(BF16)
    g_last = gc[c - 1:c, :]

    o_inter = _dot((q * jnp.exp(gc)).astype(BF16), state.astype(BF16))

    sub = _sublane((SUBLANES, 1))
    g_end = vreg(gc, TILE_VREGS - 1)
    tile_rows = lambda a: jnp.concatenate([a] * TILE_VREGS, axis=0)
    p_stream = []
    for b in _STREAM_LEVELS:
        w = b // TILE_VREGS
        g_mid = None
        for first in range(0, SUBLANES, 2 * w):
            row = jnp.broadcast_to(g_end[first + w - 1:first + w, :], (SUBLANES, dk))
            g_mid = row if g_mid is None else jnp.where(sub >= first, row, g_mid)
        upper = (lax.shift_right_logical(sub, w.bit_length() - 1) & 1) == 1
        g_q = tile_rows(jnp.where(upper, g_mid, _EXCLUDE))
        g_k = tile_rows(jnp.where(upper, -_EXCLUDE, g_mid))
        q_b = (q * jnp.exp(gc - g_q)).astype(BF16)
        k_b = (k * jnp.exp(g_k - gc)).astype(BF16)
        p_stream.append(_dot_nt(q_b, k_b))

    p_vreg = []
    for b in _VREG_LEVELS:
        upper_rows, _ = _vreg_level_rows(b)
        q_rows, k_rows = [], []
        for j in range(TILE_VREGS):
            jm = (j // (2 * b)) * (2 * b) + b - 1
            if j in upper_rows:
                q_rows.append(vreg(q, j) * jnp.exp(vreg(gc, j) - vreg(gc, jm)))
            elif j == jm:
                k_rows.append(vreg(k, j))
            else:
                k_rows.append(vreg(k, j) * jnp.exp(vreg(gc, jm) - vreg(gc, j)))
        p_vreg.append(_dot_nt(jnp.concatenate(q_rows, axis=0).astype(BF16),
                              jnp.concatenate(k_rows, axis=0).astype(BF16)))

    k_rev = (k * jnp.exp(g_last - gc)).astype(BF16)
    upd = lax.dot_general(k_rev, v, (((0,), (0,)), ((), ())), preferred_element_type=F32)
    decay_rows = jnp.broadcast_to(jnp.exp(g_last), (dk, dk))
    decay_col = jnp.transpose(decay_rows)[:, 0:1]
    new_state = decay_col * state + upd

    diag = jnp.sum(q * k, axis=-1, keepdims=True)
    return (o_inter, p_stream, p_vreg, diag), new_state


def _gla_combine(scores, proj, hn, sm_ref, vm_ref):
    o, p_stream, p_vreg, diag = scores
    dv = o.shape[1]
    dk = (proj.shape[1] - 2 * dv) // 2
    v32 = proj[:, 2 * dk:2 * dk + dv]
    r = proj[:, 2 * dk + dv:]
    v = v32.astype(BF16)

    attn = None
    for b, p in zip(_STREAM_LEVELS, p_stream):
        if b in _MASKED_STREAM_LEVELS:
            p = p * sm_ref[_MASKED_STREAM_LEVELS.index(b)]
        attn = p if attn is None else attn + p
    o = o + _dot(attn.astype(BF16), v) + diag * v32

    o_rows = [_vreg(o, j) for j in range(TILE_VREGS)]
    for li, (b, p) in enumerate(zip(_VREG_LEVELS, p_vreg)):
        upper_rows, lower_rows = _vreg_level_rows(b)
        p = (p * vm_ref[li]).astype(BF16)
        if b >= 2:
            v_low = jnp.concatenate(
                [v[j0 * SUBLANES:(j0 + b) * SUBLANES] for j0 in range(0, TILE_VREGS, 2 * b)], axis=0)
        else:
            v_low = jnp.concatenate([_vreg(v32, j) for j in lower_rows], axis=0).astype(BF16)
        o_up = _dot(p, v_low)
        for idx, j in enumerate(upper_rows):
            o_rows[j] = o_rows[j] + _vreg(o_up, idx)
    o = jnp.concatenate(o_rows, axis=0)
    return (_rms(o, hn) * (r * jax.nn.sigmoid(r))).astype(BF16)


def _gla_kernel(x_ref, mod_ref, ng_ref, wh_ref, wz_ref, wal_ref, bal_ref, hn_ref, cs_ref, sm_ref, vm_ref,
                wout_ref, o_ref, state_ref):
    c = ROW_TILE
    n_tiles = x_ref.shape[0] // c
    dk, dv = state_ref.shape[1:]

    @pl.when(pl.program_id(1) == 0)
    def _():
        state_ref[...] = jnp.zeros_like(state_ref)

    def prologue(t):
        x = x_ref[t * c:(t + 1) * c, :]
        h = (_rms(x, ng_ref[0:1, :]) * (1.0 + mod_ref[1:2, :]) + mod_ref[0:1, :]).astype(BF16)
        z = _dot(h, wz_ref[...])
        ga = _dot(z.astype(BF16), wal_ref[...]) + bal_ref[...]
        g = (jnp.minimum(ga, 0.0) - jnp.log1p(jnp.exp(-jnp.abs(ga)))) / GLA_TAU
        g_hi = g.astype(BF16)
        g_lo = (g - g_hi.astype(F32)).astype(BF16)
        cs = cs_ref[...]
        return x, h, _dot(cs, g_hi) + _dot(cs, g_lo)

    def epilogue(t, x, acc):
        o_ref[t * c:(t + 1) * c, :] = x + mod_ref[2:3, :] * _rms(acc, ng_ref[1:2, :])

    items = [(t, hh) for t in range(n_tiles) for hh in range(GLA_HEADS)]
    xs, hs, gcs, accs = {}, {}, {}, {}
    xs[0], hs[0], gcs[0] = prologue(0)

    def project(t, hh):
        return _dot(hs[t], wh_ref[hh])

    def out_project(t, hh, y):
        part = _dot(y, wout_ref[hh * dv:(hh + 1) * dv, :])
        accs[t] = part if hh == 0 else accs[t] + part
        if hh == GLA_HEADS - 1:
            epilogue(t, xs.pop(t), accs.pop(t))

    cur = project(0, 0)
    prev = None
    for idx, (t, hh) in enumerate(items):
        if hh == 1 and t + 1 < n_tiles:
            xs[t + 1], hs[t + 1], gcs[t + 1] = prologue(t + 1)
        nxt = project(*items[idx + 1]) if idx + 1 < len(items) else None
        if prev is not None:
            out_project(*prev)
        scores, state_ref[hh] = _gla_scores(cur, gcs[t][:, hh * dk:(hh + 1) * dk], state_ref[hh])
        prev = (t, hh, _gla_combine(scores, cur, hn_ref[...], sm_ref, vm_ref))
        cur = nxt
    out_project(*prev)


def _gla_layer(x, mod, ng, w_heads, w_z, w_alpha, b_alpha, head_norm, w_out):
    bsz, s, d = x.shape
    c = ROW_TILE
    dk = w_alpha.shape[1] // GLA_HEADS
    dv = d // GLA_HEADS
    cumsum_mat = jnp.asarray(_gla_cumsum_matrix(c), BF16)
    stream_masks = jnp.asarray(_gla_stream_masks(c), F32)
    vreg_masks = jnp.asarray(_gla_vreg_masks(), F32)
    n_tiles = GLA_TILES_PER_STEP
    tile = pl.BlockSpec((None, n_tiles * c, d), lambda b, i: (b, i, 0))
    return pl.pallas_call(
        _gla_kernel,
        grid=(bsz, s // (n_tiles * c)),
        in_specs=[
            tile,
            pl.BlockSpec((None, 6, d), lambda b, i: (b, 0, 0)),
            _const_spec(ng.shape),
            _const_spec(w_heads.shape),
            _const_spec(w_z.shape),
            _const_spec(w_alpha.shape),
            _const_spec(b_alpha.shape),
            _const_spec(head_norm.shape),
            _const_spec(cumsum_mat.shape),
            _const_spec(stream_masks.shape),
            _const_spec(vreg_masks.shape),
            _const_spec(w_out.shape),
        ],
        out_specs=tile,
        out_shape=jax.ShapeDtypeStruct(x.shape, F32),
        scratch_shapes=[pltpu.VMEM((GLA_HEADS, dk, dv), F32)],
        compiler_params=_params("arbitrary", "arbitrary"),
        name="gla_layer",
    )(x, mod, ng, w_heads, w_z, w_alpha, b_alpha, head_norm, cumsum_mat, stream_masks, vreg_masks, w_out)


def kernel(x, c, ada_w, ada_b, norm_g, ffn_w_up, ffn_conv_w, ffn_conv_b, ffn_w_down, rg_w_in, rg_conv_w,
           rg_conv_b, rg_wa, rg_ba, rg_wx, rg_bx, rg_lambda, rg_w_out, gla_w_in, gla_w_alpha, gla_b_alpha,
           gla_norm_g, gla_w_out):
    depth = ada_w.shape[0]
    d = x.shape[2]
    assert depth >= 1 and x.shape[1] % (ROW_TILE * FFN_TILES_PER_STEP) == 0
    mod = _ada_modulation(c, ada_w, ada_b)
    ffn_w_up_bf = ffn_w_up.astype(BF16)
    ffn_w_down_bf = ffn_w_down.astype(BF16)
    ffn_conv_b3 = ffn_conv_b[:, None, :]
    for i in range(depth):
        j = i // 2
        if i % 2 == 0:
            vec = jnp.stack([rg_conv_b[j], rg_ba[j], rg_bx[j], rg_lambda[j]])
            w_gate = jnp.concatenate([rg_wa[j], rg_wx[j]], axis=-1).astype(BF16)
            x = _rglru_layer(x, mod[i], norm_g[i], rg_w_in[j].astype(BF16), rg_conv_w[j], vec, w_gate,
                             rg_w_out[j].astype(BF16), natural_in=(i == 0))
        else:
            qk = gla_w_alpha.shape[2]
            n_main = 2 * qk + 2 * d
            rank = gla_w_in.shape[2] - n_main
            w_in = gla_w_in[j].astype(BF16)
            dk, dv = qk // GLA_HEADS, d // GLA_HEADS
            w_heads = jnp.stack([
                jnp.concatenate([w_in[:, hh * dk:(hh + 1) * dk],
                                 w_in[:, qk + hh * dk:qk + (hh + 1) * dk],
                                 w_in[:, 2 * qk + hh * dv:2 * qk + (hh + 1) * dv],
                                 w_in[:, 2 * qk + d + hh * dv:2 * qk + d + (hh + 1) * dv]], axis=1)
                for hh in range(GLA_HEADS)])
            w_z = jnp.zeros((d, GLA_RANK_PAD), BF16).at[:, :rank].set(w_in[:, n_main:])
            w_alpha = jnp.zeros((GLA_RANK_PAD, qk), BF16).at[:rank].set(gla_w_alpha[j].astype(BF16))
            x = _gla_layer(x, mod[i], norm_g[i], w_heads, w_z, w_alpha, gla_b_alpha[j][None, :],
                           gla_norm_g[j][None, :], gla_w_out[j].astype(BF16))
        x = _ffn_layer(x, mod, norm_g, ffn_w_up_bf, ffn_conv_w, ffn_conv_b3, ffn_w_down_bf, i,
                       natural_out=(i == depth - 1))
    return x
```

```python
import functools
import math

import jax
import jax.numpy as jnp
import numpy as np
from jax import lax
from jax.experimental import pallas as pl
from jax.experimental.pallas import tpu as pltpu

F32 = jnp.float32
BF16 = jnp.bfloat16

EPS = 1e-6
RG_BLOCKS = 4
RG_C = 8.0
RG_CONV = 4
FFN_CONV = 3
GLA_HEADS = 4
GLA_TAU = 16.0
GLA_RANK_PAD = 128
SUBLANES = 8

ROW_TILE = 256
TILE_VREGS = ROW_TILE // SUBLANES
TILES_PER_STEP = 2
FFN_COL_CHUNK = 256
VMEM_LIMIT = 56 * 1024 * 1024

_GELU_C0 = 2.0 * math.sqrt(2.0 / math.pi) * math.log2(math.e)
_GELU_C1 = _GELU_C0 * 0.044715


def _dot(a, b):
    return jnp.dot(a, b, preferred_element_type=F32)


def _dot_nt(a, b):
    return lax.dot_general(a, b, (((1,), (1,)), ((), ())), preferred_element_type=F32)


def _rms(x, g):
    return x * lax.rsqrt(jnp.mean(x * x, axis=-1, keepdims=True) + EPS) * g


def _gelu_times(x, v):
    neg_2z_log2e = x * (-_GELU_C0 - _GELU_C1 * (x * x))
    return (x * v) / (1.0 + jnp.exp2(neg_2z_log2e))


def _sublane(shape):
    return lax.broadcasted_iota(jnp.int32, shape, 0) % SUBLANES


def _vreg(a, j):
    return a[j * SUBLANES:(j + 1) * SUBLANES]


def _load_tile_order(ref):
    return jnp.concatenate([ref[:, j, :] for j in range(TILE_VREGS)], axis=0)


def _store_natural_order(ref, val):
    for j in range(TILE_VREGS):
        ref[:, j, :] = _vreg(val, j)


def _delay(x, tail, k):
    ts = x.shape[0]
    n_tail = tail.shape[0] // SUBLANES
    last = _sublane((SUBLANES, 1)) == SUBLANES - 1
    head = []
    for j in range(k):
        cur = _vreg(x, TILE_VREGS - k + j)
        prv = _vreg(tail, n_tail - k + j)
        head.append(pltpu.roll(jnp.where(last, prv, cur), 1, 0))
    return jnp.concatenate(head + [x[:ts - k * SUBLANES]], axis=0)


def _params(*sem):
    return pltpu.CompilerParams(dimension_semantics=sem, vmem_limit_bytes=VMEM_LIMIT)


def _const_spec(shape):
    n = len(shape)
    return pl.BlockSpec(shape, lambda *_: (0,) * n, pipeline_mode=pl.Buffered(1))


def _layer_spec(shape, layer):
    n = len(shape) - 1
    return pl.BlockSpec((None,) + tuple(shape[1:]), lambda *_: (layer,) + (0,) * n,
                        pipeline_mode=pl.Buffered(1))


def _ada_kernel(c_ref, w_ref, b_ref, o_ref):
    c = c_ref[...]
    c_act = (c * jax.nn.sigmoid(c)).astype(BF16)
    o_ref[...] = _dot(c_act, w_ref[...].astype(BF16)) + b_ref[...]


def _ada_modulation(c, ada_w, ada_b):
    depth, d, n = ada_w.shape
    bsz = c.shape[0]
    c_pad = jnp.zeros((SUBLANES, d), F32).at[:bsz].set(c)
    tn = 1024
    out = pl.pallas_call(
        _ada_kernel,
        grid=(depth, n // tn),
        in_specs=[
            pl.BlockSpec((SUBLANES, d), lambda l, j: (0, 0)),
            pl.BlockSpec((None, d, tn), lambda l, j: (l, 0, j)),
            pl.BlockSpec((None, 1, tn), lambda l, j: (l, 0, j)),
        ],
        out_specs=pl.BlockSpec((None, SUBLANES, tn), lambda l, j: (l, 0, j)),
        out_shape=jax.ShapeDtypeStruct((depth, SUBLANES, n), F32),
        compiler_params=_params("arbitrary", "arbitrary"),
        name="ada_mod",
    )(c_pad, ada_w, ada_b.reshape(depth, 1, n))
    return out[:, :bsz].reshape(depth, bsz, 6, d)


def _stream_scan(a, u, carry):
    hl = [_vreg(u, 0)]
    al = [_vreg(a, 0)]
    for j in range(1, TILE_VREGS):
        hl.append(_vreg(a, j) * hl[-1] + _vreg(u, j))
        al.append(_vreg(a, j) * al[-1])
    sub = _sublane((SUBLANES, 1))
    ea, eh = al[-1], hl[-1]
    for k in (1, 2, 4):
        keep = sub >= k
        ea_sh = jnp.where(keep, pltpu.roll(ea, k, 0), 1.0)
        eh_sh = jnp.where(keep, pltpu.roll(eh, k, 0), 0.0)
        eh = ea * eh_sh + eh
        ea = ea * ea_sh
    after = ea * carry + eh
    before = jnp.where(sub == 0, carry, pltpu.roll(after, 1, 0))
    h = jnp.concatenate([hl[j] + al[j] * before for j in range(TILE_VREGS)], axis=0)
    return h, after


def _rglru_kernel(natural_in, x_ref, mod_ref, ng_ref, win_ref, cw_ref, vec_ref, wg_ref, wout_ref, o_ref,
                  tail_ref, hc_ref):
    ts = ROW_TILE
    d = o_ref.shape[1]
    n_tiles = o_ref.shape[0] // ts
    blk = d // RG_BLOCKS
    n_tail = RG_CONV - 1

    @pl.when(pl.program_id(1) == 0)
    def _():
        tail_ref[...] = jnp.zeros_like(tail_ref)
        hc_ref[...] = jnp.zeros_like(hc_ref)

    def prologue(t):
        x = _load_tile_order(x_ref.at[t]) if natural_in else x_ref[t * ts:(t + 1) * ts, :]
        return x, (_rms(x, ng_ref[0:1, :]) * (1.0 + mod_ref[1:2, :]) + mod_ref[0:1, :]).astype(BF16)

    def epilogue(t, x, acc):
        o_ref[t * ts:(t + 1) * ts, :] = x + mod_ref[2:3, :] * _rms(acc, ng_ref[1:2, :])

    def project(t, g):
        return (_dot(hs[t], win_ref[:, g * blk:(g + 1) * blk]),
                _dot(hs[t], win_ref[:, d + g * blk:d + (g + 1) * blk]))

    def mix(g, gate_br, xb):
        cols = slice(g * blk, (g + 1) * blk)
        tail = tail_ref[:, cols]
        tail_ref[:, cols] = xb[ts - n_tail * SUBLANES:, :]
        xc = vec_ref[0:1, cols] + _delay(xb, tail, 3) * cw_ref[0:1, cols]
        xc = xc + _delay(xb, tail, 2) * cw_ref[1:2, cols]
        xc = xc + _delay(xb, tail, 1) * cw_ref[2:3, cols]
        xc = xc + xb * cw_ref[3:4, cols]
        p = _dot(xc.astype(BF16), wg_ref[g])
        r = jax.nn.sigmoid(p[:, :blk] + vec_ref[1:2, cols])
        i_g = jax.nn.sigmoid(p[:, blk:] + vec_ref[2:3, cols])
        nl = -vec_ref[3:4, cols]
        softplus_nl = jnp.maximum(nl, 0.0) + jnp.log1p(jnp.exp(-jnp.abs(nl)))
        log_a = r * ((-RG_C) * softplus_nl)
        a = jnp.exp(log_a)
        m2 = jnp.tanh(log_a) * (-1.0 - a * a)
        u = jnp.where(m2 > 0.0, m2 * lax.rsqrt(m2), 0.0) * (i_g * xc)
        hs_g, after = _stream_scan(a, u, hc_ref[SUBLANES - 1:SUBLANES, cols])
        hc_ref[:, cols] = after
        return _gelu_times(gate_br, hs_g).astype(BF16)

    items = [(t, g) for t in range(n_tiles) for g in range(RG_BLOCKS)]
    xs, hs, accs = {}, {}, {}
    xs[0], hs[0] = prologue(0)

    def out_project(t, g, y):
        part = _dot(y, wout_ref[g * blk:(g + 1) * blk, :])
        accs[t] = part if g == 0 else accs[t] + part
        if g == RG_BLOCKS - 1:
            epilogue(t, xs.pop(t), accs.pop(t))

    cur = project(0, 0)
    prev = None
    for idx, (t, g) in enumerate(items):
        if g == 1 and t + 1 < n_tiles:
            xs[t + 1], hs[t + 1] = prologue(t + 1)
        nxt = project(*items[idx + 1]) if idx + 1 < len(items) else None
        if prev is not None:
            out_project(*prev)
        prev = (t, g, mix(g, *cur))
        cur = nxt
    out_project(*prev)


def _rglru_layer(x, mod, ng, w_in, conv_w, vec, w_gate, w_out, natural_in):
    bsz, s, d = x.shape
    ts = ROW_TILE
    n_tiles = TILES_PER_STEP
    tile = pl.BlockSpec((None, n_tiles * ts, d), lambda b, i: (b, i, 0))
    x_tile = tile
    if natural_in:
        x = x.reshape(bsz, s // ts, SUBLANES, TILE_VREGS, d)
        x_tile = pl.BlockSpec((None, n_tiles, SUBLANES, TILE_VREGS, d), lambda b, i: (b, i, 0, 0, 0))
    return pl.pallas_call(
        functools.partial(_rglru_kernel, natural_in),
        grid=(bsz, s // (n_tiles * ts)),
        in_specs=[
            x_tile,
            pl.BlockSpec((None, 6, d), lambda b, i: (b, 0, 0)),
            _const_spec(ng.shape),
            _const_spec(w_in.shape),
            _const_spec(conv_w.shape),
            _const_spec(vec.shape),
            _const_spec(w_gate.shape),
            _const_spec(w_out.shape),
        ],
        out_specs=tile,
        out_shape=jax.ShapeDtypeStruct((bsz, s, d), F32),
        scratch_shapes=[
            pltpu.VMEM(((RG_CONV - 1) * SUBLANES, d), F32),
            pltpu.VMEM((SUBLANES, d), F32),
        ],
        compiler_params=_params("arbitrary", "arbitrary"),
        name="rglru_layer",
    )(x, mod, ng, w_in, conv_w, vec, w_gate, w_out)


def _ffn_kernel(natural_out, x_ref, mod_ref, ng_ref, wup_ref, cw_ref, cb_ref, wdn_ref, o_ref, tail_ref):
    ts = ROW_TILE
    n_tiles = x_ref.shape[0] // ts
    f = wdn_ref.shape[0]
    fc = FFN_COL_CHUNK
    n_chunks = f // fc
    n_tail = FFN_CONV - 1

    @pl.when(pl.program_id(1) == 0)
    def _():
        tail_ref[...] = jnp.zeros_like(tail_ref)

    def prologue(t):
        x = x_ref[t * ts:(t + 1) * ts, :]
        return x, (_rms(x, ng_ref[2:3, :]) * (1.0 + mod_ref[4:5, :]) + mod_ref[3:4, :]).astype(BF16)

    def epilogue(t, x, acc):
        out = x + mod_ref[5:6, :] * _rms(acc, ng_ref[3:4, :])
        if natural_out:
            _store_natural_order(o_ref.at[t], out)
        else:
            o_ref[t * ts:(t + 1) * ts, :] = out

    def conv_cols(up, c0):
        cols = slice(c0, c0 + fc)
        tail = tail_ref[:, cols]
        tail_ref[:, cols] = up[ts - n_tail * SUBLANES:, :]
        y = cb_ref[:, cols] + _delay(up, tail, 2) * cw_ref[0:1, cols]
        y = y + _delay(up, tail, 1) * cw_ref[1:2, cols]
        return y + up * cw_ref[2:3, cols]

    items = [(t, c) for t in range(n_tiles) for c in range(n_chunks)]
    xs, hs, accs = {}, {}, {}
    xs[0], hs[0] = prologue(0)

    def up_pair(t, c):
        return (_dot(hs[t], wup_ref[:, c * fc:(c + 1) * fc]), _dot(hs[t], wup_ref[:, f + c * fc:f + (c + 1) * fc]))

    def down(t, c, act):
        part = _dot(act, wdn_ref[c * fc:(c + 1) * fc, :])
        accs[t] = part if c == 0 else accs[t] + part
        if c == n_chunks - 1:
            epilogue(t, xs.pop(t), accs.pop(t))

    ups = up_pair(0, 0)
    prev = None
    for idx, (t, c) in enumerate(items):
        if c == n_chunks - 4 and t + 1 < n_tiles:
            xs[t + 1], hs[t + 1] = prologue(t + 1)
        nxt = up_pair(*items[idx + 1]) if idx + 1 < len(items) else None
        if prev is not None:
            down(*prev)
        g = conv_cols(ups[0], c * fc)
        val = conv_cols(ups[1], f + c * fc)
        prev = (t, c, _gelu_times(g, val).astype(BF16))
        ups = nxt
    down(*prev)


def _ffn_layer(x, mod, ng, w_up, conv_w, conv_b, w_down, layer, natural_out):
    bsz, s, d = x.shape
    ts = ROW_TILE
    n_tiles = TILES_PER_STEP
    rows = pl.BlockSpec((None, n_tiles * ts, d), lambda b, i: (b, i, 0))
    out_tile, out_shape = rows, (bsz, s, d)
    if natural_out:
        out_tile = pl.BlockSpec((None, n_tiles, SUBLANES, TILE_VREGS, d), lambda b, i: (b, i, 0, 0, 0))
        out_shape = (bsz, s // ts, SUBLANES, TILE_VREGS, d)
    out = pl.pallas_call(
        functools.partial(_ffn_kernel, natural_out),
        grid=(bsz, s // (n_tiles * ts)),
        in_specs=[
            rows,
            pl.BlockSpec((None, None, 6, d), lambda b, i: (layer, b, 0, 0)),
            _layer_spec(ng.shape, layer),
            _layer_spec(w_up.shape, layer),
            _layer_spec(conv_w.shape, layer),
            _layer_spec(conv_b.shape, layer),
            _layer_spec(w_down.shape, layer),
        ],
        out_specs=out_tile,
        out_shape=jax.ShapeDtypeStruct(out_shape, F32),
        scratch_shapes=[
            pltpu.VMEM(((FFN_CONV - 1) * SUBLANES, w_up.shape[2]), F32),
        ],
        compiler_params=_params("arbitrary", "arbitrary"),
        name="ffn_layer",
    )(x, mod, ng, w_up, conv_w, conv_b, w_down)
    return out.reshape(bsz, s, d)


_STREAM_LEVELS = tuple(TILE_VREGS << i for i in (2, 1, 0))
_VREG_LEVELS = tuple(TILE_VREGS >> i for i in range(1, 4))
_DIRECT_BLOCK = 4
_MASKED_STREAM_LEVELS = _STREAM_LEVELS[1:]
_EXCLUDE = 1e30
_GLA_PROJ_PIECES = 3
_GLA_OUT_PIECES = 2


def _tile_order_times(c):
    p = np.arange(c)
    return (p % SUBLANES) * (c // SUBLANES) + p // SUBLANES


def _gla_stream_masks(c):
    t = _tile_order_times(c)
    return np.stack([(t[:, None] // (2 * b) == t[None, :] // (2 * b)) for b in _MASKED_STREAM_LEVELS]
                    ).astype(np.float32)


def _vreg_level_rows(b):
    upper = [j for j in range(TILE_VREGS) if (j // b) % 2 == 1]
    lower = [j for j in range(TILE_VREGS) if (j // b) % 2 == 0]
    return upper, lower


def _gla_vreg_masks():
    out = []
    s = np.arange(SUBLANES)
    for b in _VREG_LEVELS:
        upper, lower = _vreg_level_rows(b)
        ju = np.repeat(np.array(upper), SUBLANES)[:, None]
        jl = np.repeat(np.array(lower), SUBLANES)[None, :]
        su = np.tile(s, len(upper))[:, None]
        sl = np.tile(s, len(lower))[None, :]
        out.append((su == sl) & (ju // (2 * b) == jl // (2 * b)))
    return np.stack(out).astype(np.float32)


def _cumsum_tile_order(g):
    rows = [_vreg(g, 0)]
    for j in range(1, TILE_VREGS):
        rows.append(rows[-1] + _vreg(g, j))
    total = rows[-1]
    sub = _sublane((SUBLANES, 1))
    incl = total
    for k in (1, 2, 4):
        incl = incl + jnp.where(sub >= k, pltpu.roll(incl, k, 0), 0.0)
    before = incl - total
    return jnp.concatenate([r + before for r in rows], axis=0)


def _gla_scores(qk, v32, gc, state, fill):
    c, dk = gc.shape
    vreg = _vreg
    q = qk[:, :dk] * (dk ** -0.5)
    k = qk[:, dk:]
    v = v32.astype(BF16)
    g_last = gc[c - 1:c, :]

    fill()
    o_inter = _dot((q * jnp.exp(gc)).astype(BF16), state.astype(BF16))

    sub = _sublane((SUBLANES, 1))
    g_end = vreg(gc, TILE_VREGS - 1)
    tile_rows = lambda a: jnp.concatenate([a] * TILE_VREGS, axis=0)
    p_stream = []
    for b in _STREAM_LEVELS:
        w = b // TILE_VREGS
        g_mid = None
        for first in range(0, SUBLANES, 2 * w):
            row = jnp.broadcast_to(g_end[first + w - 1:first + w, :], (SUBLANES, dk))
            g_mid = row if g_mid is None else jnp.where(sub >= first, row, g_mid)
        upper = (lax.shift_right_logical(sub, w.bit_length() - 1) & 1) == 1
        g_q = tile_rows(jnp.where(upper, g_mid, _EXCLUDE))
        g_k = tile_rows(jnp.where(upper, -_EXCLUDE, g_mid))
        q_b = (q * jnp.exp(gc - g_q)).astype(BF16)
        k_b = (k * jnp.exp(g_k - gc)).astype(BF16)
        p_stream.append(_dot_nt(q_b, k_b))
    fill()

    p_vreg = []
    for b in _VREG_LEVELS:
        upper_rows, _ = _vreg_level_rows(b)
        q_rows, k_rows = [], []
        for j in range(TILE_VREGS):
            jm = (j // (2 * b)) * (2 * b) + b - 1
            if j in upper_rows:
                q_rows.append(vreg(q, j) * jnp.exp(vreg(gc, j) - vreg(gc, jm)))
            elif j == jm:
                k_rows.append(vreg(k, j))
            else:
                k_rows.append(vreg(k, j) * jnp.exp(vreg(gc, jm) - vreg(gc, j)))
        p_vreg.append(_dot_nt(jnp.concatenate(q_rows, axis=0).astype(BF16),
                              jnp.concatenate(k_rows, axis=0).astype(BF16)))
    fill()

    k_rev = (k * jnp.exp(g_last - gc)).astype(BF16)
    upd = lax.dot_general(k_rev, v, (((0,), (0,)), ((), ())), preferred_element_type=F32)
    decay_rows = jnp.broadcast_to(jnp.exp(g_last), (dk, dk))
    decay_col = jnp.transpose(decay_rows)[:, 0:1]
    new_state = decay_col * state + upd

    direct = []
    for ju in range(TILE_VREGS):
        terms = vreg(v32, ju) * jnp.sum(vreg(q, ju) * vreg(k, ju), axis=-1, keepdims=True)
        for jl in range((ju // _DIRECT_BLOCK) * _DIRECT_BLOCK, ju):
            w = vreg(q, ju) * vreg(k, jl) * jnp.exp(vreg(gc, ju) - vreg(gc, jl))
            terms = terms + vreg(v32, jl) * jnp.sum(w, axis=-1, keepdims=True)
        direct.append(terms)
    return (o_inter, p_stream, p_vreg, direct), new_state


def _gla_combine(scores, v32, r, hn, sm_ref, vm_ref, fill):
    o, p_stream, p_vreg, direct = scores
    v = v32.astype(BF16)

    attn = None
    for b, p in zip(_STREAM_LEVELS, p_stream):
        if b in _MASKED_STREAM_LEVELS:
            p = p * sm_ref[_MASKED_STREAM_LEVELS.index(b)]
        attn = p if attn is None else attn + p
    o = o + _dot(attn.astype(BF16), v)
    fill()

    o_rows = [_vreg(o, j) + direct[j] for j in range(TILE_VREGS)]
    for li, (b, p) in enumerate(zip(_VREG_LEVELS, p_vreg)):
        upper_rows, _ = _vreg_level_rows(b)
        p = (p * vm_ref[li]).astype(BF16)
        v_low = jnp.concatenate(
            [v[j0 * SUBLANES:(j0 + b) * SUBLANES] for j0 in range(0, TILE_VREGS, 2 * b)], axis=0)
        o_up = _dot(p, v_low)
        if li == 1:
            fill()
        for idx, j in enumerate(upper_rows):
            o_rows[j] = o_rows[j] + _vreg(o_up, idx)
    o = jnp.concatenate(o_rows, axis=0)
    return (_rms(o, hn) * (r * jax.nn.sigmoid(r))).astype(BF16)


def _gla_kernel(x_ref, mod_ref, ng_ref, wh_ref, wz_ref, wal_ref, bal_ref, hn_ref, sm_ref, vm_ref, wout_ref,
                o_ref, state_ref):
    c = ROW_TILE
    d = x_ref.shape[1]
    n_tiles = x_ref.shape[0] // c
    dk, dv = state_ref.shape[1:]

    @pl.when(pl.program_id(1) == 0)
    def _():
        state_ref[...] = jnp.zeros_like(state_ref)

    def prologue(t):
        x = x_ref[t * c:(t + 1) * c, :]
        h = (_rms(x, ng_ref[0:1, :]) * (1.0 + mod_ref[1:2, :]) + mod_ref[0:1, :]).astype(BF16)
        z = _dot(h, wz_ref[...])
        ga = _dot(z.astype(BF16), wal_ref[...]) + bal_ref[...]
        g = (jnp.minimum(ga, 0.0) - jnp.log1p(jnp.exp(-jnp.abs(ga)))) / GLA_TAU
        return x, h, _cumsum_tile_order(g)

    def epilogue(t, x, acc):
        o_ref[t * c:(t + 1) * c, :] = x + mod_ref[2:3, :] * _rms(acc, ng_ref[1:2, :])

    items = [(t, hh) for t in range(n_tiles) for hh in range(GLA_HEADS)]
    xs, hs, gcs, accs = {}, {}, {}, {}
    xs[0], hs[0], gcs[0] = prologue(0)

    def project_piece(t, hh, j):
        w = 2 * dk if j == 0 else dv
        c0 = 0 if j == 0 else 2 * dk + (j - 1) * dv
        return _dot(hs[t], wh_ref[hh, :, c0:c0 + w])

    def out_piece(t, hh, y, j):
        w = d // _GLA_OUT_PIECES
        part = _dot(y, wout_ref[hh * dv:(hh + 1) * dv, j * w:(j + 1) * w])
        accs[t, j] = part if hh == 0 else accs[t, j] + part
        if hh == GLA_HEADS - 1 and j == _GLA_OUT_PIECES - 1:
            acc = jnp.concatenate([accs.pop((t, jj)) for jj in range(_GLA_OUT_PIECES)], axis=1)
            epilogue(t, xs.pop(t), acc)

    cur = [project_piece(0, 0, j) for j in range(_GLA_PROJ_PIECES)]
    prev = None
    for idx, (t, hh) in enumerate(items):
        if hh == 1 and t + 1 < n_tiles:
            xs[t + 1], hs[t + 1], gcs[t + 1] = prologue(t + 1)
        nxt, pending = [], []
        if idx + 1 < len(items):
            pending += [lambda j=j, idx=idx, nxt=nxt: nxt.append(project_piece(*items[idx + 1], j))
                        for j in range(_GLA_PROJ_PIECES)]
        if prev is not None:
            pending += [functools.partial(out_piece, *prev, j) for j in range(_GLA_OUT_PIECES)]
        pending.reverse()
        fill = lambda pending=pending: pending.pop()() if pending else None
        qk, v32, r = cur
        scores, state_ref[hh] = _gla_scores(qk, v32, gcs[t][:, hh * dk:(hh + 1) * dk], state_ref[hh], fill)
        prev = (t, hh, _gla_combine(scores, v32, r, hn_ref[...], sm_ref, vm_ref, fill))
        while pending:
            fill()
        cur = nxt
    for j in range(_GLA_OUT_PIECES):
        out_piece(*prev, j)


def _gla_layer(x, mod, ng, w_heads, w_z, w_alpha, b_alpha, head_norm, w_out):
    bsz, s, d = x.shape
    c = ROW_TILE
    dk = w_alpha.shape[1] // GLA_HEADS
    dv = d // GLA_HEADS
    stream_masks = jnp.asarray(_gla_stream_masks(c), F32)
    vreg_masks = jnp.asarray(_gla_vreg_masks(), F32)
    n_tiles = TILES_PER_STEP
    tile = pl.BlockSpec((None, n_tiles * c, d), lambda b, i: (b, i, 0))
    return pl.pallas_call(
        _gla_kernel,
        grid=(bsz, s // (n_tiles * c)),
        in_specs=[
            tile,
            pl.BlockSpec((None, 6, d), lambda b, i: (b, 0, 0)),
            _const_spec(ng.shape),
            _const_spec(w_heads.shape),
            _const_spec(w_z.shape),
            _const_spec(w_alpha.shape),
            _const_spec(b_alpha.shape),
            _const_spec(head_norm.shape),
            _const_spec(stream_masks.shape),
            _const_spec(vreg_masks.shape),
            _const_spec(w_out.shape),
        ],
        out_specs=tile,
        out_shape=jax.ShapeDtypeStruct(x.shape, F32),
        scratch_shapes=[pltpu.VMEM((GLA_HEADS, dk, dv), F32)],
        compiler_params=_params("arbitrary", "arbitrary"),
        name="gla_layer",
    )(x, mod, ng, w_heads, w_z, w_alpha, b_alpha, head_norm, stream_masks, vreg_masks, w_out)


def kernel(x, c, ada_w, ada_b, norm_g, ffn_w_up, ffn_conv_w, ffn_conv_b, ffn_w_down, rg_w_in, rg_conv_w,
           rg_conv_b, rg_wa, rg_ba, rg_wx, rg_bx, rg_lambda, rg_w_out, gla_w_in, gla_w_alpha, gla_b_alpha,
           gla_norm_g, gla_w_out):
    depth = ada_w.shape[0]
    d = x.shape[2]
    assert depth >= 1 and x.shape[1] % (ROW_TILE * TILES_PER_STEP) == 0
    mod = _ada_modulation(c, ada_w, ada_b)
    ffn_w_up_bf = ffn_w_up.astype(BF16)
    ffn_w_down_bf = ffn_w_down.astype(BF16)
    ffn_conv_b3 = ffn_conv_b[:, None, :]
    for i in range(depth):
        j = i // 2
        if i % 2 == 0:
            vec = jnp.stack([rg_conv_b[j], rg_ba[j], rg_bx[j], rg_lambda[j]])
            w_gate = jnp.concatenate([rg_wa[j], rg_wx[j]], axis=-1).astype(BF16)
            x = _rglru_layer(x, mod[i], norm_g[i], rg_w_in[j].astype(BF16), rg_conv_w[j], vec, w_gate,
                             rg_w_out[j].astype(BF16), natural_in=(i == 0))
        else:
            qk = gla_w_alpha.shape[2]
            n_main = 2 * qk + 2 * d
            rank = gla_w_in.shape[2] - n_main
            w_in = gla_w_in[j].astype(BF16)
            dk, dv = qk // GLA_HEADS, d // GLA_HEADS
            w_heads = jnp.stack([
                jnp.concatenate([w_in[:, hh * dk:(hh + 1) * dk],
                                 w_in[:, qk + hh * dk:qk + (hh + 1) * dk],
                                 w_in[:, 2 * qk + hh * dv:2 * qk + (hh + 1) * dv],
                                 w_in[:, 2 * qk + d + hh * dv:2 * qk + d + (hh + 1) * dv]], axis=1)
                for hh in range(GLA_HEADS)])
            w_z = jnp.zeros((d, GLA_RANK_PAD), BF16).at[:, :rank].set(w_in[:, n_main:])
            w_alpha = jnp.zeros((GLA_RANK_PAD, qk), BF16).at[:rank].set(gla_w_alpha[j].astype(BF16))
            x = _gla_layer(x, mod[i], norm_g[i], w_heads, w_z, w_alpha, gla_b_alpha[j][None, :],
                           gla_norm_g[j][None, :], gla_w_out[j].astype(BF16))
        x = _ffn_layer(x, mod, norm_g, ffn_w_up_bf, ffn_conv_w, ffn_conv_b3, ffn_w_down_bf, i,
                       natural_out=(i == depth - 1))
    return x
```

```python
import functools
import math

import jax
import jax.numpy as jnp
import numpy as np
from jax import lax
from jax.experimental import pallas as pl
from jax.experimental.pallas import tpu as pltpu

F32 = jnp.float32
BF16 = jnp.bfloat16

EPS = 1e-6
RG_BLOCKS = 4
RG_C = 8.0
RG_CONV = 4
FFN_CONV = 3
GLA_HEADS = 4
GLA_TAU = 16.0
GLA_RANK_PAD = 128
SUBLANES = 8

ROW_TILE = 256
TILE_VREGS = ROW_TILE // SUBLANES
TILES_PER_STEP = 4
FFN_COL_CHUNK = 256
VMEM_LIMIT = 56 * 1024 * 1024

_GELU_C0 = 2.0 * math.sqrt(2.0 / math.pi) * math.log2(math.e)
_GELU_C1 = _GELU_C0 * 0.044715


def _dot(a, b):
    return jnp.dot(a, b, preferred_element_type=F32)


def _dot_nt(a, b):
    return lax.dot_general(a, b, (((1,), (1,)), ((), ())), preferred_element_type=F32)


def _rms(x, g):
    return x * lax.rsqrt(jnp.mean(x * x, axis=-1, keepdims=True) + EPS) * g


def _gelu_times(x, v):
    neg_2z_log2e = x * (-_GELU_C0 - _GELU_C1 * (x * x))
    return (x * v) / (1.0 + jnp.exp2(neg_2z_log2e))


def _sublane(shape):
    return lax.broadcasted_iota(jnp.int32, shape, 0) % SUBLANES


def _vreg(a, j):
    return a[j * SUBLANES:(j + 1) * SUBLANES]


def _load_tile_order(ref):
    return jnp.concatenate([ref[:, j, :] for j in range(TILE_VREGS)], axis=0)


def _store_natural_order(ref, val):
    for j in range(TILE_VREGS):
        ref[:, j, :] = _vreg(val, j)


def _delay(x, tail, k):
    ts = x.shape[0]
    n_tail = tail.shape[0] // SUBLANES
    last = _sublane((SUBLANES, 1)) == SUBLANES - 1
    head = []
    for j in range(k):
        cur = _vreg(x, TILE_VREGS - k + j)
        prv = _vreg(tail, n_tail - k + j)
        head.append(pltpu.roll(jnp.where(last, prv, cur), 1, 0))
    return jnp.concatenate(head + [x[:ts - k * SUBLANES]], axis=0)


def _params(*sem):
    return pltpu.CompilerParams(dimension_semantics=sem, vmem_limit_bytes=VMEM_LIMIT)


def _const_spec(shape):
    n = len(shape)
    return pl.BlockSpec(shape, lambda *_: (0,) * n, pipeline_mode=pl.Buffered(1))


def _layer_spec(shape, layer):
    n = len(shape) - 1
    return pl.BlockSpec((None,) + tuple(shape[1:]), lambda *_: (layer,) + (0,) * n,
                        pipeline_mode=pl.Buffered(1))


def _ada_kernel(c_ref, w_ref, b_ref, o_ref):
    c = c_ref[...]
    c_act = (c * jax.nn.sigmoid(c)).astype(BF16)
    o_ref[...] = _dot(c_act, w_ref[...].astype(BF16)) + b_ref[...]


def _ada_modulation(c, ada_w, ada_b):
    depth, d, n = ada_w.shape
    bsz = c.shape[0]
    c_pad = jnp.zeros((SUBLANES, d), F32).at[:bsz].set(c)
    tn = 1024
    out = pl.pallas_call(
        _ada_kernel,
        grid=(depth, n // tn),
        in_specs=[
            pl.BlockSpec((SUBLANES, d), lambda l, j: (0, 0)),
            pl.BlockSpec((None, d, tn), lambda l, j: (l, 0, j)),
            pl.BlockSpec((None, 1, tn), lambda l, j: (l, 0, j)),
        ],
        out_specs=pl.BlockSpec((None, SUBLANES, tn), lambda l, j: (l, 0, j)),
        out_shape=jax.ShapeDtypeStruct((depth, SUBLANES, n), F32),
        compiler_params=_params("arbitrary", "arbitrary"),
        name="ada_mod",
    )(c_pad, ada_w, ada_b.reshape(depth, 1, n))
    return out[:, :bsz].reshape(depth, bsz, 6, d)


def _stream_scan(a, u, carry):
    hl = [_vreg(u, 0)]
    al = [_vreg(a, 0)]
    for j in range(1, TILE_VREGS):
        hl.append(_vreg(a, j) * hl[-1] + _vreg(u, j))
        al.append(_vreg(a, j) * al[-1])
    sub = _sublane((SUBLANES, 1))
    ea, eh = al[-1], hl[-1]
    for k in (1, 2, 4):
        keep = sub >= k
        ea_sh = jnp.where(keep, pltpu.roll(ea, k, 0), 1.0)
        eh_sh = jnp.where(keep, pltpu.roll(eh, k, 0), 0.0)
        eh = ea * eh_sh + eh
        ea = ea * ea_sh
    after = ea * carry + eh
    before = jnp.where(sub == 0, carry, pltpu.roll(after, 1, 0))
    h = jnp.concatenate([hl[j] + al[j] * before for j in range(TILE_VREGS)], axis=0)
    return h, after


def _rglru_kernel(natural_in, x_ref, mod_ref, ng_ref, win_ref, cw_ref, vec_ref, wg_ref, wout_ref, o_ref,
                  tail_ref, hc_ref):
    ts = ROW_TILE
    d = o_ref.shape[1]
    n_tiles = o_ref.shape[0] // ts
    blk = d // RG_BLOCKS
    n_tail = RG_CONV - 1

    @pl.when(pl.program_id(1) == 0)
    def _():
        tail_ref[...] = jnp.zeros_like(tail_ref)
        hc_ref[...] = jnp.zeros_like(hc_ref)

    def prologue(t):
        x = _load_tile_order(x_ref.at[t]) if natural_in else x_ref[t * ts:(t + 1) * ts, :]
        return x, (_rms(x, ng_ref[0:1, :]) * (1.0 + mod_ref[1:2, :]) + mod_ref[0:1, :]).astype(BF16)

    def epilogue(t, x, acc):
        o_ref[t * ts:(t + 1) * ts, :] = x + mod_ref[2:3, :] * _rms(acc, ng_ref[1:2, :])

    def project(t, g):
        return (_dot(hs[t], win_ref[:, g * blk:(g + 1) * blk]),
                _dot(hs[t], win_ref[:, d + g * blk:d + (g + 1) * blk]))

    def mix(g, gate_br, xb):
        cols = slice(g * blk, (g + 1) * blk)
        tail = tail_ref[:, cols]
        tail_ref[:, cols] = xb[ts - n_tail * SUBLANES:, :]
        xc = vec_ref[0:1, cols] + _delay(xb, tail, 3) * cw_ref[0:1, cols]
        xc = xc + _delay(xb, tail, 2) * cw_ref[1:2, cols]
        xc = xc + _delay(xb, tail, 1) * cw_ref[2:3, cols]
        xc = xc + xb * cw_ref[3:4, cols]
        p = _dot(xc.astype(BF16), wg_ref[g])
        r = jax.nn.sigmoid(p[:, :blk] + vec_ref[1:2, cols])
        i_g = jax.nn.sigmoid(p[:, blk:] + vec_ref[2:3, cols])
        nl = -vec_ref[3:4, cols]
        softplus_nl = jnp.maximum(nl, 0.0) + jnp.log1p(jnp.exp(-jnp.abs(nl)))
        log_a = r * ((-RG_C) * softplus_nl)
        a = jnp.exp(log_a)
        m2 = jnp.tanh(log_a) * (-1.0 - a * a)
        u = jnp.where(m2 > 0.0, m2 * lax.rsqrt(m2), 0.0) * (i_g * xc)
        hs_g, after = _stream_scan(a, u, hc_ref[SUBLANES - 1:SUBLANES, cols])
        hc_ref[:, cols] = after
        return _gelu_times(gate_br, hs_g).astype(BF16)

    items = [(t, g) for t in range(n_tiles) for g in range(RG_BLOCKS)]
    xs, hs, accs = {}, {}, {}
    xs[0], hs[0] = prologue(0)

    def out_project(t, g, y):
        part = _dot(y, wout_ref[g * blk:(g + 1) * blk, :])
        accs[t] = part if g == 0 else accs[t] + part
        if g == RG_BLOCKS - 1:
            epilogue(t, xs.pop(t), accs.pop(t))

    cur = project(0, 0)
    prev = None
    for idx, (t, g) in enumerate(items):
        if g == 1 and t + 1 < n_tiles:
            xs[t + 1], hs[t + 1] = prologue(t + 1)
        nxt = project(*items[idx + 1]) if idx + 1 < len(items) else None
        if prev is not None:
            out_project(*prev)
        prev = (t, g, mix(g, *cur))
        cur = nxt
    out_project(*prev)


def _rglru_layer(x, mod, ng, w_in, conv_w, vec, w_gate, w_out, natural_in):
    bsz, s, d = x.shape
    ts = ROW_TILE
    n_tiles = TILES_PER_STEP
    tile = pl.BlockSpec((None, n_tiles * ts, d), lambda b, i: (b, i, 0))
    x_tile = tile
    if natural_in:
        x = x.reshape(bsz, s // ts, SUBLANES, TILE_VREGS, d)
        x_tile = pl.BlockSpec((None, n_tiles, SUBLANES, TILE_VREGS, d), lambda b, i: (b, i, 0, 0, 0))
    return pl.pallas_call(
        functools.partial(_rglru_kernel, natural_in),
        grid=(bsz, s // (n_tiles * ts)),
        in_specs=[
            x_tile,
            pl.BlockSpec((None, 6, d), lambda b, i: (b, 0, 0)),
            _const_spec(ng.shape),
            _const_spec(w_in.shape),
            _const_spec(conv_w.shape),
            _const_spec(vec.shape),
            _const_spec(w_gate.shape),
            _const_spec(w_out.shape),
        ],
        out_specs=tile,
        out_shape=jax.ShapeDtypeStruct((bsz, s, d), F32),
        scratch_shapes=[
            pltpu.VMEM(((RG_CONV - 1) * SUBLANES, d), F32),
            pltpu.VMEM((SUBLANES, d), F32),
        ],
        compiler_params=_params("arbitrary", "arbitrary"),
        name="rglru_layer",
    )(x, mod, ng, w_in, conv_w, vec, w_gate, w_out)


def _ffn_kernel(natural_out, x_ref, mod_ref, ng_ref, wup_ref, cw_ref, cb_ref, wdn_ref, o_ref, tail_ref):
    ts = ROW_TILE
    n_tiles = x_ref.shape[0] // ts
    f = wdn_ref.shape[0]
    fc = FFN_COL_CHUNK
    n_chunks = f // fc
    n_tail = FFN_CONV - 1

    @pl.when(pl.program_id(1) == 0)
    def _():
        tail_ref[...] = jnp.zeros_like(tail_ref)

    def prologue(t):
        x = x_ref[t * ts:(t + 1) * ts, :]
        return x, (_rms(x, ng_ref[2:3, :]) * (1.0 + mod_ref[4:5, :]) + mod_ref[3:4, :]).astype(BF16)

    def epilogue(t, x, acc):
        out = x + mod_ref[5:6, :] * _rms(acc, ng_ref[3:4, :])
        if natural_out:
            _store_natural_order(o_ref.at[t], out)
        else:
            o_ref[t * ts:(t + 1) * ts, :] = out

    def conv_cols(up, c0):
        cols = slice(c0, c0 + fc)
        tail = tail_ref[:, cols]
        tail_ref[:, cols] = up[ts - n_tail * SUBLANES:, :]
        y = cb_ref[:, cols] + _delay(up, tail, 2) * cw_ref[0:1, cols]
        y = y + _delay(up, tail, 1) * cw_ref[1:2, cols]
        return y + up * cw_ref[2:3, cols]

    items = [(t, c) for t in range(n_tiles) for c in range(n_chunks)]
    xs, hs, accs = {}, {}, {}
    xs[0], hs[0] = prologue(0)

    def up_pair(t, c):
        return (_dot(hs[t], wup_ref[:, c * fc:(c + 1) * fc]), _dot(hs[t], wup_ref[:, f + c * fc:f + (c + 1) * fc]))

    def down(t, c, act):
        part = _dot(act, wdn_ref[c * fc:(c + 1) * fc, :])
        accs[t] = part if c == 0 else accs[t] + part
        if c == n_chunks - 1:
            epilogue(t, xs.pop(t), accs.pop(t))

    ups = up_pair(0, 0)
    prev = None
    for idx, (t, c) in enumerate(items):
        if c == n_chunks - 4 and t + 1 < n_tiles:
            xs[t + 1], hs[t + 1] = prologue(t + 1)
        nxt = up_pair(*items[idx + 1]) if idx + 1 < len(items) else None
        if prev is not None:
            down(*prev)
        g = conv_cols(ups[0], c * fc)
        val = conv_cols(ups[1], f + c * fc)
        prev = (t, c, _gelu_times(g, val).astype(BF16))
        ups = nxt
    down(*prev)


def _ffn_layer(x, mod, ng, w_up, conv_w, conv_b, w_down, layer, natural_out):
    bsz, s, d = x.shape
    ts = ROW_TILE
    n_tiles = TILES_PER_STEP
    rows = pl.BlockSpec((None, n_tiles * ts, d), lambda b, i: (b, i, 0))
    out_tile, out_shape = rows, (bsz, s, d)
    if natural_out:
        out_tile = pl.BlockSpec((None, n_tiles, SUBLANES, TILE_VREGS, d), lambda b, i: (b, i, 0, 0, 0))
        out_shape = (bsz, s // ts, SUBLANES, TILE_VREGS, d)
    out = pl.pallas_call(
        functools.partial(_ffn_kernel, natural_out),
        grid=(bsz, s // (n_tiles * ts)),
        in_specs=[
            rows,
            pl.BlockSpec((None, None, 6, d), lambda b, i: (layer, b, 0, 0)),
            _layer_spec(ng.shape, layer),
            _layer_spec(w_up.shape, layer),
            _layer_spec(conv_w.shape, layer),
            _layer_spec(conv_b.shape, layer),
            _layer_spec(w_down.shape, layer),
        ],
        out_specs=out_tile,
        out_shape=jax.ShapeDtypeStruct(out_shape, F32),
        scratch_shapes=[
            pltpu.VMEM(((FFN_CONV - 1) * SUBLANES, w_up.shape[2]), F32),
        ],
        compiler_params=_params("arbitrary", "arbitrary"),
        name="ffn_layer",
    )(x, mod, ng, w_up, conv_w, conv_b, w_down)
    return out.reshape(bsz, s, d)


_STREAM_LEVELS = tuple(TILE_VREGS << i for i in (2, 1, 0))
_VREG_LEVELS = tuple(TILE_VREGS >> i for i in range(1, 4))
_DIRECT_BLOCK = 4
_MASKED_STREAM_LEVELS = _STREAM_LEVELS[1:]
_EXCLUDE = 1e30
_GLA_PROJ_PIECES = 3
_GLA_OUT_PIECES = 2


def _tile_order_times(c):
    p = np.arange(c)
    return (p % SUBLANES) * (c // SUBLANES) + p // SUBLANES


def _gla_stream_masks(c):
    t = _tile_order_times(c)
    return np.stack([(t[:, None] // (2 * b) == t[None, :] // (2 * b)) for b in _MASKED_STREAM_LEVELS]
                    ).astype(np.float32)


def _vreg_level_rows(b):
    upper = [j for j in range(TILE_VREGS) if (j // b) % 2 == 1]
    lower = [j for j in range(TILE_VREGS) if (j // b) % 2 == 0]
    return upper, lower


def _gla_vreg_masks():
    out = []
    s = np.arange(SUBLANES)
    for b in _VREG_LEVELS:
        upper, lower = _vreg_level_rows(b)
        ju = np.repeat(np.array(upper), SUBLANES)[:, None]
        jl = np.repeat(np.array(lower), SUBLANES)[None, :]
        su = np.tile(s, len(upper))[:, None]
        sl = np.tile(s, len(lower))[None, :]
        out.append((su == sl) & (ju // (2 * b) == jl // (2 * b)))
    return np.stack(out).astype(np.float32)


def _cumsum_tile_order(g):
    rows = [_vreg(g, 0)]
    for j in range(1, TILE_VREGS):
        rows.append(rows[-1] + _vreg(g, j))
    total = rows[-1]
    sub = _sublane((SUBLANES, 1))
    incl = total
    for k in (1, 2, 4):
        incl = incl + jnp.where(sub >= k, pltpu.roll(incl, k, 0), 0.0)
    before = incl - total
    return jnp.concatenate([r + before for r in rows], axis=0)


def _gla_scores(qk, v32, gc, state, fill):
    c, dk = gc.shape
    vreg = _vreg
    q = qk[:, :dk] * (dk ** -0.5)
    k = qk[:, dk:]
    v = v32.astype(BF16)
    g_last = gc[c - 1:c, :]

    fill()
    o_inter = _dot((q * jnp.exp(gc)).astype(BF16), state.astype(BF16))

    sub = _sublane((SUBLANES, 1))
    g_end = vreg(gc, TILE_VREGS - 1)
    tile_rows = lambda a: jnp.concatenate([a] * TILE_VREGS, axis=0)
    p_stream = []
    for b in _STREAM_LEVELS:
        w = b // TILE_VREGS
        g_mid = None
        for first in range(0, SUBLANES, 2 * w):
            row = jnp.broadcast_to(g_end[first + w - 1:first + w, :], (SUBLANES, dk))
            g_mid = row if g_mid is None else jnp.where(sub >= first, row, g_mid)
        upper = (lax.shift_right_logical(sub, w.bit_length() - 1) & 1) == 1
        g_q = tile_rows(jnp.where(upper, g_mid, _EXCLUDE))
        g_k = tile_rows(jnp.where(upper, -_EXCLUDE, g_mid))
        q_b = (q * jnp.exp(gc - g_q)).astype(BF16)
        k_b = (k * jnp.exp(g_k - gc)).astype(BF16)
        p_stream.append(_dot_nt(q_b, k_b))
    fill()

    p_vreg = []
    for b in _VREG_LEVELS:
        upper_rows, _ = _vreg_level_rows(b)
        q_rows, k_rows = [], []
        for j in range(TILE_VREGS):
            jm = (j // (2 * b)) * (2 * b) + b - 1
            if j in upper_rows:
                q_rows.append(vreg(q, j) * jnp.exp(vreg(gc, j) - vreg(gc, jm)))
            elif j == jm:
                k_rows.append(vreg(k, j))
            else:
                k_rows.append(vreg(k, j) * jnp.exp(vreg(gc, jm) - vreg(gc, j)))
        p_vreg.append(_dot_nt(jnp.concatenate(q_rows, axis=0).astype(BF16),
                              jnp.concatenate(k_rows, axis=0).astype(BF16)))
    fill()

    k_rev = (k * jnp.exp(g_last - gc)).astype(BF16)
    upd = lax.dot_general(k_rev, v, (((0,), (0,)), ((), ())), preferred_element_type=F32)
    decay_rows = jnp.broadcast_to(jnp.exp(g_last), (dk, dk))
    decay_col = jnp.transpose(decay_rows)[:, 0:1]
    new_state = decay_col * state + upd

    direct = []
    for ju in range(TILE_VREGS):
        terms = vreg(v32, ju) * jnp.sum(vreg(q, ju) * vreg(k, ju), axis=-1, keepdims=True)
        for jl in range((ju // _DIRECT_BLOCK) * _DIRECT_BLOCK, ju):
            w = vreg(q, ju) * vreg(k, jl) * jnp.exp(vreg(gc, ju) - vreg(gc, jl))
            terms = terms + vreg(v32, jl) * jnp.sum(w, axis=-1, keepdims=True)
        direct.append(terms)
    return (o_inter, p_stream, p_vreg, direct), new_state


def _gla_combine(scores, v32, r, hn, sm_ref, vm_ref, fill):
    o, p_stream, p_vreg, direct = scores
    v = v32.astype(BF16)

    attn = None
    for b, p in zip(_STREAM_LEVELS, p_stream):
        if b in _MASKED_STREAM_LEVELS:
            p = p * sm_ref[_MASKED_STREAM_LEVELS.index(b)]
        attn = p if attn is None else attn + p
    o = o + _dot(attn.astype(BF16), v)
    fill()

    o_rows = [_vreg(o, j) + direct[j] for j in range(TILE_VREGS)]
    for li, (b, p) in enumerate(zip(_VREG_LEVELS, p_vreg)):
        upper_rows, _ = _vreg_level_rows(b)
        p = (p * vm_ref[li]).astype(BF16)
        v_low = jnp.concatenate(
            [v[j0 * SUBLANES:(j0 + b) * SUBLANES] for j0 in range(0, TILE_VREGS, 2 * b)], axis=0)
        o_up = _dot(p, v_low)
        if li == 1:
            fill()
        for idx, j in enumerate(upper_rows):
            o_rows[j] = o_rows[j] + _vreg(o_up, idx)
    o = jnp.concatenate(o_rows, axis=0)
    return (_rms(o, hn) * (r * jax.nn.sigmoid(r))).astype(BF16)


def _gla_kernel(x_ref, mod_ref, ng_ref, wh_ref, wz_ref, wal_ref, bal_ref, hn_ref, sm_ref, vm_ref, wout_ref,
                o_ref, state_ref):
    c = ROW_TILE
    d = x_ref.shape[1]
    n_tiles = x_ref.shape[0] // c
    dk, dv = state_ref.shape[1:]

    @pl.when(pl.program_id(1) == 0)
    def _():
        state_ref[...] = jnp.zeros_like(state_ref)

    def prologue(t):
        x = x_ref[t * c:(t + 1) * c, :]
        h = (_rms(x, ng_ref[0:1, :]) * (1.0 + mod_ref[1:2, :]) + mod_ref[0:1, :]).astype(BF16)
        z = _dot(h, wz_ref[...])
        ga = _dot(z.astype(BF16), wal_ref[...]) + bal_ref[...]
        g = (jnp.minimum(ga, 0.0) - jnp.log1p(jnp.exp(-jnp.abs(ga)))) / GLA_TAU
        return x, h, _cumsum_tile_order(g)

    def epilogue(t, x, acc):
        o_ref[t * c:(t + 1) * c, :] = x + mod_ref[2:3, :] * _rms(acc, ng_ref[1:2, :])

    items = [(t, hh) for t in range(n_tiles) for hh in range(GLA_HEADS)]
    xs, hs, gcs, accs = {}, {}, {}, {}
    xs[0], hs[0], gcs[0] = prologue(0)

    def project_piece(t, hh, j):
        w = 2 * dk if j == 0 else dv
        c0 = 0 if j == 0 else 2 * dk + (j - 1) * dv
        return _dot(hs[t], wh_ref[hh, :, c0:c0 + w])

    def out_piece(t, hh, y, j):
        w = d // _GLA_OUT_PIECES
        part = _dot(y, wout_ref[hh * dv:(hh + 1) * dv, j * w:(j + 1) * w])
        accs[t, j] = part if hh == 0 else accs[t, j] + part
        if hh == GLA_HEADS - 1 and j == _GLA_OUT_PIECES - 1:
            acc = jnp.concatenate([accs.pop((t, jj)) for jj in range(_GLA_OUT_PIECES)], axis=1)
            epilogue(t, xs.pop(t), acc)

    cur = [project_piece(0, 0, j) for j in range(_GLA_PROJ_PIECES)]
    prev = None
    for idx, (t, hh) in enumerate(items):
        if hh == 1 and t + 1 < n_tiles:
            xs[t + 1], hs[t + 1], gcs[t + 1] = prologue(t + 1)
        nxt, pending = [], []
        if idx + 1 < len(items):
            pending += [lambda j=j, idx=idx, nxt=nxt: nxt.append(project_piece(*items[idx + 1], j))
                        for j in range(_GLA_PROJ_PIECES)]
        if prev is not None:
            pending += [functools.partial(out_piece, *prev, j) for j in range(_GLA_OUT_PIECES)]
        pending.reverse()
        fill = lambda pending=pending: pending.pop()() if pending else None
        qk, v32, r = cur
        scores, state_ref[hh] = _gla_scores(qk, v32, gcs[t][:, hh * dk:(hh + 1) * dk], state_ref[hh], fill)
        prev = (t, hh, _gla_combine(scores, v32, r, hn_ref[...], sm_ref, vm_ref, fill))
        while pending:
            fill()
        cur = nxt
    for j in range(_GLA_OUT_PIECES):
        out_piece(*prev, j)


def _gla_layer(x, mod, ng, w_heads, w_z, w_alpha, b_alpha, head_norm, w_out):
    bsz, s, d = x.shape
    c = ROW_TILE
    dk = w_alpha.shape[1] // GLA_HEADS
    dv = d // GLA_HEADS
    stream_masks = jnp.asarray(_gla_stream_masks(c), F32)
    vreg_masks = jnp.asarray(_gla_vreg_masks(), F32)
    n_tiles = TILES_PER_STEP
    tile = pl.BlockSpec((None, n_tiles * c, d), lambda b, i: (b, i, 0))
    return pl.pallas_call(
        _gla_kernel,
        grid=(bsz, s // (n_tiles * c)),
        in_specs=[
            tile,
            pl.BlockSpec((None, 6, d), lambda b, i: (b, 0, 0)),
            _const_spec(ng.shape),
            _const_spec(w_heads.shape),
            _const_spec(w_z.shape),
            _const_spec(w_alpha.shape),
            _const_spec(b_alpha.shape),
            _const_spec(head_norm.shape),
            _const_spec(stream_masks.shape),
            _const_spec(vreg_masks.shape),
            _const_spec(w_out.shape),
        ],
        out_specs=tile,
        out_shape=jax.ShapeDtypeStruct(x.shape, F32),
        scratch_shapes=[pltpu.VMEM((GLA_HEADS, dk, dv), F32)],
        compiler_params=_params("arbitrary", "arbitrary"),
        name="gla_layer",
    )(x, mod, ng, w_heads, w_z, w_alpha, b_alpha, head_norm, stream_masks, vreg_masks, w_out)


def kernel(x, c, ada_w, ada_b, norm_g, ffn_w_up, ffn_conv_w, ffn_conv_b, ffn_w_down, rg_w_in, rg_conv_w,
           rg_conv_b, rg_wa, rg_ba, rg_wx, rg_bx, rg_lambda, rg_w_out, gla_w_in, gla_w_alpha, gla_b_alpha,
           gla_norm_g, gla_w_out):
    depth = ada_w.shape[0]
    d = x.shape[2]
    assert depth >= 1 and x.shape[1] % (ROW_TILE * TILES_PER_STEP) == 0
    mod = _ada_modulation(c, ada_w, ada_b)
    ffn_w_up_bf = ffn_w_up.astype(BF16)
    ffn_w_down_bf = ffn_w_down.astype(BF16)
    ffn_conv_b3 = ffn_conv_b[:, None, :]
    for i in range(depth):
        j = i // 2
        if i % 2 == 0:
            vec = jnp.stack([rg_conv_b[j], rg_ba[j], rg_bx[j], rg_lambda[j]])
            w_gate = jnp.concatenate([rg_wa[j], rg_wx[j]], axis=-1).astype(BF16)
            x = _rglru_layer(x, mod[i], norm_g[i], rg_w_in[j].astype(BF16), rg_conv_w[j], vec, w_gate,
                             rg_w_out[j].astype(BF16), natural_in=(i == 0))
        else:
            qk = gla_w_alpha.shape[2]
            n_main = 2 * qk + 2 * d
            rank = gla_w_in.shape[2] - n_main
            w_in = gla_w_in[j].astype(BF16)
            dk, dv = qk // GLA_HEADS, d // GLA_HEADS
            w_heads = jnp.stack([
                jnp.concatenate([w_in[:, hh * dk:(hh + 1) * dk],
                                 w_in[:, qk + hh * dk:qk + (hh + 1) * dk],
                                 w_in[:, 2 * qk + hh * dv:2 * qk + (hh + 1) * dv],
                                 w_in[:, 2 * qk + d + hh * dv:2 * qk + d + (hh + 1) * dv]], axis=1)
                for hh in range(GLA_HEADS)])
            w_z = jnp.zeros((d, GLA_RANK_PAD), BF16).at[:, :rank].set(w_in[:, n_main:])
            w_alpha = jnp.zeros((GLA_RANK_PAD, qk), BF16).at[:rank].set(gla_w_alpha[j].astype(BF16))
            x = _gla_layer(x, mod[i], norm_g[i], w_heads, w_z, w_alpha, gla_b_alpha[j][None, :],
                           gla_norm_g[j][None, :], gla_w_out[j].astype(BF16))
        x = _ffn_layer(x, mod, norm_g, ffn_w_up_bf, ffn_conv_w, ffn_conv_b3, ffn_w_down_bf, i,
                       natural_out=(i == depth - 1))
    return x
```

```python
import functools
import math

import jax
import jax.numpy as jnp
import numpy as np
from jax import lax
from jax.experimental import pallas as pl
from jax.experimental.pallas import tpu as pltpu

F32 = jnp.float32
BF16 = jnp.bfloat16

EPS = 1e-6
RG_BLOCKS = 4
RG_C = 8.0
RG_CONV = 4
FFN_CONV = 3
GLA_HEADS = 4
GLA_TAU = 16.0
GLA_RANK_PAD = 128
SUBLANES = 8

ROW_TILE = 256
TILE_VREGS = ROW_TILE // SUBLANES
TILES_PER_STEP = 4
ADA_COL_TILE = 1024
FFN_COL_CHUNK = 256
VMEM_LIMIT = 56 * 1024 * 1024

_GELU_C0 = 2.0 * math.sqrt(2.0 / math.pi) * math.log2(math.e)
_GELU_C1 = _GELU_C0 * 0.044715


def _dot(a, b):
    return jnp.dot(a, b, preferred_element_type=F32)


def _dot_nt(a, b):
    return lax.dot_general(a, b, (((1,), (1,)), ((), ())), preferred_element_type=F32)


def _rms(x, g):
    return x * lax.rsqrt(jnp.mean(x * x, axis=-1, keepdims=True) + EPS) * g


def _gelu_times(x, v):
    neg_2z_log2e = x * (-_GELU_C0 - _GELU_C1 * (x * x))
    return (x * v) / (1.0 + jnp.exp2(neg_2z_log2e))


def _sublane(shape):
    return lax.broadcasted_iota(jnp.int32, shape, 0) % SUBLANES


def _vreg(a, j):
    return a[j * SUBLANES:(j + 1) * SUBLANES]


def _load_tile_order(ref):
    return jnp.concatenate([ref[:, j, :] for j in range(TILE_VREGS)], axis=0)


def _store_natural_order(ref, val):
    for j in range(TILE_VREGS):
        ref[:, j, :] = _vreg(val, j)


def _delay(x, tail, k):
    ts = x.shape[0]
    n_tail = tail.shape[0] // SUBLANES
    last = _sublane((SUBLANES, 1)) == SUBLANES - 1
    head = []
    for j in range(k):
        cur = _vreg(x, TILE_VREGS - k + j)
        prv = _vreg(tail, n_tail - k + j)
        head.append(pltpu.roll(jnp.where(last, prv, cur), 1, 0))
    return jnp.concatenate(head + [x[:ts - k * SUBLANES]], axis=0)


def _params(*sem):
    return pltpu.CompilerParams(dimension_semantics=sem, vmem_limit_bytes=VMEM_LIMIT)


def _const_spec(shape):
    n = len(shape)
    return pl.BlockSpec(shape, lambda *_: (0,) * n, pipeline_mode=pl.Buffered(1))


def _layer_spec(shape, layer):
    n = len(shape) - 1
    return pl.BlockSpec((None,) + tuple(shape[1:]), lambda *_: (layer,) + (0,) * n,
                        pipeline_mode=pl.Buffered(1))


def _ada_kernel(c_ref, w_ref, b_ref, o_ref):
    c = c_ref[...]
    c_act = (c * jax.nn.sigmoid(c)).astype(BF16)
    o_ref[...] = _dot(c_act, w_ref[...].astype(BF16)) + b_ref[...]


def _ada_modulation(c, ada_w, ada_b):
    depth, d, n = ada_w.shape
    bsz = c.shape[0]
    tn = ADA_COL_TILE
    out = pl.pallas_call(
        _ada_kernel,
        grid=(depth, n // tn),
        in_specs=[
            pl.BlockSpec((bsz, d), lambda l, j: (0, 0)),
            pl.BlockSpec((None, d, tn), lambda l, j: (l, 0, j)),
            pl.BlockSpec((None, 1, tn), lambda l, j: (l, 0, j)),
        ],
        out_specs=pl.BlockSpec((None, bsz, tn), lambda l, j: (l, 0, j)),
        out_shape=jax.ShapeDtypeStruct((depth, bsz, n), F32),
        compiler_params=_params("arbitrary", "arbitrary"),
        name="ada_mod",
    )(c, ada_w, ada_b.reshape(depth, 1, n))
    return out.reshape(depth, bsz, 6, d)


def _stream_scan(a, u, carry):
    hl = [_vreg(u, 0)]
    al = [_vreg(a, 0)]
    for j in range(1, TILE_VREGS):
        hl.append(_vreg(a, j) * hl[-1] + _vreg(u, j))
        al.append(_vreg(a, j) * al[-1])
    sub = _sublane((SUBLANES, 1))
    ea, eh = al[-1], hl[-1]
    for k in (1, 2, 4):
        keep = sub >= k
        ea_sh = jnp.where(keep, pltpu.roll(ea, k, 0), 1.0)
        eh_sh = jnp.where(keep, pltpu.roll(eh, k, 0), 0.0)
        eh = ea * eh_sh + eh
        ea = ea * ea_sh
    after = ea * carry + eh
    before = jnp.where(sub == 0, carry, pltpu.roll(after, 1, 0))
    h = jnp.concatenate([hl[j] + al[j] * before for j in range(TILE_VREGS)], axis=0)
    return h, after


def _first_step():
    return (pl.program_id(0) == 0) & (pl.program_id(1) == 0)


def _rglru_kernel(natural_in, x_ref, mod_ref, ng_ref, win32_ref, cw_ref, cb_ref, ba_ref, bx_ref, lam_ref,
                  wa32_ref, wx32_ref, wout32_ref, o_ref, win_ref, wg_ref, wout_ref, tail_ref, hc_ref):
    ts = ROW_TILE
    d = o_ref.shape[1]
    n_tiles = o_ref.shape[0] // ts
    blk = d // RG_BLOCKS
    n_tail = RG_CONV - 1

    @pl.when(_first_step())
    def _():
        win_ref[...] = win32_ref[...].astype(BF16)
        wout_ref[...] = wout32_ref[...].astype(BF16)
        for g in range(RG_BLOCKS):
            wg_ref[g, :, :blk] = wa32_ref[g].astype(BF16)
            wg_ref[g, :, blk:] = wx32_ref[g].astype(BF16)

    @pl.when(pl.program_id(1) == 0)
    def _():
        tail_ref[...] = jnp.zeros_like(tail_ref)
        hc_ref[...] = jnp.zeros_like(hc_ref)

    def prologue(t):
        x = _load_tile_order(x_ref.at[t]) if natural_in else x_ref[t * ts:(t + 1) * ts, :]
        return x, (_rms(x, ng_ref[0:1, :]) * (1.0 + mod_ref[1:2, :]) + mod_ref[0:1, :]).astype(BF16)

    def epilogue(t, x, acc):
        o_ref[t * ts:(t + 1) * ts, :] = x + mod_ref[2:3, :] * _rms(acc, ng_ref[1:2, :])

    def project(t, g):
        return (_dot(hs[t], win_ref[:, g * blk:(g + 1) * blk]),
                _dot(hs[t], win_ref[:, d + g * blk:d + (g + 1) * blk]))

    def mix(g, gate_br, xb):
        cols = slice(g * blk, (g + 1) * blk)
        tail = tail_ref[:, cols]
        tail_ref[:, cols] = xb[ts - n_tail * SUBLANES:, :]
        xc = cb_ref[:, cols] + _delay(xb, tail, 3) * cw_ref[0:1, cols]
        xc = xc + _delay(xb, tail, 2) * cw_ref[1:2, cols]
        xc = xc + _delay(xb, tail, 1) * cw_ref[2:3, cols]
        xc = xc + xb * cw_ref[3:4, cols]
        p = _dot(xc.astype(BF16), wg_ref[g])
        r = jax.nn.sigmoid(p[:, :blk] + ba_ref[:, cols])
        i_g = jax.nn.sigmoid(p[:, blk:] + bx_ref[:, cols])
        nl = -lam_ref[:, cols]
        softplus_nl = jnp.maximum(nl, 0.0) + jnp.log1p(jnp.exp(-jnp.abs(nl)))
        log_a = r * ((-RG_C) * softplus_nl)
        a = jnp.exp(log_a)
        m2 = jnp.tanh(log_a) * (-1.0 - a * a)
        u = jnp.where(m2 > 0.0, m2 * lax.rsqrt(m2), 0.0) * (i_g * xc)
        hs_g, after = _stream_scan(a, u, hc_ref[SUBLANES - 1:SUBLANES, cols])
        hc_ref[:, cols] = after
        return _gelu_times(gate_br, hs_g).astype(BF16)

    items = [(t, g) for t in range(n_tiles) for g in range(RG_BLOCKS)]
    xs, hs, accs = {}, {}, {}
    xs[0], hs[0] = prologue(0)

    def out_project(t, g, y):
        part = _dot(y, wout_ref[g * blk:(g + 1) * blk, :])
        accs[t] = part if g == 0 else accs[t] + part
        if g == RG_BLOCKS - 1:
            epilogue(t, xs.pop(t), accs.pop(t))

    cur = project(0, 0)
    prev = None
    for idx, (t, g) in enumerate(items):
        if g == 1 and t + 1 < n_tiles:
            xs[t + 1], hs[t + 1] = prologue(t + 1)
        nxt = project(*items[idx + 1]) if idx + 1 < len(items) else None
        if prev is not None:
            out_project(*prev)
        prev = (t, g, mix(g, *cur))
        cur = nxt
    out_project(*prev)


def _rglru_layer(x, mod, ng, params, layer, mixer_index, natural_in):
    w_in, conv_w, conv_b, wa, ba, wx, bx, lam, w_out = params
    bsz, s, d = x.shape
    ts = ROW_TILE
    n_tiles = TILES_PER_STEP
    blk = d // RG_BLOCKS
    tile = pl.BlockSpec((None, n_tiles * ts, d), lambda b, i: (b, i, 0))
    x_tile = tile
    if natural_in:
        x = x.reshape(bsz, s // ts, SUBLANES, TILE_VREGS, d)
        x_tile = pl.BlockSpec((None, n_tiles, SUBLANES, TILE_VREGS, d), lambda b, i: (b, i, 0, 0, 0))
    row = lambda v: v[:, None, :]
    mix_spec = lambda p: _layer_spec(p.shape, mixer_index)
    return pl.pallas_call(
        functools.partial(_rglru_kernel, natural_in),
        grid=(bsz, s // (n_tiles * ts)),
        in_specs=[
            x_tile,
            pl.BlockSpec((None, None, 6, d), lambda b, i: (layer, b, 0, 0)),
            _layer_spec(ng.shape, layer),
            mix_spec(w_in), mix_spec(conv_w), mix_spec(row(conv_b)), mix_spec(row(ba)), mix_spec(row(bx)),
            mix_spec(row(lam)), mix_spec(wa), mix_spec(wx), mix_spec(w_out),
        ],
        out_specs=tile,
        out_shape=jax.ShapeDtypeStruct((bsz, s, d), F32),
        scratch_shapes=[
            pltpu.VMEM(w_in.shape[1:], BF16),
            pltpu.VMEM((RG_BLOCKS, blk, 2 * blk), BF16),
            pltpu.VMEM(w_out.shape[1:], BF16),
            pltpu.VMEM(((RG_CONV - 1) * SUBLANES, d), F32),
            pltpu.VMEM((SUBLANES, d), F32),
        ],
        compiler_params=_params("arbitrary", "arbitrary"),
        name="rglru_layer",
    )(x, mod, ng, w_in, conv_w, row(conv_b), row(ba), row(bx), row(lam), wa, wx, w_out)


def _ffn_kernel(natural_out, x_ref, mod_ref, ng_ref, wup_ref, cw_ref, cb_ref, wdn32_ref, o_ref, wdn_ref,
                tail_ref):
    ts = ROW_TILE
    n_tiles = x_ref.shape[0] // ts
    f = wdn_ref.shape[0]
    fc = FFN_COL_CHUNK
    n_chunks = f // fc
    n_tail = FFN_CONV - 1

    @pl.when(_first_step())
    def _():
        wdn_ref[...] = wdn32_ref[...].astype(BF16)

    @pl.when(pl.program_id(1) == 0)
    def _():
        tail_ref[...] = jnp.zeros_like(tail_ref)

    def prologue(t):
        x = x_ref[t * ts:(t + 1) * ts, :]
        return x, (_rms(x, ng_ref[2:3, :]) * (1.0 + mod_ref[4:5, :]) + mod_ref[3:4, :]).astype(BF16)

    def epilogue(t, x, acc):
        out = x + mod_ref[5:6, :] * _rms(acc, ng_ref[3:4, :])
        if natural_out:
            _store_natural_order(o_ref.at[t], out)
        else:
            o_ref[t * ts:(t + 1) * ts, :] = out

    def conv_cols(up, c0):
        cols = slice(c0, c0 + fc)
        tail = tail_ref[:, cols]
        tail_ref[:, cols] = up[ts - n_tail * SUBLANES:, :]
        y = cb_ref[:, cols] + _delay(up, tail, 2) * cw_ref[0:1, cols]
        y = y + _delay(up, tail, 1) * cw_ref[1:2, cols]
        return y + up * cw_ref[2:3, cols]

    items = [(t, c) for t in range(n_tiles) for c in range(n_chunks)]
    xs, hs, accs = {}, {}, {}
    xs[0], hs[0] = prologue(0)

    def up_pair(t, c):
        return (_dot(hs[t], wup_ref[:, c * fc:(c + 1) * fc]), _dot(hs[t], wup_ref[:, f + c * fc:f + (c + 1) * fc]))

    def down(t, c, act):
        part = _dot(act, wdn_ref[c * fc:(c + 1) * fc, :])
        accs[t] = part if c == 0 else accs[t] + part
        if c == n_chunks - 1:
            epilogue(t, xs.pop(t), accs.pop(t))

    ups = up_pair(0, 0)
    prev = None
    for idx, (t, c) in enumerate(items):
        if c == n_chunks - 4 and t + 1 < n_tiles:
            xs[t + 1], hs[t + 1] = prologue(t + 1)
        nxt = up_pair(*items[idx + 1]) if idx + 1 < len(items) else None
        if prev is not None:
            down(*prev)
        g = conv_cols(ups[0], c * fc)
        val = conv_cols(ups[1], f + c * fc)
        prev = (t, c, _gelu_times(g, val).astype(BF16))
        ups = nxt
    down(*prev)


def _ffn_layer(x, mod, ng, w_up, conv_w, conv_b, w_down, layer, natural_out):
    bsz, s, d = x.shape
    ts = ROW_TILE
    n_tiles = TILES_PER_STEP
    rows = pl.BlockSpec((None, n_tiles * ts, d), lambda b, i: (b, i, 0))
    out_tile, out_shape = rows, (bsz, s, d)
    if natural_out:
        out_tile = pl.BlockSpec((None, n_tiles, SUBLANES, TILE_VREGS, d), lambda b, i: (b, i, 0, 0, 0))
        out_shape = (bsz, s // ts, SUBLANES, TILE_VREGS, d)
    out = pl.pallas_call(
        functools.partial(_ffn_kernel, natural_out),
        grid=(bsz, s // (n_tiles * ts)),
        in_specs=[
            rows,
            pl.BlockSpec((None, None, 6, d), lambda b, i: (layer, b, 0, 0)),
            _layer_spec(ng.shape, layer),
            _layer_spec(w_up.shape, layer),
            _layer_spec(conv_w.shape, layer),
            _layer_spec(conv_b.shape, layer),
            _layer_spec(w_down.shape, layer),
        ],
        out_specs=out_tile,
        out_shape=jax.ShapeDtypeStruct(out_shape, F32),
        scratch_shapes=[
            pltpu.VMEM(w_down.shape[1:], BF16),
            pltpu.VMEM(((FFN_CONV - 1) * SUBLANES, w_up.shape[2]), F32),
        ],
        compiler_params=_params("arbitrary", "arbitrary"),
        name="ffn_layer",
    )(x, mod, ng, w_up, conv_w, conv_b, w_down)
    return out.reshape(bsz, s, d)


_STREAM_LEVELS = tuple(TILE_VREGS << i for i in (2, 1, 0))
_VREG_LEVELS = tuple(TILE_VREGS >> i for i in range(1, 4))
_DIRECT_BLOCK = 4
_MASKED_STREAM_LEVELS = _STREAM_LEVELS[1:]
_EXCLUDE = 1e30
_GLA_PROJ_PIECES = 3
_GLA_OUT_PIECES = 2


def _tile_order_times(c):
    p = np.arange(c)
    return (p % SUBLANES) * (c // SUBLANES) + p // SUBLANES


def _gla_stream_masks(c):
    t = _tile_order_times(c)
    return np.stack([(t[:, None] // (2 * b) == t[None, :] // (2 * b)) for b in _MASKED_STREAM_LEVELS]
                    ).astype(np.float32)


def _vreg_level_rows(b):
    upper = [j for j in range(TILE_VREGS) if (j // b) % 2 == 1]
    lower = [j for j in range(TILE_VREGS) if (j // b) % 2 == 0]
    return upper, lower


def _gla_vreg_masks():
    out = []
    s = np.arange(SUBLANES)
    for b in _VREG_LEVELS:
        upper, lower = _vreg_level_rows(b)
        ju = np.repeat(np.array(upper), SUBLANES)[:, None]
        jl = np.repeat(np.array(lower), SUBLANES)[None, :]
        su = np.tile(s, len(upper))[:, None]
        sl = np.tile(s, len(lower))[None, :]
        out.append((su == sl) & (ju // (2 * b) == jl // (2 * b)))
    return np.stack(out).astype(np.float32)


def _cumsum_tile_order(g):
    rows = [_vreg(g, 0)]
    for j in range(1, TILE_VREGS):
        rows.append(rows[-1] + _vreg(g, j))
    total = rows[-1]
    sub = _sublane((SUBLANES, 1))
    incl = total
    for k in (1, 2, 4):
        incl = incl + jnp.where(sub >= k, pltpu.roll(incl, k, 0), 0.0)
    before = incl - total
    return jnp.concatenate([r + before for r in rows], axis=0)


def _gla_scores(qk, v32, gc, state, fill):
    c, dk = gc.shape
    vreg = _vreg
    q = qk[:, :dk] * (dk ** -0.5)
    k = qk[:, dk:]
    v = v32.astype(BF16)
    g_last = gc[c - 1:c, :]

    fill()
    o_inter = _dot((q * jnp.exp(gc)).astype(BF16), state.astype(BF16))

    sub = _sublane((SUBLANES, 1))
    g_end = vreg(gc, TILE_VREGS - 1)
    tile_rows = lambda a: jnp.concatenate([a] * TILE_VREGS, axis=0)
    p_stream = []
    for b in _STREAM_LEVELS:
        w = b // TILE_VREGS
        g_mid = None
        for first in range(0, SUBLANES, 2 * w):
            row = jnp.broadcast_to(g_end[first + w - 1:first + w, :], (SUBLANES, dk))
            g_mid = row if g_mid is None else jnp.where(sub >= first, row, g_mid)
        upper = (lax.shift_right_logical(sub, w.bit_length() - 1) & 1) == 1
        g_q = tile_rows(jnp.where(upper, g_mid, _EXCLUDE))
        g_k = tile_rows(jnp.where(upper, -_EXCLUDE, g_mid))
        q_b = (q * jnp.exp(gc - g_q)).astype(BF16)
        k_b = (k * jnp.exp(g_k - gc)).astype(BF16)
        p_stream.append(_dot_nt(q_b, k_b))
    fill()

    p_vreg = []
    for b in _VREG_LEVELS:
        upper_rows, _ = _vreg_level_rows(b)
        q_rows, k_rows = [], []
        for j in range(TILE_VREGS):
            jm = (j // (2 * b)) * (2 * b) + b - 1
            if j in upper_rows:
                q_rows.append(vreg(q, j) * jnp.exp(vreg(gc, j) - vreg(gc, jm)))
            elif j == jm:
                k_rows.append(vreg(k, j))
            else:
                k_rows.append(vreg(k, j) * jnp.exp(vreg(gc, jm) - vreg(gc, j)))
        p_vreg.append(_dot_nt(jnp.concatenate(q_rows, axis=0).astype(BF16),
                              jnp.concatenate(k_rows, axis=0).astype(BF16)))
    fill()

    k_rev = (k * jnp.exp(g_last - gc)).astype(BF16)
    upd = lax.dot_general(k_rev, v, (((0,), (0,)), ((), ())), preferred_element_type=F32)
    decay_rows = jnp.broadcast_to(jnp.exp(g_last), (dk, dk))
    decay_col = jnp.transpose(decay_rows)[:, 0:1]
    new_state = decay_col * state + upd

    direct = []
    for ju in range(TILE_VREGS):
        terms = vreg(v32, ju) * jnp.sum(vreg(q, ju) * vreg(k, ju), axis=-1, keepdims=True)
        for jl in range((ju // _DIRECT_BLOCK) * _DIRECT_BLOCK, ju):
            w = vreg(q, ju) * vreg(k, jl) * jnp.exp(vreg(gc, ju) - vreg(gc, jl))
            terms = terms + vreg(v32, jl) * jnp.sum(w, axis=-1, keepdims=True)
        direct.append(terms)
    return (o_inter, p_stream, p_vreg, direct), new_state


def _gla_combine(scores, v32, r, hn, sm_ref, vm_ref, fill):
    o, p_stream, p_vreg, direct = scores
    v = v32.astype(BF16)

    attn = None
    for b, p in zip(_STREAM_LEVELS, p_stream):
        if b in _MASKED_STREAM_LEVELS:
            p = p * sm_ref[_MASKED_STREAM_LEVELS.index(b)]
        attn = p if attn is None else attn + p
    o = o + _dot(attn.astype(BF16), v)
    fill()

    o_rows = [_vreg(o, j) + direct[j] for j in range(TILE_VREGS)]
    for li, (b, p) in enumerate(zip(_VREG_LEVELS, p_vreg)):
        upper_rows, _ = _vreg_level_rows(b)
        p = (p * vm_ref[li]).astype(BF16)
        v_low = jnp.concatenate(
            [v[j0 * SUBLANES:(j0 + b) * SUBLANES] for j0 in range(0, TILE_VREGS, 2 * b)], axis=0)
        o_up = _dot(p, v_low)
        if li == 1:
            fill()
        for idx, j in enumerate(upper_rows):
            o_rows[j] = o_rows[j] + _vreg(o_up, idx)
    o = jnp.concatenate(o_rows, axis=0)
    return (_rms(o, hn) * (r * jax.nn.sigmoid(r))).astype(BF16)


def _gla_kernel(x_ref, mod_ref, ng_ref, win32_ref, wz_ref, wal_ref, bal_ref, hn_ref, sm_ref, vm_ref,
                wout32_ref, o_ref, wh_ref, wout_ref, state_ref):
    c = ROW_TILE
    d = x_ref.shape[1]
    n_tiles = x_ref.shape[0] // c
    dk, dv = state_ref.shape[1:]
    qk = GLA_HEADS * dk

    @pl.when(_first_step())
    def _():
        wout_ref[...] = wout32_ref[...].astype(BF16)
        for hh in range(GLA_HEADS):
            for dst, src, w in ((0, hh * dk, dk), (dk, qk + hh * dk, dk), (2 * dk, 2 * qk + hh * dv, dv),
                                (2 * dk + dv, 2 * qk + d + hh * dv, dv)):
                wh_ref[hh, :, dst:dst + w] = win32_ref[:, src:src + w].astype(BF16)

    @pl.when(pl.program_id(1) == 0)
    def _():
        state_ref[...] = jnp.zeros_like(state_ref)

    def prologue(t):
        x = x_ref[t * c:(t + 1) * c, :]
        h = (_rms(x, ng_ref[0:1, :]) * (1.0 + mod_ref[1:2, :]) + mod_ref[0:1, :]).astype(BF16)
        z = _dot(h, wz_ref[...])
        ga = _dot(z.astype(BF16), wal_ref[...]) + bal_ref[...]
        g = (jnp.minimum(ga, 0.0) - jnp.log1p(jnp.exp(-jnp.abs(ga)))) / GLA_TAU
        return x, h, _cumsum_tile_order(g)

    def epilogue(t, x, acc):
        o_ref[t * c:(t + 1) * c, :] = x + mod_ref[2:3, :] * _rms(acc, ng_ref[1:2, :])

    items = [(t, hh) for t in range(n_tiles) for hh in range(GLA_HEADS)]
    xs, hs, gcs, accs = {}, {}, {}, {}
    xs[0], hs[0], gcs[0] = prologue(0)

    def project_piece(t, hh, j):
        w = 2 * dk if j == 0 else dv
        c0 = 0 if j == 0 else 2 * dk + (j - 1) * dv
        return _dot(hs[t], wh_ref[hh, :, c0:c0 + w])

    def out_piece(t, hh, y, j):
        w = d // _GLA_OUT_PIECES
        part = _dot(y, wout_ref[hh * dv:(hh + 1) * dv, j * w:(j + 1) * w])
        accs[t, j] = part if hh == 0 else accs[t, j] + part
        if hh == GLA_HEADS - 1 and j == _GLA_OUT_PIECES - 1:
            acc = jnp.concatenate([accs.pop((t, jj)) for jj in range(_GLA_OUT_PIECES)], axis=1)
            epilogue(t, xs.pop(t), acc)

    cur = [project_piece(0, 0, j) for j in range(_GLA_PROJ_PIECES)]
    prev = None
    for idx, (t, hh) in enumerate(items):
        if hh == 1 and t + 1 < n_tiles:
            xs[t + 1], hs[t + 1], gcs[t + 1] = prologue(t + 1)
        nxt, pending = [], []
        if idx + 1 < len(items):
            pending += [lambda j=j, idx=idx, nxt=nxt: nxt.append(project_piece(*items[idx + 1], j))
                        for j in range(_GLA_PROJ_PIECES)]
        if prev is not None:
            pending += [functools.partial(out_piece, *prev, j) for j in range(_GLA_OUT_PIECES)]
        pending.reverse()
        fill = lambda pending=pending: pending.pop()() if pending else None
        qk, v32, r = cur
        scores, state_ref[hh] = _gla_scores(qk, v32, gcs[t][:, hh * dk:(hh + 1) * dk], state_ref[hh], fill)
        prev = (t, hh, _gla_combine(scores, v32, r, hn_ref[...], sm_ref, vm_ref, fill))
        while pending:
            fill()
        cur = nxt
    for j in range(_GLA_OUT_PIECES):
        out_piece(*prev, j)


def _gla_layer(x, mod, ng, params, layer, mixer_index):
    w_in, w_alpha, b_alpha, head_norm, w_out = params
    bsz, s, d = x.shape
    c = ROW_TILE
    rank, qk = w_alpha.shape[1:]
    dk, dv = qk // GLA_HEADS, d // GLA_HEADS
    n_main = 2 * qk + 2 * d
    w_z = jnp.pad(w_in[mixer_index, :, n_main:], ((0, 0), (0, GLA_RANK_PAD - rank))).astype(BF16)
    w_al = jnp.pad(w_alpha[mixer_index], ((0, GLA_RANK_PAD - rank), (0, 0))).astype(BF16)
    stream_masks = jnp.asarray(_gla_stream_masks(c), F32)
    vreg_masks = jnp.asarray(_gla_vreg_masks(), F32)
    n_tiles = TILES_PER_STEP
    tile = pl.BlockSpec((None, n_tiles * c, d), lambda b, i: (b, i, 0))
    row = lambda v: v[:, None, :]
    mix_spec = lambda p: _layer_spec(p.shape, mixer_index)
    return pl.pallas_call(
        _gla_kernel,
        grid=(bsz, s // (n_tiles * c)),
        in_specs=[
            tile,
            pl.BlockSpec((None, None, 6, d), lambda b, i: (layer, b, 0, 0)),
            _layer_spec(ng.shape, layer),
            mix_spec(w_in),
            _const_spec(w_z.shape),
            _const_spec(w_al.shape),
            mix_spec(row(b_alpha)),
            mix_spec(row(head_norm)),
            _const_spec(stream_masks.shape),
            _const_spec(vreg_masks.shape),
            mix_spec(w_out),
        ],
        out_specs=tile,
        out_shape=jax.ShapeDtypeStruct(x.shape, F32),
        scratch_shapes=[
            pltpu.VMEM((GLA_HEADS, d, 2 * dk + 2 * dv), BF16),
            pltpu.VMEM(w_out.shape[1:], BF16),
            pltpu.VMEM((GLA_HEADS, dk, dv), F32),
        ],
        compiler_params=_params("arbitrary", "arbitrary"),
        name="gla_layer",
    )(x, mod, ng, w_in, w_z, w_al, row(b_alpha), row(head_norm), stream_masks, vreg_masks, w_out)


def kernel(x, c, ada_w, ada_b, norm_g, ffn_w_up, ffn_conv_w, ffn_conv_b, ffn_w_down, rg_w_in, rg_conv_w,
           rg_conv_b, rg_wa, rg_ba, rg_wx, rg_bx, rg_lambda, rg_w_out, gla_w_in, gla_w_alpha, gla_b_alpha,
           gla_norm_g, gla_w_out):
    depth = ada_w.shape[0]
    assert depth >= 1 and x.shape[1] % (ROW_TILE * TILES_PER_STEP) == 0
    mod = _ada_modulation(c, ada_w, ada_b)
    ffn_w_up_bf = ffn_w_up.astype(BF16)
    ffn_conv_b3 = ffn_conv_b[:, None, :]
    rg_params = (rg_w_in, rg_conv_w, rg_conv_b, rg_wa, rg_ba, rg_wx, rg_bx, rg_lambda, rg_w_out)
    gla_params = (gla_w_in, gla_w_alpha, gla_b_alpha, gla_norm_g, gla_w_out)
    for i in range(depth):
        j = i // 2
        if i % 2 == 0:
            x = _rglru_layer(x, mod, norm_g, rg_params, i, j, natural_in=(i == 0))
        else:
            x = _gla_layer(x, mod, norm_g, gla_params, i, j)
        x = _ffn_layer(x, mod, norm_g, ffn_w_up_bf, ffn_conv_w, ffn_conv_b3, ffn_w_down, i,
                       natural_out=(i == depth - 1))
    return x
```

```python
import functools
import math

import jax
import jax.numpy as jnp
import numpy as np
from jax import lax
from jax.experimental import pallas as pl
from jax.experimental.pallas import tpu as pltpu

F32 = jnp.float32
BF16 = jnp.bfloat16

EPS = 1e-6
RG_BLOCKS = 4
RG_C = 8.0
RG_CONV = 4
FFN_CONV = 3
GLA_HEADS = 4
GLA_TAU = 16.0
GLA_RANK_PAD = 128
SUBLANES = 8

ROW_TILE = 256
TILE_VREGS = ROW_TILE // SUBLANES
TILES_PER_STEP = 4
ADA_COL_TILE = 1024
FFN_COL_CHUNK = 256
VMEM_LIMIT = 56 * 1024 * 1024

_GELU_C0 = 2.0 * math.sqrt(2.0 / math.pi) * math.log2(math.e)
_GELU_C1 = _GELU_C0 * 0.044715


def _dot(a, b):
    return jnp.dot(a, b, preferred_element_type=F32)


def _dot_nt(a, b):
    return lax.dot_general(a, b, (((1,), (1,)), ((), ())), preferred_element_type=F32)


def _rms(x, g):
    return x * lax.rsqrt(jnp.mean(x * x, axis=-1, keepdims=True) + EPS) * g


def _gelu_times(x, v):
    neg_2z_log2e = x * (-_GELU_C0 - _GELU_C1 * (x * x))
    return (x * v) / (1.0 + jnp.exp2(neg_2z_log2e))


def _sublane(shape):
    return lax.broadcasted_iota(jnp.int32, shape, 0) % SUBLANES


def _vreg(a, j):
    return a[j * SUBLANES:(j + 1) * SUBLANES]


def _load_tile_order(ref):
    return jnp.concatenate([ref[:, j, :] for j in range(TILE_VREGS)], axis=0)


def _store_natural_order(ref, val):
    for j in range(TILE_VREGS):
        ref[:, j, :] = _vreg(val, j)


def _delay(x, tail, k):
    ts = x.shape[0]
    n_tail = tail.shape[0] // SUBLANES
    last = _sublane((SUBLANES, 1)) == SUBLANES - 1
    head = []
    for j in range(k):
        cur = _vreg(x, TILE_VREGS - k + j)
        prv = _vreg(tail, n_tail - k + j)
        head.append(pltpu.roll(jnp.where(last, prv, cur), 1, 0))
    return jnp.concatenate(head + [x[:ts - k * SUBLANES]], axis=0)


def _params(*sem):
    return pltpu.CompilerParams(dimension_semantics=sem, vmem_limit_bytes=VMEM_LIMIT)


def _const_spec(shape):
    n = len(shape)
    return pl.BlockSpec(shape, lambda *_: (0,) * n, pipeline_mode=pl.Buffered(1))


def _layer_spec(shape, layer):
    n = len(shape) - 1
    return pl.BlockSpec((None,) + tuple(shape[1:]), lambda *_: (layer,) + (0,) * n,
                        pipeline_mode=pl.Buffered(1))


def _ada_kernel(c_ref, w_ref, b_ref, o_ref):
    c = c_ref[...]
    c_act = (c * jax.nn.sigmoid(c)).astype(BF16)
    o_ref[...] = _dot(c_act, w_ref[...].astype(BF16)) + b_ref[...]


def _ada_modulation(c, ada_w, ada_b):
    depth, d, n = ada_w.shape
    bsz = c.shape[0]
    tn = ADA_COL_TILE
    out = pl.pallas_call(
        _ada_kernel,
        grid=(depth, n // tn),
        in_specs=[
            pl.BlockSpec((bsz, d), lambda l, j: (0, 0)),
            pl.BlockSpec((None, d, tn), lambda l, j: (l, 0, j)),
            pl.BlockSpec((None, 1, tn), lambda l, j: (l, 0, j)),
        ],
        out_specs=pl.BlockSpec((None, bsz, tn), lambda l, j: (l, 0, j)),
        out_shape=jax.ShapeDtypeStruct((depth, bsz, n), F32),
        compiler_params=_params("arbitrary", "arbitrary"),
        name="ada_mod",
    )(c, ada_w, ada_b.reshape(depth, 1, n))
    return out.reshape(depth, bsz, 6, d)


def _stream_scan(a, u, carry):
    hl = [_vreg(u, 0)]
    al = [_vreg(a, 0)]
    for j in range(1, TILE_VREGS):
        hl.append(_vreg(a, j) * hl[-1] + _vreg(u, j))
        al.append(_vreg(a, j) * al[-1])
    sub = _sublane((SUBLANES, 1))
    ea, eh = al[-1], hl[-1]
    for k in (1, 2, 4):
        keep = sub >= k
        ea_sh = jnp.where(keep, pltpu.roll(ea, k, 0), 1.0)
        eh_sh = jnp.where(keep, pltpu.roll(eh, k, 0), 0.0)
        eh = ea * eh_sh + eh
        ea = ea * ea_sh
    after = ea * carry + eh
    before = jnp.where(sub == 0, carry, pltpu.roll(after, 1, 0))
    h = jnp.concatenate([hl[j] + al[j] * before for j in range(TILE_VREGS)], axis=0)
    return h, after


def _first_step():
    return (pl.program_id(0) == 0) & (pl.program_id(1) == 0)


def _tile_order_gather(x_hbm, buf, sem, b, step, slot):
    n_tiles = buf.shape[1] // ROW_TILE
    return [
        pltpu.make_async_copy(x_hbm.at[b, step * n_tiles + t, :, j, :],
                              buf.at[slot, pl.ds(t * ROW_TILE + j * SUBLANES, SUBLANES), :], sem.at[slot])
        for t in range(n_tiles) for j in range(TILE_VREGS)]


def _rglru_kernel(natural_in, x_ref, mod_ref, ng_ref, win32_ref, cw_ref, cb_ref, ba_ref, bx_ref, lam_ref,
                  wa32_ref, wx32_ref, wout32_ref, o_ref, win_ref, wg_ref, wout_ref, tail_ref, hc_ref,
                  *gather_scratch):
    ts = ROW_TILE
    d = o_ref.shape[1]
    n_tiles = o_ref.shape[0] // ts
    blk = d // RG_BLOCKS
    n_tail = RG_CONV - 1

    if natural_in:
        xbuf, sem = gather_scratch
        b, i = pl.program_id(0), pl.program_id(1)
        n_steps = pl.num_programs(1)
        linear = b * n_steps + i
        slot = linear % 2

        @pl.when(linear == 0)
        def _():
            for cp in _tile_order_gather(x_ref, xbuf, sem, b, i, slot):
                cp.start()

        @pl.when(linear + 1 < pl.num_programs(0) * n_steps)
        def _():
            wrap = i + 1 == n_steps
            for cp in _tile_order_gather(x_ref, xbuf, sem, jnp.where(wrap, b + 1, b), jnp.where(wrap, 0, i + 1),
                                         1 - slot):
                cp.start()

        for cp in _tile_order_gather(x_ref, xbuf, sem, b, i, slot):
            cp.wait()

    @pl.when(_first_step())
    def _():
        win_ref[...] = win32_ref[...].astype(BF16)
        wout_ref[...] = wout32_ref[...].astype(BF16)
        for g in range(RG_BLOCKS):
            wg_ref[g, :, :blk] = wa32_ref[g].astype(BF16)
            wg_ref[g, :, blk:] = wx32_ref[g].astype(BF16)

    @pl.when(pl.program_id(1) == 0)
    def _():
        tail_ref[...] = jnp.zeros_like(tail_ref)
        hc_ref[...] = jnp.zeros_like(hc_ref)

    def prologue(t):
        x = xbuf[slot, t * ts:(t + 1) * ts, :] if natural_in else x_ref[t * ts:(t + 1) * ts, :]
        return x, (_rms(x, ng_ref[0:1, :]) * (1.0 + mod_ref[1:2, :]) + mod_ref[0:1, :]).astype(BF16)

    def epilogue(t, x, acc):
        o_ref[t * ts:(t + 1) * ts, :] = x + mod_ref[2:3, :] * _rms(acc, ng_ref[1:2, :])

    def project(t, g):
        return (_dot(hs[t], win_ref[:, g * blk:(g + 1) * blk]),
                _dot(hs[t], win_ref[:, d + g * blk:d + (g + 1) * blk]))

    def mix(g, gate_br, xb):
        cols = slice(g * blk, (g + 1) * blk)
        tail = tail_ref[:, cols]
        tail_ref[:, cols] = xb[ts - n_tail * SUBLANES:, :]
        xc = cb_ref[:, cols] + _delay(xb, tail, 3) * cw_ref[0:1, cols]
        xc = xc + _delay(xb, tail, 2) * cw_ref[1:2, cols]
        xc = xc + _delay(xb, tail, 1) * cw_ref[2:3, cols]
        xc = xc + xb * cw_ref[3:4, cols]
        p = _dot(xc.astype(BF16), wg_ref[g])
        r = jax.nn.sigmoid(p[:, :blk] + ba_ref[:, cols])
        i_g = jax.nn.sigmoid(p[:, blk:] + bx_ref[:, cols])
        nl = -lam_ref[:, cols]
        softplus_nl = jnp.maximum(nl, 0.0) + jnp.log1p(jnp.exp(-jnp.abs(nl)))
        log_a = r * ((-RG_C) * softplus_nl)
        a = jnp.exp(log_a)
        m2 = jnp.tanh(log_a) * (-1.0 - a * a)
        u = jnp.where(m2 > 0.0, m2 * lax.rsqrt(m2), 0.0) * (i_g * xc)
        hs_g, after = _stream_scan(a, u, hc_ref[SUBLANES - 1:SUBLANES, cols])
        hc_ref[:, cols] = after
        return _gelu_times(gate_br, hs_g).astype(BF16)

    items = [(t, g) for t in range(n_tiles) for g in range(RG_BLOCKS)]
    xs, hs, accs = {}, {}, {}
    xs[0], hs[0] = prologue(0)

    def out_project(t, g, y):
        part = _dot(y, wout_ref[g * blk:(g + 1) * blk, :])
        accs[t] = part if g == 0 else accs[t] + part
        if g == RG_BLOCKS - 1:
            epilogue(t, xs.pop(t), accs.pop(t))

    cur = project(0, 0)
    prev = None
    for idx, (t, g) in enumerate(items):
        if g == 1 and t + 1 < n_tiles:
            xs[t + 1], hs[t + 1] = prologue(t + 1)
        nxt = project(*items[idx + 1]) if idx + 1 < len(items) else None
        if prev is not None:
            out_project(*prev)
        prev = (t, g, mix(g, *cur))
        cur = nxt
    out_project(*prev)


def _rglru_layer(x, mod, ng, params, layer, mixer_index, natural_in):
    w_in, conv_w, conv_b, wa, ba, wx, bx, lam, w_out = params
    bsz, s, d = x.shape
    ts = ROW_TILE
    n_tiles = TILES_PER_STEP
    blk = d // RG_BLOCKS
    tile = pl.BlockSpec((None, n_tiles * ts, d), lambda b, i: (b, i, 0))
    x_tile, gather_scratch = tile, []
    if natural_in:
        x = x.reshape(bsz, s // ts, SUBLANES, TILE_VREGS, d)
        x_tile = pl.BlockSpec(memory_space=pl.ANY)
        gather_scratch = [pltpu.VMEM((2, n_tiles * ts, d), F32), pltpu.SemaphoreType.DMA((2,))]
    row = lambda v: v[:, None, :]
    mix_spec = lambda p: _layer_spec(p.shape, mixer_index)
    return pl.pallas_call(
        functools.partial(_rglru_kernel, natural_in),
        grid=(bsz, s // (n_tiles * ts)),
        in_specs=[
            x_tile,
            pl.BlockSpec((None, None, 6, d), lambda b, i: (layer, b, 0, 0)),
            _layer_spec(ng.shape, layer),
            mix_spec(w_in), mix_spec(conv_w), mix_spec(row(conv_b)), mix_spec(row(ba)), mix_spec(row(bx)),
            mix_spec(row(lam)), mix_spec(wa), mix_spec(wx), mix_spec(w_out),
        ],
        out_specs=tile,
        out_shape=jax.ShapeDtypeStruct((bsz, s, d), F32),
        scratch_shapes=[
            pltpu.VMEM(w_in.shape[1:], BF16),
            pltpu.VMEM((RG_BLOCKS, blk, 2 * blk), BF16),
            pltpu.VMEM(w_out.shape[1:], BF16),
            pltpu.VMEM(((RG_CONV - 1) * SUBLANES, d), F32),
            pltpu.VMEM((SUBLANES, d), F32),
        ] + gather_scratch,
        compiler_params=_params("arbitrary", "arbitrary"),
        name="rglru_layer",
    )(x, mod, ng, w_in, conv_w, row(conv_b), row(ba), row(bx), row(lam), wa, wx, w_out)


def _ffn_kernel(natural_out, x_ref, mod_ref, ng_ref, wup_ref, cw_ref, cb_ref, wdn32_ref, o_ref, wdn_ref,
                tail_ref):
    ts = ROW_TILE
    n_tiles = x_ref.shape[0] // ts
    f = wdn_ref.shape[0]
    fc = FFN_COL_CHUNK
    n_chunks = f // fc
    n_tail = FFN_CONV - 1

    @pl.when(_first_step())
    def _():
        wdn_ref[...] = wdn32_ref[...].astype(BF16)

    @pl.when(pl.program_id(1) == 0)
    def _():
        tail_ref[...] = jnp.zeros_like(tail_ref)

    def prologue(t):
        x = x_ref[t * ts:(t + 1) * ts, :]
        return x, (_rms(x, ng_ref[2:3, :]) * (1.0 + mod_ref[4:5, :]) + mod_ref[3:4, :]).astype(BF16)

    def epilogue(t, x, acc):
        out = x + mod_ref[5:6, :] * _rms(acc, ng_ref[3:4, :])
        if natural_out:
            _store_natural_order(o_ref.at[t], out)
        else:
            o_ref[t * ts:(t + 1) * ts, :] = out

    def conv_cols(up, c0):
        cols = slice(c0, c0 + fc)
        tail = tail_ref[:, cols]
        tail_ref[:, cols] = up[ts - n_tail * SUBLANES:, :]
        y = cb_ref[:, cols] + _delay(up, tail, 2) * cw_ref[0:1, cols]
        y = y + _delay(up, tail, 1) * cw_ref[1:2, cols]
        return y + up * cw_ref[2:3, cols]

    items = [(t, c) for t in range(n_tiles) for c in range(n_chunks)]
    xs, hs, accs = {}, {}, {}
    xs[0], hs[0] = prologue(0)

    def up_pair(t, c):
        return (_dot(hs[t], wup_ref[:, c * fc:(c + 1) * fc]), _dot(hs[t], wup_ref[:, f + c * fc:f + (c + 1) * fc]))

    def down(t, c, act):
        part = _dot(act, wdn_ref[c * fc:(c + 1) * fc, :])
        accs[t] = part if c == 0 else accs[t] + part
        if c == n_chunks - 1:
            epilogue(t, xs.pop(t), accs.pop(t))

    ups = up_pair(0, 0)
    prev = None
    for idx, (t, c) in enumerate(items):
        if c == n_chunks - 4 and t + 1 < n_tiles:
            xs[t + 1], hs[t + 1] = prologue(t + 1)
        nxt = up_pair(*items[idx + 1]) if idx + 1 < len(items) else None
        if prev is not None:
            down(*prev)
        g = conv_cols(ups[0], c * fc)
        val = conv_cols(ups[1], f + c * fc)
        prev = (t, c, _gelu_times(g, val).astype(BF16))
        ups = nxt
    down(*prev)


def _ffn_layer(x, mod, ng, w_up, conv_w, conv_b, w_down, layer, natural_out):
    bsz, s, d = x.shape
    ts = ROW_TILE
    n_tiles = TILES_PER_STEP
    rows = pl.BlockSpec((None, n_tiles * ts, d), lambda b, i: (b, i, 0))
    out_tile, out_shape = rows, (bsz, s, d)
    if natural_out:
        out_tile = pl.BlockSpec((None, n_tiles, SUBLANES, TILE_VREGS, d), lambda b, i: (b, i, 0, 0, 0))
        out_shape = (bsz, s // ts, SUBLANES, TILE_VREGS, d)
    out = pl.pallas_call(
        functools.partial(_ffn_kernel, natural_out),
        grid=(bsz, s // (n_tiles * ts)),
        in_specs=[
            rows,
            pl.BlockSpec((None, None, 6, d), lambda b, i: (layer, b, 0, 0)),
            _layer_spec(ng.shape, layer),
            _layer_spec(w_up.shape, layer),
            _layer_spec(conv_w.shape, layer),
            _layer_spec(conv_b.shape, layer),
            _layer_spec(w_down.shape, layer),
        ],
        out_specs=out_tile,
        out_shape=jax.ShapeDtypeStruct(out_shape, F32),
        scratch_shapes=[
            pltpu.VMEM(w_down.shape[1:], BF16),
            pltpu.VMEM(((FFN_CONV - 1) * SUBLANES, w_up.shape[2]), F32),
        ],
        compiler_params=_params("arbitrary", "arbitrary"),
        name="ffn_layer",
    )(x, mod, ng, w_up, conv_w, conv_b, w_down)
    return out.reshape(bsz, s, d)


_STREAM_LEVELS = tuple(TILE_VREGS << i for i in (2, 1, 0))
_VREG_LEVELS = tuple(TILE_VREGS >> i for i in range(1, 4))
_DIRECT_BLOCK = 4
_MASKED_STREAM_LEVELS = _STREAM_LEVELS[1:]
_EXCLUDE = 1e30
_GLA_PROJ_PIECES = 3
_GLA_OUT_PIECES = 2


def _tile_order_times(c):
    p = np.arange(c)
    return (p % SUBLANES) * (c // SUBLANES) + p // SUBLANES


def _gla_stream_masks(c):
    t = _tile_order_times(c)
    return np.stack([(t[:, None] // (2 * b) == t[None, :] // (2 * b)) for b in _MASKED_STREAM_LEVELS]
                    ).astype(np.float32)


def _vreg_level_rows(b):
    upper = [j for j in range(TILE_VREGS) if (j // b) % 2 == 1]
    lower = [j for j in range(TILE_VREGS) if (j // b) % 2 == 0]
    return upper, lower


def _gla_vreg_masks():
    out = []
    s = np.arange(SUBLANES)
    for b in _VREG_LEVELS:
        upper, lower = _vreg_level_rows(b)
        ju = np.repeat(np.array(upper), SUBLANES)[:, None]
        jl = np.repeat(np.array(lower), SUBLANES)[None, :]
        su = np.tile(s, len(upper))[:, None]
        sl = np.tile(s, len(lower))[None, :]
        out.append((su == sl) & (ju // (2 * b) == jl // (2 * b)))
    return np.stack(out).astype(np.float32)


def _cumsum_tile_order(g):
    rows = [_vreg(g, 0)]
    for j in range(1, TILE_VREGS):
        rows.append(rows[-1] + _vreg(g, j))
    total = rows[-1]
    sub = _sublane((SUBLANES, 1))
    incl = total
    for k in (1, 2, 4):
        incl = incl + jnp.where(sub >= k, pltpu.roll(incl, k, 0), 0.0)
    before = incl - total
    return jnp.concatenate([r + before for r in rows], axis=0)


def _gla_scores(qk, v32, gc, state, fill):
    c, dk = gc.shape
    vreg = _vreg
    q = qk[:, :dk] * (dk ** -0.5)
    k = qk[:, dk:]
    v = v32.astype(BF16)
    g_last = gc[c - 1:c, :]

    fill()
    o_inter = _dot((q * jnp.exp(gc)).astype(BF16), state.astype(BF16))

    sub = _sublane((SUBLANES, 1))
    g_end = vreg(gc, TILE_VREGS - 1)
    tile_rows = lambda a: jnp.concatenate([a] * TILE_VREGS, axis=0)
    p_stream = []
    for b in _STREAM_LEVELS:
        w = b // TILE_VREGS
        g_mid = None
        for first in range(0, SUBLANES, 2 * w):
            row = jnp.broadcast_to(g_end[first + w - 1:first + w, :], (SUBLANES, dk))
            g_mid = row if g_mid is None else jnp.where(sub >= first, row, g_mid)
        upper = (lax.shift_right_logical(sub, w.bit_length() - 1) & 1) == 1
        g_q = tile_rows(jnp.where(upper, g_mid, _EXCLUDE))
        g_k = tile_rows(jnp.where(upper, -_EXCLUDE, g_mid))
        q_b = (q * jnp.exp(gc - g_q)).astype(BF16)
        k_b = (k * jnp.exp(g_k - gc)).astype(BF16)
        p_stream.append(_dot_nt(q_b, k_b))
    fill()

    p_vreg = []
    for b in _VREG_LEVELS:
        upper_rows, _ = _vreg_level_rows(b)
        q_rows, k_rows = [], []
        for j in range(TILE_VREGS):
            jm = (j // (2 * b)) * (2 * b) + b - 1
            if j in upper_rows:
                q_rows.append(vreg(q, j) * jnp.exp(vreg(gc, j) - vreg(gc, jm)))
            elif j == jm:
                k_rows.append(vreg(k, j))
            else:
                k_rows.append(vreg(k, j) * jnp.exp(vreg(gc, jm) - vreg(gc, j)))
        p_vreg.append(_dot_nt(jnp.concatenate(q_rows, axis=0).astype(BF16),
                              jnp.concatenate(k_rows, axis=0).astype(BF16)))
    fill()

    k_rev = (k * jnp.exp(g_last - gc)).astype(BF16)
    upd = lax.dot_general(k_rev, v, (((0,), (0,)), ((), ())), preferred_element_type=F32)
    decay_rows = jnp.broadcast_to(jnp.exp(g_last), (dk, dk))
    decay_col = jnp.transpose(decay_rows)[:, 0:1]
    new_state = decay_col * state + upd

    direct = []
    for ju in range(TILE_VREGS):
        terms = vreg(v32, ju) * jnp.sum(vreg(q, ju) * vreg(k, ju), axis=-1, keepdims=True)
        for jl in range((ju // _DIRECT_BLOCK) * _DIRECT_BLOCK, ju):
            w = vreg(q, ju) * vreg(k, jl) * jnp.exp(vreg(gc, ju) - vreg(gc, jl))
            terms = terms + vreg(v32, jl) * jnp.sum(w, axis=-1, keepdims=True)
        direct.append(terms)
    return (o_inter, p_stream, p_vreg, direct), new_state


def _gla_combine(scores, v32, r, hn, sm_ref, vm_ref, fill):
    o, p_stream, p_vreg, direct = scores
    v = v32.astype(BF16)

    attn = None
    for b, p in zip(_STREAM_LEVELS, p_stream):
        if b in _MASKED_STREAM_LEVELS:
            p = p * sm_ref[_MASKED_STREAM_LEVELS.index(b)]
        attn = p if attn is None else attn + p
    o = o + _dot(attn.astype(BF16), v)
    fill()

    o_rows = [_vreg(o, j) + direct[j] for j in range(TILE_VREGS)]
    for li, (b, p) in enumerate(zip(_VREG_LEVELS, p_vreg)):
        upper_rows, _ = _vreg_level_rows(b)
        p = (p * vm_ref[li]).astype(BF16)
        v_low = jnp.concatenate(
            [v[j0 * SUBLANES:(j0 + b) * SUBLANES] for j0 in range(0, TILE_VREGS, 2 * b)], axis=0)
        o_up = _dot(p, v_low)
        if li == 1:
            fill()
        for idx, j in enumerate(upper_rows):
            o_rows[j] = o_rows[j] + _vreg(o_up, idx)
    o = jnp.concatenate(o_rows, axis=0)
    return (_rms(o, hn) * (r * jax.nn.sigmoid(r))).astype(BF16)


def _gla_kernel(x_ref, mod_ref, ng_ref, win32_ref, wz_ref, wal_ref, bal_ref, hn_ref, sm_ref, vm_ref,
                wout32_ref, o_ref, wh_ref, wout_ref, state_ref):
    c = ROW_TILE
    d = x_ref.shape[1]
    n_tiles = x_ref.shape[0] // c
    dk, dv = state_ref.shape[1:]
    qk = GLA_HEADS * dk

    @pl.when(_first_step())
    def _():
        wout_ref[...] = wout32_ref[...].astype(BF16)
        for hh in range(GLA_HEADS):
            for dst, src, w in ((0, hh * dk, dk), (dk, qk + hh * dk, dk), (2 * dk, 2 * qk + hh * dv, dv),
                                (2 * dk + dv, 2 * qk + d + hh * dv, dv)):
                wh_ref[hh, :, dst:dst + w] = win32_ref[:, src:src + w].astype(BF16)

    @pl.when(pl.program_id(1) == 0)
    def _():
        state_ref[...] = jnp.zeros_like(state_ref)

    def prologue(t):
        x = x_ref[t * c:(t + 1) * c, :]
        h = (_rms(x, ng_ref[0:1, :]) * (1.0 + mod_ref[1:2, :]) + mod_ref[0:1, :]).astype(BF16)
        z = _dot(h, wz_ref[...])
        ga = _dot(z.astype(BF16), wal_ref[...]) + bal_ref[...]
        g = (jnp.minimum(ga, 0.0) - jnp.log1p(jnp.exp(-jnp.abs(ga)))) / GLA_TAU
        return x, h, _cumsum_tile_order(g)

    def epilogue(t, x, acc):
        o_ref[t * c:(t + 1) * c, :] = x + mod_ref[2:3, :] * _rms(acc, ng_ref[1:2, :])

    items = [(t, hh) for t in range(n_tiles) for hh in range(GLA_HEADS)]
    xs, hs, gcs, accs = {}, {}, {}, {}
    xs[0], hs[0], gcs[0] = prologue(0)

    def project_piece(t, hh, j):
        w = 2 * dk if j == 0 else dv
        c0 = 0 if j == 0 else 2 * dk + (j - 1) * dv
        return _dot(hs[t], wh_ref[hh, :, c0:c0 + w])

    def out_piece(t, hh, y, j):
        w = d // _GLA_OUT_PIECES
        part = _dot(y, wout_ref[hh * dv:(hh + 1) * dv, j * w:(j + 1) * w])
        accs[t, j] = part if hh == 0 else accs[t, j] + part
        if hh == GLA_HEADS - 1 and j == _GLA_OUT_PIECES - 1:
            acc = jnp.concatenate([accs.pop((t, jj)) for jj in range(_GLA_OUT_PIECES)], axis=1)
            epilogue(t, xs.pop(t), acc)

    cur = [project_piece(0, 0, j) for j in range(_GLA_PROJ_PIECES)]
    prev = None
    for idx, (t, hh) in enumerate(items):
        if hh == 1 and t + 1 < n_tiles:
            xs[t + 1], hs[t + 1], gcs[t + 1] = prologue(t + 1)
        nxt, pending = [], []
        if idx + 1 < len(items):
            pending += [lambda j=j, idx=idx, nxt=nxt: nxt.append(project_piece(*items[idx + 1], j))
                        for j in range(_GLA_PROJ_PIECES)]
        if prev is not None:
            pending += [functools.partial(out_piece, *prev, j) for j in range(_GLA_OUT_PIECES)]
        pending.reverse()
        fill = lambda pending=pending: pending.pop()() if pending else None
        qk, v32, r = cur
        scores, state_ref[hh] = _gla_scores(qk, v32, gcs[t][:, hh * dk:(hh + 1) * dk], state_ref[hh], fill)
        prev = (t, hh, _gla_combine(scores, v32, r, hn_ref[...], sm_ref, vm_ref, fill))
        while pending:
            fill()
        cur = nxt
    for j in range(_GLA_OUT_PIECES):
        out_piece(*prev, j)


def _gla_layer(x, mod, ng, params, layer, mixer_index):
    w_in, w_alpha, b_alpha, head_norm, w_out = params
    bsz, s, d = x.shape
    c = ROW_TILE
    rank, qk = w_alpha.shape[1:]
    dk, dv = qk // GLA_HEADS, d // GLA_HEADS
    n_main = 2 * qk + 2 * d
    w_z = jnp.pad(w_in[mixer_index, :, n_main:], ((0, 0), (0, GLA_RANK_PAD - rank))).astype(BF16)
    w_al = jnp.pad(w_alpha[mixer_index], ((0, GLA_RANK_PAD - rank), (0, 0))).astype(BF16)
    stream_masks = jnp.asarray(_gla_stream_masks(c), F32)
    vreg_masks = jnp.asarray(_gla_vreg_masks(), F32)
    n_tiles = TILES_PER_STEP
    tile = pl.BlockSpec((None, n_tiles * c, d), lambda b, i: (b, i, 0))
    row = lambda v: v[:, None, :]
    mix_spec = lambda p: _layer_spec(p.shape, mixer_index)
    return pl.pallas_call(
        _gla_kernel,
        grid=(bsz, s // (n_tiles * c)),
        in_specs=[
            tile,
            pl.BlockSpec((None, None, 6, d), lambda b, i: (layer, b, 0, 0)),
            _layer_spec(ng.shape, layer),
            mix_spec(w_in),
            _const_spec(w_z.shape),
            _const_spec(w_al.shape),
            mix_spec(row(b_alpha)),
            mix_spec(row(head_norm)),
            _const_spec(stream_masks.shape),
            _const_spec(vreg_masks.shape),
            mix_spec(w_out),
        ],
        out_specs=tile,
        out_shape=jax.ShapeDtypeStruct(x.shape, F32),
        scratch_shapes=[
            pltpu.VMEM((GLA_HEADS, d, 2 * dk + 2 * dv), BF16),
            pltpu.VMEM(w_out.shape[1:], BF16),
            pltpu.VMEM((GLA_HEADS, dk, dv), F32),
        ],
        compiler_params=_params("arbitrary", "arbitrary"),
        name="gla_layer",
    )(x, mod, ng, w_in, w_z, w_al, row(b_alpha), row(head_norm), stream_masks, vreg_masks, w_out)


def kernel(x, c, ada_w, ada_b, norm_g, ffn_w_up, ffn_conv_w, ffn_conv_b, ffn_w_down, rg_w_in, rg_conv_w,
           rg_conv_b, rg_wa, rg_ba, rg_wx, rg_bx, rg_lambda, rg_w_out, gla_w_in, gla_w_alpha, gla_b_alpha,
           gla_norm_g, gla_w_out):
    depth = ada_w.shape[0]
    assert depth >= 1 and x.shape[1] % (ROW_TILE * TILES_PER_STEP) == 0
    mod = _ada_modulation(c, ada_w, ada_b)
    ffn_w_up_bf = ffn_w_up.astype(BF16)
    ffn_conv_b3 = ffn_conv_b[:, None, :]
    rg_params = (rg_w_in, rg_conv_w, rg_conv_b, rg_wa, rg_ba, rg_wx, rg_bx, rg_lambda, rg_w_out)
    gla_params = (gla_w_in, gla_w_alpha, gla_b_alpha, gla_norm_g, gla_w_out)
    for i in range(depth):
        j = i // 2
        if i % 2 == 0:
            x = _rglru_layer(x, mod, norm_g, rg_params, i, j, natural_in=(i == 0))
        else:
            x = _gla_layer(x, mod, norm_g, gla_params, i, j)
        x = _ffn_layer(x, mod, norm_g, ffn_w_up_bf, ffn_conv_w, ffn_conv_b3, ffn_w_down, i,
                       natural_out=(i == depth - 1))
    return x
```

```python
import functools
import math

import jax
import jax.numpy as jnp
import numpy as np
from jax import lax
from jax.experimental import pallas as pl
from jax.experimental.pallas import tpu as pltpu

F32 = jnp.float32
BF16 = jnp.bfloat16

EPS = 1e-6
RG_BLOCKS = 4
RG_C = 8.0
RG_CONV = 4
FFN_CONV = 3
GLA_HEADS = 4
GLA_TAU = 16.0
GLA_RANK_PAD = 128
SUBLANES = 8

ROW_TILE = 256
TILE_VREGS = ROW_TILE // SUBLANES
TILES_PER_STEP = 4
ADA_COL_TILE = 1024
FFN_COL_CHUNK = 256
VMEM_LIMIT = 56 * 1024 * 1024

_GELU_C0 = 2.0 * math.sqrt(2.0 / math.pi) * math.log2(math.e)
_GELU_C1 = _GELU_C0 * 0.044715


def _dot(a, b):
    return jnp.dot(a, b, preferred_element_type=F32)


def _dot_nt(a, b):
    return lax.dot_general(a, b, (((1,), (1,)), ((), ())), preferred_element_type=F32)


def _rms(x, g):
    return x * lax.rsqrt(jnp.mean(x * x, axis=-1, keepdims=True) + EPS) * g


def _gelu_times(x, v):
    neg_2z_log2e = x * (-_GELU_C0 - _GELU_C1 * (x * x))
    return (x * v) / (1.0 + jnp.exp2(neg_2z_log2e))


def _sublane(shape):
    return lax.broadcasted_iota(jnp.int32, shape, 0) % SUBLANES


def _vreg(a, j):
    return a[j * SUBLANES:(j + 1) * SUBLANES]


def _delay(x, tail, k):
    ts = x.shape[0]
    n_tail = tail.shape[0] // SUBLANES
    last = _sublane((SUBLANES, 1)) == SUBLANES - 1
    head = []
    for j in range(k):
        cur = _vreg(x, TILE_VREGS - k + j)
        prv = _vreg(tail, n_tail - k + j)
        head.append(pltpu.roll(jnp.where(last, prv, cur), 1, 0))
    return jnp.concatenate(head + [x[:ts - k * SUBLANES]], axis=0)


def _params(*sem):
    return pltpu.CompilerParams(dimension_semantics=sem, vmem_limit_bytes=VMEM_LIMIT)


def _const_spec(shape):
    n = len(shape)
    return pl.BlockSpec(shape, lambda *_: (0,) * n, pipeline_mode=pl.Buffered(1))


def _layer_spec(shape, layer):
    n = len(shape) - 1
    return pl.BlockSpec((None,) + tuple(shape[1:]), lambda *_: (layer,) + (0,) * n,
                        pipeline_mode=pl.Buffered(1))


def _ada_kernel(c_ref, w_ref, b_ref, o_ref):
    c = c_ref[...]
    c_act = (c * jax.nn.sigmoid(c)).astype(BF16)
    o_ref[...] = _dot(c_act, w_ref[...].astype(BF16)) + b_ref[...]


def _ada_modulation(c, ada_w, ada_b):
    depth, d, n = ada_w.shape
    bsz = c.shape[0]
    tn = ADA_COL_TILE
    out = pl.pallas_call(
        _ada_kernel,
        grid=(depth, n // tn),
        in_specs=[
            pl.BlockSpec((bsz, d), lambda l, j: (0, 0)),
            pl.BlockSpec((None, d, tn), lambda l, j: (l, 0, j)),
            pl.BlockSpec((None, 1, tn), lambda l, j: (l, 0, j)),
        ],
        out_specs=pl.BlockSpec((None, bsz, tn), lambda l, j: (l, 0, j)),
        out_shape=jax.ShapeDtypeStruct((depth, bsz, n), F32),
        compiler_params=_params("arbitrary", "arbitrary"),
        name="ada_mod",
    )(c, ada_w, ada_b.reshape(depth, 1, n))
    return out.reshape(depth, bsz, 6, d)


def _stream_scan(a, u, carry):
    hl = [_vreg(u, 0)]
    al = [_vreg(a, 0)]
    for j in range(1, TILE_VREGS):
        hl.append(_vreg(a, j) * hl[-1] + _vreg(u, j))
        al.append(_vreg(a, j) * al[-1])
    sub = _sublane((SUBLANES, 1))
    ea, eh = al[-1], hl[-1]
    for k in (1, 2, 4):
        keep = sub >= k
        ea_sh = jnp.where(keep, pltpu.roll(ea, k, 0), 1.0)
        eh_sh = jnp.where(keep, pltpu.roll(eh, k, 0), 0.0)
        eh = ea * eh_sh + eh
        ea = ea * ea_sh
    after = ea * carry + eh
    before = jnp.where(sub == 0, carry, pltpu.roll(after, 1, 0))
    h = jnp.concatenate([hl[j] + al[j] * before for j in range(TILE_VREGS)], axis=0)
    return h, after


def _first_step():
    return (pl.program_id(0) == 0) & (pl.program_id(1) == 0)


def _tile_order_gather(x_hbm, buf, sem, b, step, slot):
    n_tiles = buf.shape[1] // ROW_TILE
    return [
        pltpu.make_async_copy(x_hbm.at[b, step * n_tiles + t, :, j, :],
                              buf.at[slot, pl.ds(t * ROW_TILE + j * SUBLANES, SUBLANES), :], sem.at[slot])
        for t in range(n_tiles) for j in range(TILE_VREGS)]


def _rglru_kernel(natural_in, x_ref, mod_ref, ng_ref, win32_ref, cw_ref, cb_ref, ba_ref, bx_ref, lam_ref,
                  wa32_ref, wx32_ref, wout32_ref, o_ref, win_ref, wg_ref, wout_ref, tail_ref, hc_ref,
                  *gather_scratch):
    ts = ROW_TILE
    d = o_ref.shape[1]
    n_tiles = o_ref.shape[0] // ts
    blk = d // RG_BLOCKS
    n_tail = RG_CONV - 1

    if natural_in:
        xbuf, sem = gather_scratch
        b, i = pl.program_id(0), pl.program_id(1)
        n_steps = pl.num_programs(1)
        linear = b * n_steps + i
        slot = linear % 2

        @pl.when(linear == 0)
        def _():
            for cp in _tile_order_gather(x_ref, xbuf, sem, b, i, slot):
                cp.start()

        @pl.when(linear + 1 < pl.num_programs(0) * n_steps)
        def _():
            wrap = i + 1 == n_steps
            for cp in _tile_order_gather(x_ref, xbuf, sem, jnp.where(wrap, b + 1, b), jnp.where(wrap, 0, i + 1),
                                         1 - slot):
                cp.start()

        for cp in _tile_order_gather(x_ref, xbuf, sem, b, i, slot):
            cp.wait()

    @pl.when(_first_step())
    def _():
        win_ref[...] = win32_ref[...].astype(BF16)
        wout_ref[...] = wout32_ref[...].astype(BF16)
        for g in range(RG_BLOCKS):
            wg_ref[g, :, :blk] = wa32_ref[g].astype(BF16)
            wg_ref[g, :, blk:] = wx32_ref[g].astype(BF16)

    @pl.when(pl.program_id(1) == 0)
    def _():
        tail_ref[...] = jnp.zeros_like(tail_ref)
        hc_ref[...] = jnp.zeros_like(hc_ref)

    def prologue(t):
        x = xbuf[slot, t * ts:(t + 1) * ts, :] if natural_in else x_ref[t * ts:(t + 1) * ts, :]
        return x, (_rms(x, ng_ref[0:1, :]) * (1.0 + mod_ref[1:2, :]) + mod_ref[0:1, :]).astype(BF16)

    def epilogue(t, x, acc):
        o_ref[t * ts:(t + 1) * ts, :] = x + mod_ref[2:3, :] * _rms(acc, ng_ref[1:2, :])

    def project(t, g):
        return (_dot(hs[t], win_ref[:, g * blk:(g + 1) * blk]),
                _dot(hs[t], win_ref[:, d + g * blk:d + (g + 1) * blk]))

    def mix(g, gate_br, xb):
        cols = slice(g * blk, (g + 1) * blk)
        tail = tail_ref[:, cols]
        tail_ref[:, cols] = xb[ts - n_tail * SUBLANES:, :]
        xc = cb_ref[:, cols] + _delay(xb, tail, 3) * cw_ref[0:1, cols]
        xc = xc + _delay(xb, tail, 2) * cw_ref[1:2, cols]
        xc = xc + _delay(xb, tail, 1) * cw_ref[2:3, cols]
        xc = xc + xb * cw_ref[3:4, cols]
        p = _dot(xc.astype(BF16), wg_ref[g])
        r = jax.nn.sigmoid(p[:, :blk] + ba_ref[:, cols])
        i_g = jax.nn.sigmoid(p[:, blk:] + bx_ref[:, cols])
        nl = -lam_ref[:, cols]
        softplus_nl = jnp.maximum(nl, 0.0) + jnp.log1p(jnp.exp(-jnp.abs(nl)))
        log_a = r * ((-RG_C) * softplus_nl)
        a = jnp.exp(log_a)
        m2 = jnp.tanh(log_a) * (-1.0 - a * a)
        u = jnp.where(m2 > 0.0, m2 * lax.rsqrt(m2), 0.0) * (i_g * xc)
        hs_g, after = _stream_scan(a, u, hc_ref[SUBLANES - 1:SUBLANES, cols])
        hc_ref[:, cols] = after
        return _gelu_times(gate_br, hs_g).astype(BF16)

    items = [(t, g) for t in range(n_tiles) for g in range(RG_BLOCKS)]
    xs, hs, accs = {}, {}, {}
    xs[0], hs[0] = prologue(0)

    def out_project(t, g, y):
        part = _dot(y, wout_ref[g * blk:(g + 1) * blk, :])
        accs[t] = part if g == 0 else accs[t] + part
        if g == RG_BLOCKS - 1:
            epilogue(t, xs.pop(t), accs.pop(t))

    cur = project(0, 0)
    prev = None
    for idx, (t, g) in enumerate(items):
        if g == 1 and t + 1 < n_tiles:
            xs[t + 1], hs[t + 1] = prologue(t + 1)
        nxt = project(*items[idx + 1]) if idx + 1 < len(items) else None
        if prev is not None:
            out_project(*prev)
        prev = (t, g, mix(g, *cur))
        cur = nxt
    out_project(*prev)


def _rglru_layer(x, mod, ng, params, layer, mixer_index, natural_in):
    w_in, conv_w, conv_b, wa, ba, wx, bx, lam, w_out = params
    bsz, s, d = x.shape
    ts = ROW_TILE
    n_tiles = TILES_PER_STEP
    blk = d // RG_BLOCKS
    tile = pl.BlockSpec((None, n_tiles * ts, d), lambda b, i: (b, i, 0))
    x_tile, gather_scratch = tile, []
    if natural_in:
        x = x.reshape(bsz, s // ts, SUBLANES, TILE_VREGS, d)
        x_tile = pl.BlockSpec(memory_space=pl.ANY)
        gather_scratch = [pltpu.VMEM((2, n_tiles * ts, d), F32), pltpu.SemaphoreType.DMA((2,))]
    row = lambda v: v[:, None, :]
    mix_spec = lambda p: _layer_spec(p.shape, mixer_index)
    return pl.pallas_call(
        functools.partial(_rglru_kernel, natural_in),
        grid=(bsz, s // (n_tiles * ts)),
        in_specs=[
            x_tile,
            pl.BlockSpec((None, None, 6, d), lambda b, i: (layer, b, 0, 0)),
            _layer_spec(ng.shape, layer),
            mix_spec(w_in), mix_spec(conv_w), mix_spec(row(conv_b)), mix_spec(row(ba)), mix_spec(row(bx)),
            mix_spec(row(lam)), mix_spec(wa), mix_spec(wx), mix_spec(w_out),
        ],
        out_specs=tile,
        out_shape=jax.ShapeDtypeStruct((bsz, s, d), F32),
        scratch_shapes=[
            pltpu.VMEM(w_in.shape[1:], BF16),
            pltpu.VMEM((RG_BLOCKS, blk, 2 * blk), BF16),
            pltpu.VMEM(w_out.shape[1:], BF16),
            pltpu.VMEM(((RG_CONV - 1) * SUBLANES, d), F32),
            pltpu.VMEM((SUBLANES, d), F32),
        ] + gather_scratch,
        compiler_params=_params("arbitrary", "arbitrary"),
        name="rglru_layer",
    )(x, mod, ng, w_in, conv_w, row(conv_b), row(ba), row(bx), row(lam), wa, wx, w_out)


def _natural_order_scatter(buf, o_hbm, sem, b, step, slot, t):
    n_tiles = buf.shape[1] // ROW_TILE
    return [
        pltpu.make_async_copy(buf.at[slot, pl.ds(t * ROW_TILE + j * SUBLANES, SUBLANES), :],
                              o_hbm.at[b, step * n_tiles + t, :, j, :], sem.at[slot])
        for j in range(TILE_VREGS)]


def _ffn_kernel(natural_out, x_ref, mod_ref, ng_ref, wup_ref, cw_ref, cb_ref, wdn32_ref, o_ref, wdn_ref,
                tail_ref, *scatter_scratch):
    ts = ROW_TILE
    n_tiles = x_ref.shape[0] // ts
    f = wdn_ref.shape[0]
    fc = FFN_COL_CHUNK
    n_chunks = f // fc
    n_tail = FFN_CONV - 1

    if natural_out:
        obuf, sem = scatter_scratch
        b, i = pl.program_id(0), pl.program_id(1)
        n_steps = pl.num_programs(1)
        linear = b * n_steps + i
        slot = linear % 2

        def wait_step(lin):
            for t in range(n_tiles):
                for cp in _natural_order_scatter(obuf, o_ref, sem, lin // n_steps, lin % n_steps, lin % 2, t):
                    cp.wait()

        @pl.when(linear >= 2)
        def _():
            wait_step(linear - 2)

    @pl.when(_first_step())
    def _():
        wdn_ref[...] = wdn32_ref[...].astype(BF16)

    @pl.when(pl.program_id(1) == 0)
    def _():
        tail_ref[...] = jnp.zeros_like(tail_ref)

    def prologue(t):
        x = x_ref[t * ts:(t + 1) * ts, :]
        return x, (_rms(x, ng_ref[2:3, :]) * (1.0 + mod_ref[4:5, :]) + mod_ref[3:4, :]).astype(BF16)

    def epilogue(t, x, acc):
        out = x + mod_ref[5:6, :] * _rms(acc, ng_ref[3:4, :])
        if natural_out:
            obuf[slot, t * ts:(t + 1) * ts, :] = out
            for cp in _natural_order_scatter(obuf, o_ref, sem, b, i, slot, t):
                cp.start()
        else:
            o_ref[t * ts:(t + 1) * ts, :] = out

    def conv_cols(up, c0):
        cols = slice(c0, c0 + fc)
        tail = tail_ref[:, cols]
        tail_ref[:, cols] = up[ts - n_tail * SUBLANES:, :]
        y = cb_ref[:, cols] + _delay(up, tail, 2) * cw_ref[0:1, cols]
        y = y + _delay(up, tail, 1) * cw_ref[1:2, cols]
        return y + up * cw_ref[2:3, cols]

    items = [(t, c) for t in range(n_tiles) for c in range(n_chunks)]
    xs, hs, accs = {}, {}, {}
    xs[0], hs[0] = prologue(0)

    def up_pair(t, c):
        return (_dot(hs[t], wup_ref[:, c * fc:(c + 1) * fc]), _dot(hs[t], wup_ref[:, f + c * fc:f + (c + 1) * fc]))

    def down(t, c, act):
        part = _dot(act, wdn_ref[c * fc:(c + 1) * fc, :])
        accs[t] = part if c == 0 else accs[t] + part
        if c == n_chunks - 1:
            epilogue(t, xs.pop(t), accs.pop(t))

    ups = up_pair(0, 0)
    prev = None
    for idx, (t, c) in enumerate(items):
        if c == n_chunks - 4 and t + 1 < n_tiles:
            xs[t + 1], hs[t + 1] = prologue(t + 1)
        nxt = up_pair(*items[idx + 1]) if idx + 1 < len(items) else None
        if prev is not None:
            down(*prev)
        g = conv_cols(ups[0], c * fc)
        val = conv_cols(ups[1], f + c * fc)
        prev = (t, c, _gelu_times(g, val).astype(BF16))
        ups = nxt
    down(*prev)

    if natural_out:
        @pl.when(linear == pl.num_programs(0) * n_steps - 1)
        def _():
            @pl.when(linear >= 1)
            def _():
                wait_step(linear - 1)
            wait_step(linear)


def _ffn_layer(x, mod, ng, w_up, conv_w, conv_b, w_down, layer, natural_out):
    bsz, s, d = x.shape
    ts = ROW_TILE
    n_tiles = TILES_PER_STEP
    rows = pl.BlockSpec((None, n_tiles * ts, d), lambda b, i: (b, i, 0))
    out_tile, out_shape, scatter_scratch = rows, (bsz, s, d), []
    if natural_out:
        out_tile = pl.BlockSpec(memory_space=pl.ANY)
        out_shape = (bsz, s // ts, SUBLANES, TILE_VREGS, d)
        scatter_scratch = [pltpu.VMEM((2, n_tiles * ts, d), F32), pltpu.SemaphoreType.DMA((2,))]
    out = pl.pallas_call(
        functools.partial(_ffn_kernel, natural_out),
        grid=(bsz, s // (n_tiles * ts)),
        in_specs=[
            rows,
            pl.BlockSpec((None, None, 6, d), lambda b, i: (layer, b, 0, 0)),
            _layer_spec(ng.shape, layer),
            _layer_spec(w_up.shape, layer),
            _layer_spec(conv_w.shape, layer),
            _layer_spec(conv_b.shape, layer),
            _layer_spec(w_down.shape, layer),
        ],
        out_specs=out_tile,
        out_shape=jax.ShapeDtypeStruct(out_shape, F32),
        scratch_shapes=[
            pltpu.VMEM(w_down.shape[1:], BF16),
            pltpu.VMEM(((FFN_CONV - 1) * SUBLANES, w_up.shape[2]), F32),
        ] + scatter_scratch,
        compiler_params=_params("arbitrary", "arbitrary"),
        name="ffn_layer",
    )(x, mod, ng, w_up, conv_w, conv_b, w_down)
    return out.reshape(bsz, s, d)


_STREAM_LEVELS = tuple(TILE_VREGS << i for i in (2, 1, 0))
_VREG_LEVELS = tuple(TILE_VREGS >> i for i in range(1, 4))
_DIRECT_BLOCK = 4
_MASKED_STREAM_LEVELS = _STREAM_LEVELS[1:]
_EXCLUDE = 1e30
_GLA_PROJ_PIECES = 3
_GLA_OUT_PIECES = 2


def _tile_order_times(c):
    p = np.arange(c)
    return (p % SUBLANES) * (c // SUBLANES) + p // SUBLANES


def _gla_stream_masks(c):
    t = _tile_order_times(c)
    return np.stack([(t[:, None] // (2 * b) == t[None, :] // (2 * b)) for b in _MASKED_STREAM_LEVELS]
                    ).astype(np.float32)


def _vreg_level_rows(b):
    upper = [j for j in range(TILE_VREGS) if (j // b) % 2 == 1]
    lower = [j for j in range(TILE_VREGS) if (j // b) % 2 == 0]
    return upper, lower


def _gla_vreg_masks():
    out = []
    s = np.arange(SUBLANES)
    for b in _VREG_LEVELS:
        upper, lower = _vreg_level_rows(b)
        ju = np.repeat(np.array(upper), SUBLANES)[:, None]
        jl = np.repeat(np.array(lower), SUBLANES)[None, :]
        su = np.tile(s, len(upper))[:, None]
        sl = np.tile(s, len(lower))[None, :]
        out.append((su == sl) & (ju // (2 * b) == jl // (2 * b)))
    return np.stack(out).astype(np.float32)


def _cumsum_tile_order(g):
    rows = [_vreg(g, 0)]
    for j in range(1, TILE_VREGS):
        rows.append(rows[-1] + _vreg(g, j))
    total = rows[-1]
    sub = _sublane((SUBLANES, 1))
    incl = total
    for k in (1, 2, 4):
        incl = incl + jnp.where(sub >= k, pltpu.roll(incl, k, 0), 0.0)
    before = incl - total
    return jnp.concatenate([r + before for r in rows], axis=0)


def _gla_scores(qk, v32, gc, state, fill):
    c, dk = gc.shape
    vreg = _vreg
    q = qk[:, :dk] * (dk ** -0.5)
    k = qk[:, dk:]
    v = v32.astype(BF16)
    g_last = gc[c - 1:c, :]

    fill()
    o_inter = _dot((q * jnp.exp(gc)).astype(BF16), state.astype(BF16))

    sub = _sublane((SUBLANES, 1))
    g_end = vreg(gc, TILE_VREGS - 1)
    tile_rows = lambda a: jnp.concatenate([a] * TILE_VREGS, axis=0)
    p_stream = []
    for b in _STREAM_LEVELS:
        w = b // TILE_VREGS
        g_mid = None
        for first in range(0, SUBLANES, 2 * w):
            row = jnp.broadcast_to(g_end[first + w - 1:first + w, :], (SUBLANES, dk))
            g_mid = row if g_mid is None else jnp.where(sub >= first, row, g_mid)
        upper = (lax.shift_right_logical(sub, w.bit_length() - 1) & 1) == 1
        g_q = tile_rows(jnp.where(upper, g_mid, _EXCLUDE))
        g_k = tile_rows(jnp.where(upper, -_EXCLUDE, g_mid))
        q_b = (q * jnp.exp(gc - g_q)).astype(BF16)
        k_b = (k * jnp.exp(g_k - gc)).astype(BF16)
        p_stream.append(_dot_nt(q_b, k_b))
    fill()

    p_vreg = []
    for b in _VREG_LEVELS:
        upper_rows, _ = _vreg_level_rows(b)
        q_rows, k_rows = [], []
        for j in range(TILE_VREGS):
            jm = (j // (2 * b)) * (2 * b) + b - 1
            if j in upper_rows:
                q_rows.append(vreg(q, j) * jnp.exp(vreg(gc, j) - vreg(gc, jm)))
            elif j == jm:
                k_rows.append(vreg(k, j))
            else:
                k_rows.append(vreg(k, j) * jnp.exp(vreg(gc, jm) - vreg(gc, j)))
        p_vreg.append(_dot_nt(jnp.concatenate(q_rows, axis=0).astype(BF16),
                              jnp.concatenate(k_rows, axis=0).astype(BF16)))
    fill()

    k_rev = (k * jnp.exp(g_last - gc)).astype(BF16)
    upd = lax.dot_general(k_rev, v, (((0,), (0,)), ((), ())), preferred_element_type=F32)
    decay_rows = jnp.broadcast_to(jnp.exp(g_last), (dk, dk))
    decay_col = jnp.transpose(decay_rows)[:, 0:1]
    new_state = decay_col * state + upd

    direct = []
    for ju in range(TILE_VREGS):
        terms = vreg(v32, ju) * jnp.sum(vreg(q, ju) * vreg(k, ju), axis=-1, keepdims=True)
        for jl in range((ju // _DIRECT_BLOCK) * _DIRECT_BLOCK, ju):
            w = vreg(q, ju) * vreg(k, jl) * jnp.exp(vreg(gc, ju) - vreg(gc, jl))
            terms = terms + vreg(v32, jl) * jnp.sum(w, axis=-1, keepdims=True)
        direct.append(terms)
    return (o_inter, p_stream, p_vreg, direct), new_state


def _gla_combine(scores, v32, r, hn, sm_ref, vm_ref, fill):
    o, p_stream, p_vreg, direct = scores
    v = v32.astype(BF16)

    attn = None
    for b, p in zip(_STREAM_LEVELS, p_stream):
        if b in _MASKED_STREAM_LEVELS:
            p = p * sm_ref[_MASKED_STREAM_LEVELS.index(b)]
        attn = p if attn is None else attn + p
    o = o + _dot(attn.astype(BF16), v)
    fill()

    o_rows = [_vreg(o, j) + direct[j] for j in range(TILE_VREGS)]
    for li, (b, p) in enumerate(zip(_VREG_LEVELS, p_vreg)):
        upper_rows, _ = _vreg_level_rows(b)
        p = (p * vm_ref[li]).astype(BF16)
        v_low = jnp.concatenate(
            [v[j0 * SUBLANES:(j0 + b) * SUBLANES] for j0 in range(0, TILE_VREGS, 2 * b)], axis=0)
        o_up = _dot(p, v_low)
        if li == 1:
            fill()
        for idx, j in enumerate(upper_rows):
            o_rows[j] = o_rows[j] + _vreg(o_up, idx)
    o = jnp.concatenate(o_rows, axis=0)
    return (_rms(o, hn) * (r * jax.nn.sigmoid(r))).astype(BF16)


def _gla_kernel(x_ref, mod_ref, ng_ref, wmain_ref, wz_ref, wal_ref, bal_ref, hn_ref, sm_ref, vm_ref,
                wout32_ref, o_ref, wh_ref, wout_ref, state_ref):
    c = ROW_TILE
    d = x_ref.shape[1]
    n_tiles = x_ref.shape[0] // c
    dk, dv = state_ref.shape[1:]
    qk = GLA_HEADS * dk

    @pl.when(_first_step())
    def _():
        wout_ref[...] = wout32_ref[...].astype(BF16)
        for hh in range(GLA_HEADS):
            for dst, src, w in ((0, hh * dk, dk), (dk, qk + hh * dk, dk), (2 * dk, 2 * qk + hh * dv, dv),
                                (2 * dk + dv, 2 * qk + d + hh * dv, dv)):
                wh_ref[hh, :, dst:dst + w] = wmain_ref[:, src:src + w]

    @pl.when(pl.program_id(1) == 0)
    def _():
        state_ref[...] = jnp.zeros_like(state_ref)

    def prologue(t):
        x = x_ref[t * c:(t + 1) * c, :]
        h = (_rms(x, ng_ref[0:1, :]) * (1.0 + mod_ref[1:2, :]) + mod_ref[0:1, :]).astype(BF16)
        z = _dot(h, wz_ref[...])
        ga = _dot(z.astype(BF16), wal_ref[...]) + bal_ref[...]
        g = (jnp.minimum(ga, 0.0) - jnp.log1p(jnp.exp(-jnp.abs(ga)))) / GLA_TAU
        return x, h, _cumsum_tile_order(g)

    def epilogue(t, x, acc):
        o_ref[t * c:(t + 1) * c, :] = x + mod_ref[2:3, :] * _rms(acc, ng_ref[1:2, :])

    items = [(t, hh) for t in range(n_tiles) for hh in range(GLA_HEADS)]
    xs, hs, gcs, accs = {}, {}, {}, {}
    xs[0], hs[0], gcs[0] = prologue(0)

    def project_piece(t, hh, j):
        w = 2 * dk if j == 0 else dv
        c0 = 0 if j == 0 else 2 * dk + (j - 1) * dv
        return _dot(hs[t], wh_ref[hh, :, c0:c0 + w])

    def out_piece(t, hh, y, j):
        w = d // _GLA_OUT_PIECES
        part = _dot(y, wout_ref[hh * dv:(hh + 1) * dv, j * w:(j + 1) * w])
        accs[t, j] = part if hh == 0 else accs[t, j] + part
        if hh == GLA_HEADS - 1 and j == _GLA_OUT_PIECES - 1:
            acc = jnp.concatenate([accs.pop((t, jj)) for jj in range(_GLA_OUT_PIECES)], axis=1)
            epilogue(t, xs.pop(t), acc)

    cur = [project_piece(0, 0, j) for j in range(_GLA_PROJ_PIECES)]
    prev = None
    for idx, (t, hh) in enumerate(items):
        if hh == 1 and t + 1 < n_tiles:
            xs[t + 1], hs[t + 1], gcs[t + 1] = prologue(t + 1)
        nxt, pending = [], []
        if idx + 1 < len(items):
            pending += [lambda j=j, idx=idx, nxt=nxt: nxt.append(project_piece(*items[idx + 1], j))
                        for j in range(_GLA_PROJ_PIECES)]
        if prev is not None:
            pending += [functools.partial(out_piece, *prev, j) for j in range(_GLA_OUT_PIECES)]
        pending.reverse()
        fill = lambda pending=pending: pending.pop()() if pending else None
        qk, v32, r = cur
        scores, state_ref[hh] = _gla_scores(qk, v32, gcs[t][:, hh * dk:(hh + 1) * dk], state_ref[hh], fill)
        prev = (t, hh, _gla_combine(scores, v32, r, hn_ref[...], sm_ref, vm_ref, fill))
        while pending:
            fill()
        cur = nxt
    for j in range(_GLA_OUT_PIECES):
        out_piece(*prev, j)


def _gla_layer(x, mod, ng, params, layer, mixer_index):
    w_in, w_alpha, b_alpha, head_norm, w_out = params
    bsz, s, d = x.shape
    c = ROW_TILE
    rank, qk = w_alpha.shape[1:]
    dk, dv = qk // GLA_HEADS, d // GLA_HEADS
    n_main = 2 * qk + 2 * d
    w_main = w_in[mixer_index, :, :n_main].astype(BF16)
    w_z = jnp.pad(w_in[mixer_index, :, n_main:], ((0, 0), (0, GLA_RANK_PAD - rank))).astype(BF16)
    w_al = jnp.pad(w_alpha[mixer_index], ((0, GLA_RANK_PAD - rank), (0, 0))).astype(BF16)
    stream_masks = jnp.asarray(_gla_stream_masks(c), F32)
    vreg_masks = jnp.asarray(_gla_vreg_masks(), F32)
    n_tiles = TILES_PER_STEP
    tile = pl.BlockSpec((None, n_tiles * c, d), lambda b, i: (b, i, 0))
    row = lambda v: v[:, None, :]
    mix_spec = lambda p: _layer_spec(p.shape, mixer_index)
    return pl.pallas_call(
        _gla_kernel,
        grid=(bsz, s // (n_tiles * c)),
        in_specs=[
            tile,
            pl.BlockSpec((None, None, 6, d), lambda b, i: (layer, b, 0, 0)),
            _layer_spec(ng.shape, layer),
            _const_spec(w_main.shape),
            _const_spec(w_z.shape),
            _const_spec(w_al.shape),
            mix_spec(row(b_alpha)),
            mix_spec(row(head_norm)),
            _const_spec(stream_masks.shape),
            _const_spec(vreg_masks.shape),
            mix_spec(w_out),
        ],
        out_specs=tile,
        out_shape=jax.ShapeDtypeStruct(x.shape, F32),
        scratch_shapes=[
            pltpu.VMEM((GLA_HEADS, d, 2 * dk + 2 * dv), BF16),
            pltpu.VMEM(w_out.shape[1:], BF16),
            pltpu.VMEM((GLA_HEADS, dk, dv), F32),
        ],
        compiler_params=_params("arbitrary", "arbitrary"),
        name="gla_layer",
    )(x, mod, ng, w_main, w_z, w_al, row(b_alpha), row(head_norm), stream_masks, vreg_masks, w_out)


def kernel(x, c, ada_w, ada_b, norm_g, ffn_w_up, ffn_conv_w, ffn_conv_b, ffn_w_down, rg_w_in, rg_conv_w,
           rg_conv_b, rg_wa, rg_ba, rg_wx, rg_bx, rg_lambda, rg_w_out, gla_w_in, gla_w_alpha, gla_b_alpha,
           gla_norm_g, gla_w_out):
    depth = ada_w.shape[0]
    assert depth >= 1 and x.shape[1] % (ROW_TILE * TILES_PER_STEP) == 0
    mod = _ada_modulation(c, ada_w, ada_b)
    ffn_w_up_bf = ffn_w_up.astype(BF16)
    ffn_conv_b3 = ffn_conv_b[:, None, :]
    rg_params = (rg_w_in, rg_conv_w, rg_conv_b, rg_wa, rg_ba, rg_wx, rg_bx, rg_lambda, rg_w_out)
    gla_params = (gla_w_in, gla_w_alpha, gla_b_alpha, gla_norm_g, gla_w_out)
    for i in range(depth):
        j = i // 2
        if i % 2 == 0:
            x = _rglru_layer(x, mod, norm_g, rg_params, i, j, natural_in=(i == 0))
        else:
            x = _gla_layer(x, mod, norm_g, gla_params, i, j)
        x = _ffn_layer(x, mod, norm_g, ffn_w_up_bf, ffn_conv_w, ffn_conv_b3, ffn_w_down, i,
                       natural_out=(i == depth - 1))
    return x
```

```python
import functools
import math

import jax
import jax.numpy as jnp
import numpy as np
from jax import lax
from jax.experimental import pallas as pl
from jax.experimental.pallas import tpu as pltpu

F32 = jnp.float32
BF16 = jnp.bfloat16

EPS = 1e-6
RG_BLOCKS = 4
RG_C = 8.0
RG_CONV = 4
FFN_CONV = 3
GLA_HEADS = 4
GLA_TAU = 16.0
GLA_RANK_PAD = 128
SUBLANES = 8

ROW_TILE = 256
TILE_VREGS = ROW_TILE // SUBLANES
TILES_PER_STEP = 4
ADA_COL_TILE = 1024
FFN_COL_CHUNK = 256
VMEM_LIMIT = 56 * 1024 * 1024

_GELU_C0 = 2.0 * math.sqrt(2.0 / math.pi) * math.log2(math.e)
_GELU_C1 = _GELU_C0 * 0.044715


def _dot(a, b):
    return jnp.dot(a, b, preferred_element_type=F32)


def _dot_nt(a, b):
    return lax.dot_general(a, b, (((1,), (1,)), ((), ())), preferred_element_type=F32)


def _rms(x, g):
    return x * lax.rsqrt(jnp.mean(x * x, axis=-1, keepdims=True) + EPS) * g


def _gelu_times(x, v):
    neg_2z_log2e = x * (-_GELU_C0 - _GELU_C1 * (x * x))
    return (x * v) / (1.0 + jnp.exp2(neg_2z_log2e))


def _sublane(shape):
    return lax.broadcasted_iota(jnp.int32, shape, 0) % SUBLANES


def _vreg(a, j):
    return a[j * SUBLANES:(j + 1) * SUBLANES]


def _load_tile_order(ref):
    return jnp.concatenate([ref[:, j, :] for j in range(TILE_VREGS)], axis=0)


def _store_natural_order(ref, val):
    for j in range(TILE_VREGS):
        ref[:, j, :] = _vreg(val, j)


def _delay(x, tail, k):
    ts = x.shape[0]
    n_tail = tail.shape[0] // SUBLANES
    last = _sublane((SUBLANES, 1)) == SUBLANES - 1
    head = []
    for j in range(k):
        cur = _vreg(x, TILE_VREGS - k + j)
        prv = _vreg(tail, n_tail - k + j)
        head.append(pltpu.roll(jnp.where(last, prv, cur), 1, 0))
    return jnp.concatenate(head + [x[:ts - k * SUBLANES]], axis=0)


def _params(*sem):
    return pltpu.CompilerParams(dimension_semantics=sem, vmem_limit_bytes=VMEM_LIMIT)


def _const_spec(shape):
    n = len(shape)
    return pl.BlockSpec(shape, lambda *_: (0,) * n, pipeline_mode=pl.Buffered(1))


def _layer_spec(shape, layer):
    n = len(shape) - 1
    return pl.BlockSpec((None,) + tuple(shape[1:]), lambda *_: (layer,) + (0,) * n,
                        pipeline_mode=pl.Buffered(1))


def _ada_kernel(c_ref, w_ref, b_ref, o_ref):
    c = c_ref[...]
    c_act = (c * jax.nn.sigmoid(c)).astype(BF16)
    o_ref[...] = _dot(c_act, w_ref[...].astype(BF16)) + b_ref[...]


def _ada_modulation(c, ada_w, ada_b):
    depth, d, n = ada_w.shape
    bsz = c.shape[0]
    tn = ADA_COL_TILE
    out = pl.pallas_call(
        _ada_kernel,
        grid=(depth, n // tn),
        in_specs=[
            pl.BlockSpec((bsz, d), lambda l, j: (0, 0)),
            pl.BlockSpec((None, d, tn), lambda l, j: (l, 0, j)),
            pl.BlockSpec((None, 1, tn), lambda l, j: (l, 0, j)),
        ],
        out_specs=pl.BlockSpec((None, bsz, tn), lambda l, j: (l, 0, j)),
        out_shape=jax.ShapeDtypeStruct((depth, bsz, n), F32),
        compiler_params=_params("arbitrary", "arbitrary"),
        name="ada_mod",
    )(c, ada_w, ada_b.reshape(depth, 1, n))
    return out.reshape(depth, bsz, 6, d)


def _stream_scan(a, u, carry):
    hl = [_vreg(u, 0)]
    al = [_vreg(a, 0)]
    for j in range(1, TILE_VREGS):
        hl.append(_vreg(a, j) * hl[-1] + _vreg(u, j))
        al.append(_vreg(a, j) * al[-1])
    sub = _sublane((SUBLANES, 1))
    ea, eh = al[-1], hl[-1]
    for k in (1, 2, 4):
        keep = sub >= k
        ea_sh = jnp.where(keep, pltpu.roll(ea, k, 0), 1.0)
        eh_sh = jnp.where(keep, pltpu.roll(eh, k, 0), 0.0)
        eh = ea * eh_sh + eh
        ea = ea * ea_sh
    after = ea * carry + eh
    before = jnp.where(sub == 0, carry, pltpu.roll(after, 1, 0))
    h = jnp.concatenate([hl[j] + al[j] * before for j in range(TILE_VREGS)], axis=0)
    return h, after


def _first_step():
    return (pl.program_id(0) == 0) & (pl.program_id(1) == 0)


def _tile_order_gather(x_hbm, buf, sem, b, step, slot):
    n_tiles = buf.shape[1] // ROW_TILE
    return [
        pltpu.make_async_copy(x_hbm.at[b, step * n_tiles + t, :, j, :],
                              buf.at[slot, pl.ds(t * ROW_TILE + j * SUBLANES, SUBLANES), :], sem.at[slot])
        for t in range(n_tiles) for j in range(TILE_VREGS)]


def _rglru_kernel(natural_in, x_ref, mod_ref, ng_ref, win32_ref, cw_ref, cb_ref, ba_ref, bx_ref, lam_ref,
                  wa32_ref, wx32_ref, wout32_ref, o_ref, win_ref, wg_ref, wout_ref, tail_ref, hc_ref, y_ref,
                  *gather_scratch):
    ts = ROW_TILE
    d = o_ref.shape[1]
    n_tiles = o_ref.shape[0] // ts
    blk = d // RG_BLOCKS
    n_tail = RG_CONV - 1

    if natural_in:
        xbuf, sem = gather_scratch
        b, i = pl.program_id(0), pl.program_id(1)
        n_steps = pl.num_programs(1)
        linear = b * n_steps + i
        slot = linear % 2

        @pl.when(linear == 0)
        def _():
            for cp in _tile_order_gather(x_ref, xbuf, sem, b, i, slot):
                cp.start()

        @pl.when(linear + 1 < pl.num_programs(0) * n_steps)
        def _():
            wrap = i + 1 == n_steps
            for cp in _tile_order_gather(x_ref, xbuf, sem, jnp.where(wrap, b + 1, b), jnp.where(wrap, 0, i + 1),
                                         1 - slot):
                cp.start()

        for cp in _tile_order_gather(x_ref, xbuf, sem, b, i, slot):
            cp.wait()

    @pl.when(_first_step())
    def _():
        win_ref[...] = win32_ref[...].astype(BF16)
        wout_ref[...] = wout32_ref[...].astype(BF16)
        for g in range(RG_BLOCKS):
            wg_ref[g, :, :blk] = wa32_ref[g].astype(BF16)
            wg_ref[g, :, blk:] = wx32_ref[g].astype(BF16)

    @pl.when(pl.program_id(1) == 0)
    def _():
        tail_ref[...] = jnp.zeros_like(tail_ref)
        hc_ref[...] = jnp.zeros_like(hc_ref)

    def prologue(t):
        x = xbuf[slot, t * ts:(t + 1) * ts, :] if natural_in else x_ref[t * ts:(t + 1) * ts, :]
        return x, (_rms(x, ng_ref[0:1, :]) * (1.0 + mod_ref[1:2, :]) + mod_ref[0:1, :]).astype(BF16)

    def epilogue(t, x, acc):
        o_ref[t * ts:(t + 1) * ts, :] = x + mod_ref[2:3, :] * _rms(acc, ng_ref[1:2, :])

    def project(t, g):
        return (_dot(hs[t], win_ref[:, g * blk:(g + 1) * blk]),
                _dot(hs[t], win_ref[:, d + g * blk:d + (g + 1) * blk]))

    def mix(g, gate_br, xb):
        cols = slice(g * blk, (g + 1) * blk)
        tail = tail_ref[:, cols]
        tail_ref[:, cols] = xb[ts - n_tail * SUBLANES:, :]
        xc = cb_ref[:, cols] + _delay(xb, tail, 3) * cw_ref[0:1, cols]
        xc = xc + _delay(xb, tail, 2) * cw_ref[1:2, cols]
        xc = xc + _delay(xb, tail, 1) * cw_ref[2:3, cols]
        xc = xc + xb * cw_ref[3:4, cols]
        p = _dot(xc.astype(BF16), wg_ref[g])
        r = jax.nn.sigmoid(p[:, :blk] + ba_ref[:, cols])
        i_g = jax.nn.sigmoid(p[:, blk:] + bx_ref[:, cols])
        nl = -lam_ref[:, cols]
        softplus_nl = jnp.maximum(nl, 0.0) + jnp.log1p(jnp.exp(-jnp.abs(nl)))
        log_a = r * ((-RG_C) * softplus_nl)
        a = jnp.exp(log_a)
        m2 = jnp.tanh(log_a) * (-1.0 - a * a)
        u = jnp.where(m2 > 0.0, m2 * lax.rsqrt(m2), 0.0) * (i_g * xc)
        hs_g, after = _stream_scan(a, u, hc_ref[SUBLANES - 1:SUBLANES, cols])
        hc_ref[:, cols] = after
        return _gelu_times(gate_br, hs_g).astype(BF16)

    items = [(t, g) for t in range(n_tiles) for g in range(RG_BLOCKS)]
    xs, hs = {}, {}
    xs[0], hs[0] = prologue(0)

    def out_project(t):
        epilogue(t, xs.pop(t), _dot(y_ref[t % 2], wout_ref[...]))

    cur = project(0, 0)
    due = None
    for idx, (t, g) in enumerate(items):
        if g == 1 and t + 1 < n_tiles:
            xs[t + 1], hs[t + 1] = prologue(t + 1)
        nxt = project(*items[idx + 1]) if idx + 1 < len(items) else None
        if due is not None:
            out_project(due)
            due = None
        y_ref[t % 2, :, g * blk:(g + 1) * blk] = mix(g, *cur)
        if g == RG_BLOCKS - 1:
            due = t
        cur = nxt
    out_project(due)


def _rglru_layer(x, mod, ng, params, layer, mixer_index, natural_in):
    w_in, conv_w, conv_b, wa, ba, wx, bx, lam, w_out = params
    bsz, s, d = x.shape
    ts = ROW_TILE
    n_tiles = TILES_PER_STEP
    blk = d // RG_BLOCKS
    tile = pl.BlockSpec((None, n_tiles * ts, d), lambda b, i: (b, i, 0))
    x_tile, gather_scratch = tile, []
    if natural_in:
        x = x.reshape(bsz, s // ts, SUBLANES, TILE_VREGS, d)
        x_tile = pl.BlockSpec(memory_space=pl.ANY)
        gather_scratch = [pltpu.VMEM((2, n_tiles * ts, d), F32), pltpu.SemaphoreType.DMA((2,))]
    row = lambda v: v[:, None, :]
    mix_spec = lambda p: _layer_spec(p.shape, mixer_index)
    return pl.pallas_call(
        functools.partial(_rglru_kernel, natural_in),
        grid=(bsz, s // (n_tiles * ts)),
        in_specs=[
            x_tile,
            pl.BlockSpec((None, None, 6, d), lambda b, i: (layer, b, 0, 0)),
            _layer_spec(ng.shape, layer),
            mix_spec(w_in), mix_spec(conv_w), mix_spec(row(conv_b)), mix_spec(row(ba)), mix_spec(row(bx)),
            mix_spec(row(lam)), mix_spec(wa), mix_spec(wx), mix_spec(w_out),
        ],
        out_specs=tile,
        out_shape=jax.ShapeDtypeStruct((bsz, s, d), F32),
        scratch_shapes=[
            pltpu.VMEM(w_in.shape[1:], BF16),
            pltpu.VMEM((RG_BLOCKS, blk, 2 * blk), BF16),
            pltpu.VMEM(w_out.shape[1:], BF16),
            pltpu.VMEM(((RG_CONV - 1) * SUBLANES, d), F32),
            pltpu.VMEM((SUBLANES, d), F32),
            pltpu.VMEM((2, ts, d), BF16),
        ] + gather_scratch,
        compiler_params=_params("arbitrary", "arbitrary"),
        name="rglru_layer",
    )(x, mod, ng, w_in, conv_w, row(conv_b), row(ba), row(bx), row(lam), wa, wx, w_out)


def _ffn_kernel(natural_out, x_ref, mod_ref, ng_ref, wup_ref, cw_ref, cb_ref, wdn32_ref, o_ref, wdn_ref,
                tail_ref, act_ref):
    ts = ROW_TILE
    n_tiles = x_ref.shape[0] // ts
    f = wdn_ref.shape[0]
    fc = FFN_COL_CHUNK
    n_chunks = f // fc
    n_tail = FFN_CONV - 1

    @pl.when(_first_step())
    def _():
        wdn_ref[...] = wdn32_ref[...].astype(BF16)

    @pl.when(pl.program_id(1) == 0)
    def _():
        tail_ref[...] = jnp.zeros_like(tail_ref)

    def prologue(t):
        x = x_ref[t * ts:(t + 1) * ts, :]
        return x, (_rms(x, ng_ref[2:3, :]) * (1.0 + mod_ref[4:5, :]) + mod_ref[3:4, :]).astype(BF16)

    def epilogue(t, x, acc):
        out = x + mod_ref[5:6, :] * _rms(acc, ng_ref[3:4, :])
        if natural_out:
            _store_natural_order(o_ref.at[t], out)
        else:
            o_ref[t * ts:(t + 1) * ts, :] = out

    def conv_cols(up, c0):
        cols = slice(c0, c0 + fc)
        tail = tail_ref[:, cols]
        tail_ref[:, cols] = up[ts - n_tail * SUBLANES:, :]
        y = cb_ref[:, cols] + _delay(up, tail, 2) * cw_ref[0:1, cols]
        y = y + _delay(up, tail, 1) * cw_ref[1:2, cols]
        return y + up * cw_ref[2:3, cols]

    items = [(t, c) for t in range(n_tiles) for c in range(n_chunks)]
    xs, hs = {}, {}
    xs[0], hs[0] = prologue(0)

    def up_pair(t, c):
        return (_dot(hs[t], wup_ref[:, c * fc:(c + 1) * fc]), _dot(hs[t], wup_ref[:, f + c * fc:f + (c + 1) * fc]))

    def down(t):
        epilogue(t, xs.pop(t), _dot(act_ref[t % 2], wdn_ref[...]))

    ups = up_pair(0, 0)
    due = None
    for idx, (t, c) in enumerate(items):
        if c == n_chunks - 4 and t + 1 < n_tiles:
            xs[t + 1], hs[t + 1] = prologue(t + 1)
        nxt = up_pair(*items[idx + 1]) if idx + 1 < len(items) else None
        if due is not None:
            down(due)
            due = None
        g = conv_cols(ups[0], c * fc)
        val = conv_cols(ups[1], f + c * fc)
        act_ref[t % 2, :, c * fc:(c + 1) * fc] = _gelu_times(g, val).astype(BF16)
        if c == n_chunks - 1:
            due = t
        ups = nxt
    down(due)


def _ffn_layer(x, mod, ng, w_up, conv_w, conv_b, w_down, layer, natural_out):
    bsz, s, d = x.shape
    ts = ROW_TILE
    n_tiles = TILES_PER_STEP
    rows = pl.BlockSpec((None, n_tiles * ts, d), lambda b, i: (b, i, 0))
    out_tile, out_shape = rows, (bsz, s, d)
    if natural_out:
        out_tile = pl.BlockSpec((None, n_tiles, SUBLANES, TILE_VREGS, d), lambda b, i: (b, i, 0, 0, 0))
        out_shape = (bsz, s // ts, SUBLANES, TILE_VREGS, d)
    out = pl.pallas_call(
        functools.partial(_ffn_kernel, natural_out),
        grid=(bsz, s // (n_tiles * ts)),
        in_specs=[
            rows,
            pl.BlockSpec((None, None, 6, d), lambda b, i: (layer, b, 0, 0)),
            _layer_spec(ng.shape, layer),
            _layer_spec(w_up.shape, layer),
            _layer_spec(conv_w.shape, layer),
            _layer_spec(conv_b.shape, layer),
            _layer_spec(w_down.shape, layer),
        ],
        out_specs=out_tile,
        out_shape=jax.ShapeDtypeStruct(out_shape, F32),
        scratch_shapes=[
            pltpu.VMEM(w_down.shape[1:], BF16),
            pltpu.VMEM(((FFN_CONV - 1) * SUBLANES, w_up.shape[2]), F32),
            pltpu.VMEM((2, ts, w_down.shape[1]), BF16),
        ],
        compiler_params=_params("arbitrary", "arbitrary"),
        name="ffn_layer",
    )(x, mod, ng, w_up, conv_w, conv_b, w_down)
    return out.reshape(bsz, s, d)


_STREAM_LEVELS = tuple(TILE_VREGS << i for i in (2, 1, 0))
_VREG_LEVELS = tuple(TILE_VREGS >> i for i in range(1, 4))
_DIRECT_BLOCK = 4
_MASKED_STREAM_LEVELS = _STREAM_LEVELS[1:]
_EXCLUDE = 1e30
_GLA_PROJ_PIECES = 3
_GLA_OUT_PIECES = 2


def _tile_order_times(c):
    p = np.arange(c)
    return (p % SUBLANES) * (c // SUBLANES) + p // SUBLANES


def _gla_stream_masks(c):
    t = _tile_order_times(c)
    return np.stack([(t[:, None] // (2 * b) == t[None, :] // (2 * b)) for b in _MASKED_STREAM_LEVELS]
                    ).astype(np.float32)


def _vreg_level_rows(b):
    upper = [j for j in range(TILE_VREGS) if (j // b) % 2 == 1]
    lower = [j for j in range(TILE_VREGS) if (j // b) % 2 == 0]
    return upper, lower


def _gla_vreg_masks():
    out = []
    s = np.arange(SUBLANES)
    for b in _VREG_LEVELS:
        upper, lower = _vreg_level_rows(b)
        ju = np.repeat(np.array(upper), SUBLANES)[:, None]
        jl = np.repeat(np.array(lower), SUBLANES)[None, :]
        su = np.tile(s, len(upper))[:, None]
        sl = np.tile(s, len(lower))[None, :]
        out.append((su == sl) & (ju // (2 * b) == jl // (2 * b)))
    return np.stack(out).astype(np.float32)


def _cumsum_tile_order(g):
    rows = [_vreg(g, 0)]
    for j in range(1, TILE_VREGS):
        rows.append(rows[-1] + _vreg(g, j))
    total = rows[-1]
    sub = _sublane((SUBLANES, 1))
    incl = total
    for k in (1, 2, 4):
        incl = incl + jnp.where(sub >= k, pltpu.roll(incl, k, 0), 0.0)
    before = incl - total
    return jnp.concatenate([r + before for r in rows], axis=0)


def _gla_scores(qk, v32, gc, state, fill):
    c, dk = gc.shape
    vreg = _vreg
    q = qk[:, :dk] * (dk ** -0.5)
    k = qk[:, dk:]
    v = v32.astype(BF16)
    g_last = gc[c - 1:c, :]

    fill()
    o_inter = _dot((q * jnp.exp(gc)).astype(BF16), state.astype(BF16))

    sub = _sublane((SUBLANES, 1))
    g_end = vreg(gc, TILE_VREGS - 1)
    tile_rows = lambda a: jnp.concatenate([a] * TILE_VREGS, axis=0)
    p_stream = []
    for b in _STREAM_LEVELS:
        w = b // TILE_VREGS
        g_mid = None
        for first in range(0, SUBLANES, 2 * w):
            row = jnp.broadcast_to(g_end[first + w - 1:first + w, :], (SUBLANES, dk))
            g_mid = row if g_mid is None else jnp.where(sub >= first, row, g_mid)
        upper = (lax.shift_right_logical(sub, w.bit_length() - 1) & 1) == 1
        g_q = tile_rows(jnp.where(upper, g_mid, _EXCLUDE))
        g_k = tile_rows(jnp.where(upper, -_EXCLUDE, g_mid))
        q_b = (q * jnp.exp(gc - g_q)).astype(BF16)
        k_b = (k * jnp.exp(g_k - gc)).astype(BF16)
        p_stream.append(_dot_nt(q_b, k_b))
    fill()

    p_vreg = []
    for b in _VREG_LEVELS:
        upper_rows, _ = _vreg_level_rows(b)
        q_rows, k_rows = [], []
        for j in range(TILE_VREGS):
            jm = (j // (2 * b)) * (2 * b) + b - 1
            if j in upper_rows:
                q_rows.append(vreg(q, j) * jnp.exp(vreg(gc, j) - vreg(gc, jm)))
            elif j == jm:
                k_rows.append(vreg(k, j))
            else:
                k_rows.append(vreg(k, j) * jnp.exp(vreg(gc, jm) - vreg(gc, j)))
        p_vreg.append(_dot_nt(jnp.concatenate(q_rows, axis=0).astype(BF16),
                              jnp.concatenate(k_rows, axis=0).astype(BF16)))
    fill()

    k_rev = (k * jnp.exp(g_last - gc)).astype(BF16)
    upd = lax.dot_general(k_rev, v, (((0,), (0,)), ((), ())), preferred_element_type=F32)
    decay_rows = jnp.broadcast_to(jnp.exp(g_last), (dk, dk))
    decay_col = jnp.transpose(decay_rows)[:, 0:1]
    new_state = decay_col * state + upd

    direct = []
    for ju in range(TILE_VREGS):
        terms = vreg(v32, ju) * jnp.sum(vreg(q, ju) * vreg(k, ju), axis=-1, keepdims=True)
        for jl in range((ju // _DIRECT_BLOCK) * _DIRECT_BLOCK, ju):
            w = vreg(q, ju) * vreg(k, jl) * jnp.exp(vreg(gc, ju) - vreg(gc, jl))
            terms = terms + vreg(v32, jl) * jnp.sum(w, axis=-1, keepdims=True)
        direct.append(terms)
    return (o_inter, p_stream, p_vreg, direct), new_state


def _gla_combine(scores, v32, r, hn, sm_ref, vm_ref, fill):
    o, p_stream, p_vreg, direct = scores
    v = v32.astype(BF16)

    attn = None
    for b, p in zip(_STREAM_LEVELS, p_stream):
        if b in _MASKED_STREAM_LEVELS:
            p = p * sm_ref[_MASKED_STREAM_LEVELS.index(b)]
        attn = p if attn is None else attn + p
    o = o + _dot(attn.astype(BF16), v)
    fill()

    o_rows = [_vreg(o, j) + direct[j] for j in range(TILE_VREGS)]
    for li, (b, p) in enumerate(zip(_VREG_LEVELS, p_vreg)):
        upper_rows, _ = _vreg_level_rows(b)
        p = (p * vm_ref[li]).astype(BF16)
        v_low = jnp.concatenate(
            [v[j0 * SUBLANES:(j0 + b) * SUBLANES] for j0 in range(0, TILE_VREGS, 2 * b)], axis=0)
        o_up = _dot(p, v_low)
        if li == 1:
            fill()
        for idx, j in enumerate(upper_rows):
            o_rows[j] = o_rows[j] + _vreg(o_up, idx)
    o = jnp.concatenate(o_rows, axis=0)
    return (_rms(o, hn) * (r * jax.nn.sigmoid(r))).astype(BF16)


def _gla_kernel(x_ref, mod_ref, ng_ref, win32_ref, wz_ref, wal_ref, bal_ref, hn_ref, sm_ref, vm_ref,
                wout32_ref, o_ref, wh_ref, wout_ref, state_ref, y_ref):
    c = ROW_TILE
    d = x_ref.shape[1]
    n_tiles = x_ref.shape[0] // c
    dk, dv = state_ref.shape[1:]
    qk = GLA_HEADS * dk

    @pl.when(_first_step())
    def _():
        wout_ref[...] = wout32_ref[...].astype(BF16)
        for hh in range(GLA_HEADS):
            for dst, src, w in ((0, hh * dk, dk), (dk, qk + hh * dk, dk), (2 * dk, 2 * qk + hh * dv, dv),
                                (2 * dk + dv, 2 * qk + d + hh * dv, dv)):
                wh_ref[hh, :, dst:dst + w] = win32_ref[:, src:src + w].astype(BF16)

    @pl.when(pl.program_id(1) == 0)
    def _():
        state_ref[...] = jnp.zeros_like(state_ref)

    def prologue(t):
        x = x_ref[t * c:(t + 1) * c, :]
        h = (_rms(x, ng_ref[0:1, :]) * (1.0 + mod_ref[1:2, :]) + mod_ref[0:1, :]).astype(BF16)
        z = _dot(h, wz_ref[...])
        ga = _dot(z.astype(BF16), wal_ref[...]) + bal_ref[...]
        g = (jnp.minimum(ga, 0.0) - jnp.log1p(jnp.exp(-jnp.abs(ga)))) / GLA_TAU
        return x, h, _cumsum_tile_order(g)

    def epilogue(t, x, acc):
        o_ref[t * c:(t + 1) * c, :] = x + mod_ref[2:3, :] * _rms(acc, ng_ref[1:2, :])

    items = [(t, hh) for t in range(n_tiles) for hh in range(GLA_HEADS)]
    xs, hs, gcs = {}, {}, {}
    xs[0], hs[0], gcs[0] = prologue(0)

    def project_piece(t, hh, j):
        w = 2 * dk if j == 0 else dv
        c0 = 0 if j == 0 else 2 * dk + (j - 1) * dv
        return _dot(hs[t], wh_ref[hh, :, c0:c0 + w])

    def out_piece(t, parts, j):
        w = d // _GLA_OUT_PIECES
        parts.append(_dot(y_ref[t % 2], wout_ref[:, j * w:(j + 1) * w]))
        if j == _GLA_OUT_PIECES - 1:
            epilogue(t, xs.pop(t), jnp.concatenate(parts, axis=1))

    cur = [project_piece(0, 0, j) for j in range(_GLA_PROJ_PIECES)]
    due = None
    for idx, (t, hh) in enumerate(items):
        if hh == 1 and t + 1 < n_tiles:
            xs[t + 1], hs[t + 1], gcs[t + 1] = prologue(t + 1)
        nxt, pending = [], []
        if idx + 1 < len(items):
            pending += [lambda j=j, idx=idx, nxt=nxt: nxt.append(project_piece(*items[idx + 1], j))
                        for j in range(_GLA_PROJ_PIECES)]
        if due is not None:
            parts = []
            pending += [functools.partial(out_piece, due, parts, j) for j in range(_GLA_OUT_PIECES)]
            due = None
        pending.reverse()
        fill = lambda pending=pending: pending.pop()() if pending else None
        qk, v32, r = cur
        scores, state_ref[hh] = _gla_scores(qk, v32, gcs[t][:, hh * dk:(hh + 1) * dk], state_ref[hh], fill)
        y_ref[t % 2, :, hh * dv:(hh + 1) * dv] = _gla_combine(scores, v32, r, hn_ref[...], sm_ref, vm_ref, fill)
        while pending:
            fill()
        if hh == GLA_HEADS - 1:
            due = t
        cur = nxt
    parts = []
    for j in range(_GLA_OUT_PIECES):
        out_piece(due, parts, j)


def _gla_layer(x, mod, ng, params, layer, mixer_index):
    w_in, w_alpha, b_alpha, head_norm, w_out = params
    bsz, s, d = x.shape
    c = ROW_TILE
    rank, qk = w_alpha.shape[1:]
    dk, dv = qk // GLA_HEADS, d // GLA_HEADS
    n_main = 2 * qk + 2 * d
    w_z = jnp.pad(w_in[mixer_index, :, n_main:], ((0, 0), (0, GLA_RANK_PAD - rank))).astype(BF16)
    w_al = jnp.pad(w_alpha[mixer_index], ((0, GLA_RANK_PAD - rank), (0, 0))).astype(BF16)
    stream_masks = jnp.asarray(_gla_stream_masks(c), F32)
    vreg_masks = jnp.asarray(_gla_vreg_masks(), F32)
    n_tiles = TILES_PER_STEP
    tile = pl.BlockSpec((None, n_tiles * c, d), lambda b, i: (b, i, 0))
    row = lambda v: v[:, None, :]
    mix_spec = lambda p: _layer_spec(p.shape, mixer_index)
    return pl.pallas_call(
        _gla_kernel,
        grid=(bsz, s // (n_tiles * c)),
        in_specs=[
            tile,
            pl.BlockSpec((None, None, 6, d), lambda b, i: (layer, b, 0, 0)),
            _layer_spec(ng.shape, layer),
            mix_spec(w_in),
            _const_spec(w_z.shape),
            _const_spec(w_al.shape),
            mix_spec(row(b_alpha)),
            mix_spec(row(head_norm)),
            _const_spec(stream_masks.shape),
            _const_spec(vreg_masks.shape),
            mix_spec(w_out),
        ],
        out_specs=tile,
        out_shape=jax.ShapeDtypeStruct(x.shape, F32),
        scratch_shapes=[
            pltpu.VMEM((GLA_HEADS, d, 2 * dk + 2 * dv), BF16),
            pltpu.VMEM(w_out.shape[1:], BF16),
            pltpu.VMEM((GLA_HEADS, dk, dv), F32),
            pltpu.VMEM((2, c, d), BF16),
        ],
        compiler_params=_params("arbitrary", "arbitrary"),
        name="gla_layer",
    )(x, mod, ng, w_in, w_z, w_al, row(b_alpha), row(head_norm), stream_masks, vreg_masks, w_out)


def kernel(x, c, ada_w, ada_b, norm_g, ffn_w_up, ffn_conv_w, ffn_conv_b, ffn_w_down, rg_w_in, rg_conv_w,
           rg_conv_b, rg_wa, rg_ba, rg_wx, rg_bx, rg_lambda, rg_w_out, gla_w_in, gla_w_alpha, gla_b_alpha,
           gla_norm_g, gla_w_out):
    depth = ada_w.shape[0]
    assert depth >= 1 and x.shape[1] % (ROW_TILE * TILES_PER_STEP) == 0
    mod = _ada_modulation(c, ada_w, ada_b)
    ffn_w_up_bf = ffn_w_up.astype(BF16)
    ffn_conv_b3 = ffn_conv_b[:, None, :]
    rg_params = (rg_w_in, rg_conv_w, rg_conv_b, rg_wa, rg_ba, rg_wx, rg_bx, rg_lambda, rg_w_out)
    gla_params = (gla_w_in, gla_w_alpha, gla_b_alpha, gla_norm_g, gla_w_out)
    for i in range(depth):
        j = i // 2
        if i % 2 == 0:
            x = _rglru_layer(x, mod, norm_g, rg_params, i, j, natural_in=(i == 0))
        else:
            x = _gla_layer(x, mod, norm_g, gla_params, i, j)
        x = _ffn_layer(x, mod, norm_g, ffn_w_up_bf, ffn_conv_w, ffn_conv_b3, ffn_w_down, i,
                       natural_out=(i == depth - 1))
    return x
```

```python
import functools
import math

import jax
import jax.numpy as jnp
import numpy as np
from jax import lax
from jax.experimental import pallas as pl
from jax.experimental.pallas import tpu as pltpu

F32 = jnp.float32
BF16 = jnp.bfloat16

EPS = 1e-6
RG_BLOCKS = 4
RG_C = 8.0
RG_CONV = 4
FFN_CONV = 3
GLA_HEADS = 4
GLA_TAU = 16.0
GLA_RANK_PAD = 128
SUBLANES = 8

ROW_TILE = 256
TILE_VREGS = ROW_TILE // SUBLANES
TILES_PER_STEP = 4
ADA_COL_TILE = 3072
FFN_COL_CHUNK = 256
VMEM_LIMIT = 56 * 1024 * 1024

_GELU_C0 = 2.0 * math.sqrt(2.0 / math.pi) * math.log2(math.e)
_GELU_C1 = _GELU_C0 * 0.044715


def _dot(a, b):
    return jnp.dot(a, b, preferred_element_type=F32)


def _dot_nt(a, b):
    return lax.dot_general(a, b, (((1,), (1,)), ((), ())), preferred_element_type=F32)


def _rms(x, g):
    return x * lax.rsqrt(jnp.mean(x * x, axis=-1, keepdims=True) + EPS) * g


def _pre_norm(x, g, scale, shift):
    return _rms(x, g * (1.0 + scale)) + shift


def _post_norm(x, y, g, gate):
    return x + _rms(y, gate * g)


def _gelu_times(x, v):
    neg_2z_log2e = x * (-_GELU_C0 - _GELU_C1 * (x * x))
    return (x * v) / (1.0 + jnp.exp2(neg_2z_log2e))


def _sublane(shape):
    return lax.broadcasted_iota(jnp.int32, shape, 0) % SUBLANES


def _vreg(a, j):
    return a[j * SUBLANES:(j + 1) * SUBLANES]


def _load_tile_order(ref):
    return jnp.concatenate([ref[:, j, :] for j in range(TILE_VREGS)], axis=0)


def _store_natural_order(ref, val):
    for j in range(TILE_VREGS):
        ref[:, j, :] = _vreg(val, j)


def _delay(x, tail, k):
    ts = x.shape[0]
    n_tail = tail.shape[0] // SUBLANES
    last = _sublane((SUBLANES, 1)) == SUBLANES - 1
    head = []
    for j in range(k):
        cur = _vreg(x, TILE_VREGS - k + j)
        prv = _vreg(tail, n_tail - k + j)
        head.append(pltpu.roll(jnp.where(last, prv, cur), 1, 0))
    return jnp.concatenate(head + [x[:ts - k * SUBLANES]], axis=0)


def _params(*sem):
    return pltpu.CompilerParams(dimension_semantics=sem, vmem_limit_bytes=VMEM_LIMIT)


def _const_spec(shape):
    n = len(shape)
    return pl.BlockSpec(shape, lambda *_: (0,) * n, pipeline_mode=pl.Buffered(1))


def _layer_spec(shape, layer):
    n = len(shape) - 1
    return pl.BlockSpec((None,) + tuple(shape[1:]), lambda *_: (layer,) + (0,) * n,
                        pipeline_mode=pl.Buffered(1))


def _ada_kernel(c_ref, w_ref, b_ref, o_ref):
    c = c_ref[...]
    c_act = (c * jax.nn.sigmoid(c)).astype(BF16)
    o_ref[...] = _dot(c_act, w_ref[...].astype(BF16)) + b_ref[...]


def _ada_modulation(c, ada_w, ada_b):
    depth, d, n = ada_w.shape
    bsz = c.shape[0]
    tn = ADA_COL_TILE
    out = pl.pallas_call(
        _ada_kernel,
        grid=(depth, n // tn),
        in_specs=[
            pl.BlockSpec((bsz, d), lambda l, j: (0, 0)),
            pl.BlockSpec((None, d, tn), lambda l, j: (l, 0, j)),
            pl.BlockSpec((None, 1, tn), lambda l, j: (l, 0, j)),
        ],
        out_specs=pl.BlockSpec((None, bsz, tn), lambda l, j: (l, 0, j)),
        out_shape=jax.ShapeDtypeStruct((depth, bsz, n), F32),
        compiler_params=_params("arbitrary", "arbitrary"),
        name="ada_mod",
    )(c, ada_w, ada_b.reshape(depth, 1, n))
    return out.reshape(depth, bsz, 6, d)


def _stream_scan(a, u, carry):
    hl = [_vreg(u, 0)]
    al = [_vreg(a, 0)]
    for j in range(1, TILE_VREGS):
        hl.append(_vreg(a, j) * hl[-1] + _vreg(u, j))
        al.append(_vreg(a, j) * al[-1])
    sub = _sublane((SUBLANES, 1))
    ea, eh = al[-1], hl[-1]
    for k in (1, 2, 4):
        keep = sub >= k
        ea_sh = jnp.where(keep, pltpu.roll(ea, k, 0), 1.0)
        eh_sh = jnp.where(keep, pltpu.roll(eh, k, 0), 0.0)
        eh = ea * eh_sh + eh
        ea = ea * ea_sh
    after = ea * carry + eh
    before = jnp.where(sub == 0, carry, pltpu.roll(after, 1, 0))
    h = jnp.concatenate([hl[j] + al[j] * before for j in range(TILE_VREGS)], axis=0)
    return h, after


def _first_step():
    return (pl.program_id(0) == 0) & (pl.program_id(1) == 0)


def _tile_order_gather(x_hbm, buf, sem, b, step, slot):
    n_tiles = buf.shape[1] // ROW_TILE
    return [
        pltpu.make_async_copy(x_hbm.at[b, step * n_tiles + t, :, j, :],
                              buf.at[slot, pl.ds(t * ROW_TILE + j * SUBLANES, SUBLANES), :], sem.at[slot])
        for t in range(n_tiles) for j in range(TILE_VREGS)]


def _rglru_kernel(natural_in, x_ref, mod_ref, ng_ref, win32_ref, cw_ref, cb_ref, ba_ref, bx_ref, lam_ref,
                  wa32_ref, wx32_ref, wout32_ref, o_ref, win_ref, wg_ref, wout_ref, tail_ref, hc_ref, y_ref,
                  *gather_scratch):
    ts = ROW_TILE
    d = o_ref.shape[1]
    n_tiles = o_ref.shape[0] // ts
    blk = d // RG_BLOCKS
    n_tail = RG_CONV - 1

    if natural_in:
        xbuf, sem = gather_scratch
        b, i = pl.program_id(0), pl.program_id(1)
        n_steps = pl.num_programs(1)
        linear = b * n_steps + i
        slot = linear % 2

        @pl.when(linear == 0)
        def _():
            for cp in _tile_order_gather(x_ref, xbuf, sem, b, i, slot):
                cp.start()

        @pl.when(linear + 1 < pl.num_programs(0) * n_steps)
        def _():
            wrap = i + 1 == n_steps
            for cp in _tile_order_gather(x_ref, xbuf, sem, jnp.where(wrap, b + 1, b), jnp.where(wrap, 0, i + 1),
                                         1 - slot):
                cp.start()

        for cp in _tile_order_gather(x_ref, xbuf, sem, b, i, slot):
            cp.wait()

    @pl.when(_first_step())
    def _():
        win_ref[...] = win32_ref[...].astype(BF16)
        wout_ref[...] = wout32_ref[...].astype(BF16)
        for g in range(RG_BLOCKS):
            wg_ref[g, :, :blk] = wa32_ref[g].astype(BF16)
            wg_ref[g, :, blk:] = wx32_ref[g].astype(BF16)

    @pl.when(pl.program_id(1) == 0)
    def _():
        tail_ref[...] = jnp.zeros_like(tail_ref)
        hc_ref[...] = jnp.zeros_like(hc_ref)

    def prologue(t):
        x = xbuf[slot, t * ts:(t + 1) * ts, :] if natural_in else x_ref[t * ts:(t + 1) * ts, :]
        return x, _pre_norm(x, ng_ref[0:1, :], mod_ref[1:2, :], mod_ref[0:1, :]).astype(BF16)

    def epilogue(t, x, acc):
        o_ref[t * ts:(t + 1) * ts, :] = _post_norm(x, acc, ng_ref[1:2, :], mod_ref[2:3, :])

    def project(t, g):
        return (_dot(hs[t], win_ref[:, g * blk:(g + 1) * blk]),
                _dot(hs[t], win_ref[:, d + g * blk:d + (g + 1) * blk]))

    def mix(g, gate_br, xb):
        cols = slice(g * blk, (g + 1) * blk)
        tail = tail_ref[:, cols]
        tail_ref[:, cols] = xb[ts - n_tail * SUBLANES:, :]
        xc = cb_ref[:, cols] + _delay(xb, tail, 3) * cw_ref[0:1, cols]
        xc = xc + _delay(xb, tail, 2) * cw_ref[1:2, cols]
        xc = xc + _delay(xb, tail, 1) * cw_ref[2:3, cols]
        xc = xc + xb * cw_ref[3:4, cols]
        p = _dot(xc.astype(BF16), wg_ref[g])
        r = jax.nn.sigmoid(p[:, :blk] + ba_ref[:, cols])
        i_g = jax.nn.sigmoid(p[:, blk:] + bx_ref[:, cols])
        nl = -lam_ref[:, cols]
        softplus_nl = jnp.maximum(nl, 0.0) + jnp.log1p(jnp.exp(-jnp.abs(nl)))
        log_a = r * ((-RG_C) * softplus_nl)
        a = jnp.exp(log_a)
        m2 = jnp.tanh(log_a) * (-1.0 - a * a)
        u = jnp.where(m2 > 0.0, m2 * lax.rsqrt(m2), 0.0) * (i_g * xc)
        hs_g, after = _stream_scan(a, u, hc_ref[SUBLANES - 1:SUBLANES, cols])
        hc_ref[:, cols] = after
        return _gelu_times(gate_br, hs_g).astype(BF16)

    items = [(t, g) for t in range(n_tiles) for g in range(RG_BLOCKS)]
    xs, hs = {}, {}
    xs[0], hs[0] = prologue(0)

    def out_project(t):
        epilogue(t, xs.pop(t), _dot(y_ref[t % 2], wout_ref[...]))

    cur = project(0, 0)
    due = None
    for idx, (t, g) in enumerate(items):
        if g == 1 and t + 1 < n_tiles:
            xs[t + 1], hs[t + 1] = prologue(t + 1)
        nxt = project(*items[idx + 1]) if idx + 1 < len(items) else None
        if due is not None:
            out_project(due)
            due = None
        y_ref[t % 2, :, g * blk:(g + 1) * blk] = mix(g, *cur)
        if g == RG_BLOCKS - 1:
            due = t
        cur = nxt
    out_project(due)


def _rglru_layer(x, mod, ng, params, layer, mixer_index, natural_in):
    w_in, conv_w, conv_b, wa, ba, wx, bx, lam, w_out = params
    bsz, s, d = x.shape
    ts = ROW_TILE
    n_tiles = TILES_PER_STEP
    blk = d // RG_BLOCKS
    tile = pl.BlockSpec((None, n_tiles * ts, d), lambda b, i: (b, i, 0))
    x_tile, gather_scratch = tile, []
    if natural_in:
        x = x.reshape(bsz, s // ts, SUBLANES, TILE_VREGS, d)
        x_tile = pl.BlockSpec(memory_space=pl.ANY)
        gather_scratch = [pltpu.VMEM((2, n_tiles * ts, d), F32), pltpu.SemaphoreType.DMA((2,))]
    row = lambda v: v[:, None, :]
    mix_spec = lambda p: _layer_spec(p.shape, mixer_index)
    return pl.pallas_call(
        functools.partial(_rglru_kernel, natural_in),
        grid=(bsz, s // (n_tiles * ts)),
        in_specs=[
            x_tile,
            pl.BlockSpec((None, None, 6, d), lambda b, i: (layer, b, 0, 0)),
            _layer_spec(ng.shape, layer),
            mix_spec(w_in), mix_spec(conv_w), mix_spec(row(conv_b)), mix_spec(row(ba)), mix_spec(row(bx)),
            mix_spec(row(lam)), mix_spec(wa), mix_spec(wx), mix_spec(w_out),
        ],
        out_specs=tile,
        out_shape=jax.ShapeDtypeStruct((bsz, s, d), F32),
        scratch_shapes=[
            pltpu.VMEM(w_in.shape[1:], BF16),
            pltpu.VMEM((RG_BLOCKS, blk, 2 * blk), BF16),
            pltpu.VMEM(w_out.shape[1:], BF16),
            pltpu.VMEM(((RG_CONV - 1) * SUBLANES, d), F32),
            pltpu.VMEM((SUBLANES, d), F32),
            pltpu.VMEM((2, ts, d), BF16),
        ] + gather_scratch,
        compiler_params=_params("arbitrary", "arbitrary"),
        name="rglru_layer",
    )(x, mod, ng, w_in, conv_w, row(conv_b), row(ba), row(bx), row(lam), wa, wx, w_out)


def _ffn_kernel(natural_out, x_ref, mod_ref, ng_ref, wup_ref, cw_ref, cb_ref, wdn32_ref, o_ref, wdn_ref,
                tail_ref, act_ref):
    ts = ROW_TILE
    n_tiles = x_ref.shape[0] // ts
    f = wdn_ref.shape[0]
    fc = FFN_COL_CHUNK
    n_chunks = f // fc
    n_tail = FFN_CONV - 1

    @pl.when(_first_step())
    def _():
        wdn_ref[...] = wdn32_ref[...].astype(BF16)

    @pl.when(pl.program_id(1) == 0)
    def _():
        tail_ref[...] = jnp.zeros_like(tail_ref)

    def prologue(t):
        x = x_ref[t * ts:(t + 1) * ts, :]
        return x, _pre_norm(x, ng_ref[2:3, :], mod_ref[4:5, :], mod_ref[3:4, :]).astype(BF16)

    def epilogue(t, x, acc):
        out = _post_norm(x, acc, ng_ref[3:4, :], mod_ref[5:6, :])
        if natural_out:
            _store_natural_order(o_ref.at[t], out)
        else:
            o_ref[t * ts:(t + 1) * ts, :] = out

    def conv_cols(up, c0):
        cols = slice(c0, c0 + fc)
        tail = tail_ref[:, cols]
        tail_ref[:, cols] = up[ts - n_tail * SUBLANES:, :]
        y = cb_ref[:, cols] + _delay(up, tail, 2) * cw_ref[0:1, cols]
        y = y + _delay(up, tail, 1) * cw_ref[1:2, cols]
        return y + up * cw_ref[2:3, cols]

    items = [(t, c) for t in range(n_tiles) for c in range(n_chunks)]
    xs, hs = {}, {}
    xs[0], hs[0] = prologue(0)

    def up_pair(t, c):
        return (_dot(hs[t], wup_ref[:, c * fc:(c + 1) * fc]), _dot(hs[t], wup_ref[:, f + c * fc:f + (c + 1) * fc]))

    def down(t):
        epilogue(t, xs.pop(t), _dot(act_ref[t % 2], wdn_ref[...]))

    ups = up_pair(0, 0)
    due = None
    for idx, (t, c) in enumerate(items):
        if c == n_chunks - 4 and t + 1 < n_tiles:
            xs[t + 1], hs[t + 1] = prologue(t + 1)
        nxt = up_pair(*items[idx + 1]) if idx + 1 < len(items) else None
        if due is not None:
            down(due)
            due = None
        g = conv_cols(ups[0], c * fc)
        val = conv_cols(ups[1], f + c * fc)
        act_ref[t % 2, :, c * fc:(c + 1) * fc] = _gelu_times(g, val).astype(BF16)
        if c == n_chunks - 1:
            due = t
        ups = nxt
    down(due)


def _ffn_layer(x, mod, ng, w_up, conv_w, conv_b, w_down, layer, natural_out):
    bsz, s, d = x.shape
    ts = ROW_TILE
    n_tiles = TILES_PER_STEP
    rows = pl.BlockSpec((None, n_tiles * ts, d), lambda b, i: (b, i, 0))
    out_tile, out_shape = rows, (bsz, s, d)
    if natural_out:
        out_tile = pl.BlockSpec((None, n_tiles, SUBLANES, TILE_VREGS, d), lambda b, i: (b, i, 0, 0, 0))
        out_shape = (bsz, s // ts, SUBLANES, TILE_VREGS, d)
    out = pl.pallas_call(
        functools.partial(_ffn_kernel, natural_out),
        grid=(bsz, s // (n_tiles * ts)),
        in_specs=[
            rows,
            pl.BlockSpec((None, None, 6, d), lambda b, i: (layer, b, 0, 0)),
            _layer_spec(ng.shape, layer),
            _layer_spec(w_up.shape, layer),
            _layer_spec(conv_w.shape, layer),
            _layer_spec(conv_b.shape, layer),
            _layer_spec(w_down.shape, layer),
        ],
        out_specs=out_tile,
        out_shape=jax.ShapeDtypeStruct(out_shape, F32),
        scratch_shapes=[
            pltpu.VMEM(w_down.shape[1:], BF16),
            pltpu.VMEM(((FFN_CONV - 1) * SUBLANES, w_up.shape[2]), F32),
            pltpu.VMEM((2, ts, w_down.shape[1]), BF16),
        ],
        compiler_params=_params("arbitrary", "arbitrary"),
        name="ffn_layer",
    )(x, mod, ng, w_up, conv_w, conv_b, w_down)
    return out.reshape(bsz, s, d)


_STREAM_LEVELS = tuple(TILE_VREGS << i for i in (2, 1, 0))
_VREG_LEVELS = tuple(TILE_VREGS >> i for i in range(1, 4))
_DIRECT_BLOCK = 4
_MASKED_STREAM_LEVELS = _STREAM_LEVELS[1:]
_EXCLUDE = 1e30
_GLA_PROJ_PIECES = 3
_GLA_OUT_PIECES = 2


def _tile_order_times(c):
    p = np.arange(c)
    return (p % SUBLANES) * (c // SUBLANES) + p // SUBLANES


def _gla_stream_masks(c):
    t = _tile_order_times(c)
    return np.stack([(t[:, None] // (2 * b) == t[None, :] // (2 * b)) for b in _MASKED_STREAM_LEVELS]
                    ).astype(np.float32)


def _vreg_level_rows(b):
    upper = [j for j in range(TILE_VREGS) if (j // b) % 2 == 1]
    lower = [j for j in range(TILE_VREGS) if (j // b) % 2 == 0]
    return upper, lower


def _gla_vreg_masks():
    out = []
    s = np.arange(SUBLANES)
    for b in _VREG_LEVELS:
        upper, lower = _vreg_level_rows(b)
        ju = np.repeat(np.array(upper), SUBLANES)[:, None]
        jl = np.repeat(np.array(lower), SUBLANES)[None, :]
        su = np.tile(s, len(upper))[:, None]
        sl = np.tile(s, len(lower))[None, :]
        out.append((su == sl) & (ju // (2 * b) == jl // (2 * b)))
    return np.stack(out).astype(np.float32)


def _cumsum_tile_order(g):
    rows = [_vreg(g, 0)]
    for j in range(1, TILE_VREGS):
        rows.append(rows[-1] + _vreg(g, j))
    total = rows[-1]
    sub = _sublane((SUBLANES, 1))
    incl = total
    for k in (1, 2, 4):
        incl = incl + jnp.where(sub >= k, pltpu.roll(incl, k, 0), 0.0)
    before = incl - total
    return jnp.concatenate([r + before for r in rows], axis=0)


def _gla_scores(qk, v32, gc, state, fill):
    c, dk = gc.shape
    vreg = _vreg
    q = qk[:, :dk] * (dk ** -0.5)
    k = qk[:, dk:]
    v = v32.astype(BF16)
    g_last = gc[c - 1:c, :]

    fill()
    o_inter = _dot((q * jnp.exp(gc)).astype(BF16), state.astype(BF16))

    sub = _sublane((SUBLANES, 1))
    g_end = vreg(gc, TILE_VREGS - 1)
    tile_rows = lambda a: jnp.concatenate([a] * TILE_VREGS, axis=0)
    p_stream = []
    for b in _STREAM_LEVELS:
        w = b // TILE_VREGS
        g_mid = None
        for first in range(0, SUBLANES, 2 * w):
            row = jnp.broadcast_to(g_end[first + w - 1:first + w, :], (SUBLANES, dk))
            g_mid = row if g_mid is None else jnp.where(sub >= first, row, g_mid)
        upper = (lax.shift_right_logical(sub, w.bit_length() - 1) & 1) == 1
        g_q = tile_rows(jnp.where(upper, g_mid, _EXCLUDE))
        g_k = tile_rows(jnp.where(upper, -_EXCLUDE, g_mid))
        q_b = (q * jnp.exp(gc - g_q)).astype(BF16)
        k_b = (k * jnp.exp(g_k - gc)).astype(BF16)
        p_stream.append(_dot_nt(q_b, k_b))
    fill()

    p_vreg = []
    for b in _VREG_LEVELS:
        upper_rows, _ = _vreg_level_rows(b)
        q_rows, k_rows = [], []
        for j in range(TILE_VREGS):
            jm = (j // (2 * b)) * (2 * b) + b - 1
            if j in upper_rows:
                q_rows.append(vreg(q, j) * jnp.exp(vreg(gc, j) - vreg(gc, jm)))
            elif j == jm:
                k_rows.append(vreg(k, j))
            else:
                k_rows.append(vreg(k, j) * jnp.exp(vreg(gc, jm) - vreg(gc, j)))
        p_vreg.append(_dot_nt(jnp.concatenate(q_rows, axis=0).astype(BF16),
                              jnp.concatenate(k_rows, axis=0).astype(BF16)))
    fill()

    k_rev = (k * jnp.exp(g_last - gc)).astype(BF16)
    upd = lax.dot_general(k_rev, v, (((0,), (0,)), ((), ())), preferred_element_type=F32)
    decay_rows = jnp.broadcast_to(jnp.exp(g_last), (dk, dk))
    decay_col = jnp.transpose(decay_rows)[:, 0:1]
    new_state = decay_col * state + upd

    direct = []
    for ju in range(TILE_VREGS):
        terms = vreg(v32, ju) * jnp.sum(vreg(q, ju) * vreg(k, ju), axis=-1, keepdims=True)
        for jl in range((ju // _DIRECT_BLOCK) * _DIRECT_BLOCK, ju):
            w = vreg(q, ju) * vreg(k, jl) * jnp.exp(vreg(gc, ju) - vreg(gc, jl))
            terms = terms + vreg(v32, jl) * jnp.sum(w, axis=-1, keepdims=True)
        direct.append(terms)
    return (o_inter, p_stream, p_vreg, direct), new_state


def _gla_combine(scores, v32, r, hn, sm_ref, vm_ref, fill):
    o, p_stream, p_vreg, direct = scores
    v = v32.astype(BF16)

    attn = None
    for b, p in zip(_STREAM_LEVELS, p_stream):
        if b in _MASKED_STREAM_LEVELS:
            p = p * sm_ref[_MASKED_STREAM_LEVELS.index(b)]
        attn = p if attn is None else attn + p
    o = o + _dot(attn.astype(BF16), v)
    fill()

    o_rows = [_vreg(o, j) + direct[j] for j in range(TILE_VREGS)]
    for li, (b, p) in enumerate(zip(_VREG_LEVELS, p_vreg)):
        upper_rows, _ = _vreg_level_rows(b)
        p = (p * vm_ref[li]).astype(BF16)
        v_low = jnp.concatenate(
            [v[j0 * SUBLANES:(j0 + b) * SUBLANES] for j0 in range(0, TILE_VREGS, 2 * b)], axis=0)
        o_up = _dot(p, v_low)
        if li == 1:
            fill()
        for idx, j in enumerate(upper_rows):
            o_rows[j] = o_rows[j] + _vreg(o_up, idx)
    o = jnp.concatenate(o_rows, axis=0)
    return (_rms(o, hn) * (r * jax.nn.sigmoid(r))).astype(BF16)


def _gla_kernel(x_ref, mod_ref, ng_ref, win32_ref, wz_ref, wal_ref, bal_ref, hn_ref, sm_ref, vm_ref,
                wout32_ref, o_ref, wh_ref, wout_ref, state_ref, y_ref):
    c = ROW_TILE
    d = x_ref.shape[1]
    n_tiles = x_ref.shape[0] // c
    dk, dv = state_ref.shape[1:]
    qk = GLA_HEADS * dk

    @pl.when(_first_step())
    def _():
        wout_ref[...] = wout32_ref[...].astype(BF16)
        for hh in range(GLA_HEADS):
            for dst, src, w in ((0, hh * dk, dk), (dk, qk + hh * dk, dk), (2 * dk, 2 * qk + hh * dv, dv),
                                (2 * dk + dv, 2 * qk + d + hh * dv, dv)):
                wh_ref[hh, :, dst:dst + w] = win32_ref[:, src:src + w].astype(BF16)

    @pl.when(pl.program_id(1) == 0)
    def _():
        state_ref[...] = jnp.zeros_like(state_ref)

    def prologue(t):
        x = x_ref[t * c:(t + 1) * c, :]
        h = _pre_norm(x, ng_ref[0:1, :], mod_ref[1:2, :], mod_ref[0:1, :]).astype(BF16)
        z = _dot(h, wz_ref[...])
        ga = _dot(z.astype(BF16), wal_ref[...]) + bal_ref[...]
        g = (jnp.minimum(ga, 0.0) - jnp.log1p(jnp.exp(-jnp.abs(ga)))) / GLA_TAU
        return x, h, _cumsum_tile_order(g)

    def epilogue(t, x, acc):
        o_ref[t * c:(t + 1) * c, :] = _post_norm(x, acc, ng_ref[1:2, :], mod_ref[2:3, :])

    items = [(t, hh) for t in range(n_tiles) for hh in range(GLA_HEADS)]
    xs, hs, gcs = {}, {}, {}
    xs[0], hs[0], gcs[0] = prologue(0)

    def project_piece(t, hh, j):
        w = 2 * dk if j == 0 else dv
        c0 = 0 if j == 0 else 2 * dk + (j - 1) * dv
        return _dot(hs[t], wh_ref[hh, :, c0:c0 + w])

    def out_piece(t, parts, j):
        w = d // _GLA_OUT_PIECES
        parts.append(_dot(y_ref[t % 2], wout_ref[:, j * w:(j + 1) * w]))
        if j == _GLA_OUT_PIECES - 1:
            epilogue(t, xs.pop(t), jnp.concatenate(parts, axis=1))

    cur = [project_piece(0, 0, j) for j in range(_GLA_PROJ_PIECES)]
    due = None
    for idx, (t, hh) in enumerate(items):
        if hh == 1 and t + 1 < n_tiles:
            xs[t + 1], hs[t + 1], gcs[t + 1] = prologue(t + 1)
        nxt, pending = [], []
        if idx + 1 < len(items):
            pending += [lambda j=j, idx=idx, nxt=nxt: nxt.append(project_piece(*items[idx + 1], j))
                        for j in range(_GLA_PROJ_PIECES)]
        if due is not None:
            parts = []
            pending += [functools.partial(out_piece, due, parts, j) for j in range(_GLA_OUT_PIECES)]
            due = None
        pending.reverse()
        fill = lambda pending=pending: pending.pop()() if pending else None
        qk, v32, r = cur
        scores, state_ref[hh] = _gla_scores(qk, v32, gcs[t][:, hh * dk:(hh + 1) * dk], state_ref[hh], fill)
        y_ref[t % 2, :, hh * dv:(hh + 1) * dv] = _gla_combine(scores, v32, r, hn_ref[...], sm_ref, vm_ref, fill)
        while pending:
            fill()
        if hh == GLA_HEADS - 1:
            due = t
        cur = nxt
    parts = []
    for j in range(_GLA_OUT_PIECES):
        out_piece(due, parts, j)


def _gla_layer(x, mod, ng, params, layer, mixer_index):
    w_in, w_alpha, b_alpha, head_norm, w_out = params
    bsz, s, d = x.shape
    c = ROW_TILE
    rank, qk = w_alpha.shape[1:]
    dk, dv = qk // GLA_HEADS, d // GLA_HEADS
    n_main = 2 * qk + 2 * d
    w_z = jnp.pad(w_in[mixer_index, :, n_main:], ((0, 0), (0, GLA_RANK_PAD - rank))).astype(BF16)
    w_al = jnp.pad(w_alpha[mixer_index], ((0, GLA_RANK_PAD - rank), (0, 0))).astype(BF16)
    stream_masks = jnp.asarray(_gla_stream_masks(c), F32)
    vreg_masks = jnp.asarray(_gla_vreg_masks(), F32)
    n_tiles = TILES_PER_STEP
    tile = pl.BlockSpec((None, n_tiles * c, d), lambda b, i: (b, i, 0))
    row = lambda v: v[:, None, :]
    mix_spec = lambda p: _layer_spec(p.shape, mixer_index)
    return pl.pallas_call(
        _gla_kernel,
        grid=(bsz, s // (n_tiles * c)),
        in_specs=[
            tile,
            pl.BlockSpec((None, None, 6, d), lambda b, i: (layer, b, 0, 0)),
            _layer_spec(ng.shape, layer),
            mix_spec(w_in),
            _const_spec(w_z.shape),
            _const_spec(w_al.shape),
            mix_spec(row(b_alpha)),
            mix_spec(row(head_norm)),
            _const_spec(stream_masks.shape),
            _const_spec(vreg_masks.shape),
            mix_spec(w_out),
        ],
        out_specs=tile,
        out_shape=jax.ShapeDtypeStruct(x.shape, F32),
        scratch_shapes=[
            pltpu.VMEM((GLA_HEADS, d, 2 * dk + 2 * dv), BF16),
            pltpu.VMEM(w_out.shape[1:], BF16),
            pltpu.VMEM((GLA_HEADS, dk, dv), F32),
            pltpu.VMEM((2, c, d), BF16),
        ],
        compiler_params=_params("arbitrary", "arbitrary"),
        name="gla_layer",
    )(x, mod, ng, w_in, w_z, w_al, row(b_alpha), row(head_norm), stream_masks, vreg_masks, w_out)


def kernel(x, c, ada_w, ada_b, norm_g, ffn_w_up, ffn_conv_w, ffn_conv_b, ffn_w_down, rg_w_in, rg_conv_w,
           rg_conv_b, rg_wa, rg_ba, rg_wx, rg_bx, rg_lambda, rg_w_out, gla_w_in, gla_w_alpha, gla_b_alpha,
           gla_norm_g, gla_w_out):
    depth = ada_w.shape[0]
    assert depth >= 1 and x.shape[1] % (ROW_TILE * TILES_PER_STEP) == 0
    mod = _ada_modulation(c, ada_w, ada_b)
    ffn_w_up_bf = ffn_w_up.astype(BF16)
    ffn_conv_b3 = ffn_conv_b[:, None, :]
    rg_params = (rg_w_in, rg_conv_w, rg_conv_b, rg_wa, rg_ba, rg_wx, rg_bx, rg_lambda, rg_w_out)
    gla_params = (gla_w_in, gla_w_alpha, gla_b_alpha, gla_norm_g, gla_w_out)
    for i in range(depth):
        j = i // 2
        if i % 2 == 0:
            x = _rglru_layer(x, mod, norm_g, rg_params, i, j, natural_in=(i == 0))
        else:
            x = _gla_layer(x, mod, norm_g, gla_params, i, j)
        x = _ffn_layer(x, mod, norm_g, ffn_w_up_bf, ffn_conv_w, ffn_conv_b3, ffn_w_down, i,
                       natural_out=(i == depth - 1))
    return x
```

```python
import functools
import math

import jax
import jax.numpy as jnp
import numpy as np
from jax import lax
from jax.experimental import pallas as pl
from jax.experimental.pallas import tpu as pltpu

F32 = jnp.float32
BF16 = jnp.bfloat16

EPS = 1e-6
RG_BLOCKS = 4
RG_C = 8.0
RG_CONV = 4
FFN_CONV = 3
GLA_HEADS = 4
GLA_TAU = 16.0
GLA_RANK_PAD = 128
SUBLANES = 8

ROW_TILE = 256
TILE_VREGS = ROW_TILE // SUBLANES
TILES_PER_STEP = 4
ADA_COL_TILE = 3072
FFN_COL_CHUNK = 256
MIXER_NEXT_NORM_ITEM = 1
FFN_NEXT_NORM_LEAD = 4
VMEM_LIMIT = 56 * 1024 * 1024

_GELU_C0 = 2.0 * math.sqrt(2.0 / math.pi) * math.log2(math.e)
_GELU_C1 = _GELU_C0 * 0.044715


def _dot(a, b):
    return jnp.dot(a, b, preferred_element_type=F32)


def _dot_nt(a, b):
    return lax.dot_general(a, b, (((1,), (1,)), ((), ())), preferred_element_type=F32)


def _rms(x, g):
    return x * lax.rsqrt(jnp.mean(x * x, axis=-1, keepdims=True) + EPS) * g


def _pre_norm(x, g, scale, shift):
    return _rms(x, g * (1.0 + scale)) + shift


def _post_norm(x, y, g, gate):
    return x + _rms(y, gate * g)


def _gelu_times(x, v):
    neg_2z_log2e = x * (-_GELU_C0 - _GELU_C1 * (x * x))
    return (x * v) / (1.0 + jnp.exp2(neg_2z_log2e))


def _sublane(shape):
    return lax.broadcasted_iota(jnp.int32, shape, 0) % SUBLANES


def _vreg(a, j):
    return a[j * SUBLANES:(j + 1) * SUBLANES]


def _load_tile_order(ref):
    return jnp.concatenate([ref[:, j, :] for j in range(TILE_VREGS)], axis=0)


def _store_natural_order(ref, val):
    for j in range(TILE_VREGS):
        ref[:, j, :] = _vreg(val, j)


def _delay(x, tail, k):
    ts = x.shape[0]
    n_tail = tail.shape[0] // SUBLANES
    last = _sublane((SUBLANES, 1)) == SUBLANES - 1
    head = []
    for j in range(k):
        cur = _vreg(x, TILE_VREGS - k + j)
        prv = _vreg(tail, n_tail - k + j)
        head.append(pltpu.roll(jnp.where(last, prv, cur), 1, 0))
    return jnp.concatenate(head + [x[:ts - k * SUBLANES]], axis=0)


def _params(*sem):
    return pltpu.CompilerParams(dimension_semantics=sem, vmem_limit_bytes=VMEM_LIMIT)


def _const_spec(shape):
    n = len(shape)
    return pl.BlockSpec(shape, lambda *_: (0,) * n, pipeline_mode=pl.Buffered(1))


def _layer_spec(shape, layer):
    n = len(shape) - 1
    return pl.BlockSpec((None,) + tuple(shape[1:]), lambda *_: (layer,) + (0,) * n,
                        pipeline_mode=pl.Buffered(1))


def _ada_kernel(c_ref, w_ref, b_ref, o_ref):
    c = c_ref[...]
    c_act = (c * jax.nn.sigmoid(c)).astype(BF16)
    o_ref[...] = _dot(c_act, w_ref[...].astype(BF16)) + b_ref[...]


def _ada_modulation(c, ada_w, ada_b):
    depth, d, n = ada_w.shape
    bsz = c.shape[0]
    tn = ADA_COL_TILE
    out = pl.pallas_call(
        _ada_kernel,
        grid=(depth, n // tn),
        in_specs=[
            pl.BlockSpec((bsz, d), lambda l, j: (0, 0)),
            pl.BlockSpec((None, d, tn), lambda l, j: (l, 0, j)),
            pl.BlockSpec((None, 1, tn), lambda l, j: (l, 0, j)),
        ],
        out_specs=pl.BlockSpec((None, bsz, tn), lambda l, j: (l, 0, j)),
        out_shape=jax.ShapeDtypeStruct((depth, bsz, n), F32),
        compiler_params=_params("arbitrary", "arbitrary"),
        name="ada_mod",
    )(c, ada_w, ada_b.reshape(depth, 1, n))
    return out.reshape(depth, bsz, 6, d)


def _stream_scan(a, u, carry):
    hl = [_vreg(u, 0)]
    al = [_vreg(a, 0)]
    for j in range(1, TILE_VREGS):
        hl.append(_vreg(a, j) * hl[-1] + _vreg(u, j))
        al.append(_vreg(a, j) * al[-1])
    sub = _sublane((SUBLANES, 1))
    ea, eh = al[-1], hl[-1]
    for k in (1, 2, 4):
        keep = sub >= k
        ea_sh = jnp.where(keep, pltpu.roll(ea, k, 0), 1.0)
        eh_sh = jnp.where(keep, pltpu.roll(eh, k, 0), 0.0)
        eh = ea * eh_sh + eh
        ea = ea * ea_sh
    after = ea * carry + eh
    before = jnp.where(sub == 0, carry, pltpu.roll(after, 1, 0))
    h = jnp.concatenate([hl[j] + al[j] * before for j in range(TILE_VREGS)], axis=0)
    return h, after


def _first_step():
    return (pl.program_id(0) == 0) & (pl.program_id(1) == 0)


def _tile_order_gather(x_hbm, buf, sem, b, step, slot):
    n_tiles = buf.shape[1] // ROW_TILE
    return [
        pltpu.make_async_copy(x_hbm.at[b, step * n_tiles + t, :, j, :],
                              buf.at[slot, pl.ds(t * ROW_TILE + j * SUBLANES, SUBLANES), :], sem.at[slot])
        for t in range(n_tiles) for j in range(TILE_VREGS)]


def _rglru_kernel(natural_in, x_ref, mod_ref, ng_ref, win32_ref, cw_ref, cb_ref, ba_ref, bx_ref, lam_ref,
                  wa32_ref, wx32_ref, wout32_ref, o_ref, win_ref, wg_ref, wout_ref, tail_ref, hc_ref, y_ref,
                  *gather_scratch):
    ts = ROW_TILE
    d = o_ref.shape[1]
    n_tiles = o_ref.shape[0] // ts
    blk = d // RG_BLOCKS
    n_tail = RG_CONV - 1

    if natural_in:
        xbuf, sem = gather_scratch
        b, i = pl.program_id(0), pl.program_id(1)
        n_steps = pl.num_programs(1)
        linear = b * n_steps + i
        slot = linear % 2

        @pl.when(linear == 0)
        def _():
            for cp in _tile_order_gather(x_ref, xbuf, sem, b, i, slot):
                cp.start()

        @pl.when(linear + 1 < pl.num_programs(0) * n_steps)
        def _():
            wrap = i + 1 == n_steps
            for cp in _tile_order_gather(x_ref, xbuf, sem, jnp.where(wrap, b + 1, b), jnp.where(wrap, 0, i + 1),
                                         1 - slot):
                cp.start()

        for cp in _tile_order_gather(x_ref, xbuf, sem, b, i, slot):
            cp.wait()

    @pl.when(_first_step())
    def _():
        win_ref[...] = win32_ref[...].astype(BF16)
        wout_ref[...] = wout32_ref[...].astype(BF16)
        for g in range(RG_BLOCKS):
            wg_ref[g, :, :blk] = wa32_ref[g].astype(BF16)
            wg_ref[g, :, blk:] = wx32_ref[g].astype(BF16)

    @pl.when(pl.program_id(1) == 0)
    def _():
        tail_ref[...] = jnp.zeros_like(tail_ref)
        hc_ref[...] = jnp.zeros_like(hc_ref)

    def prologue(t):
        x = xbuf[slot, t * ts:(t + 1) * ts, :] if natural_in else x_ref[t * ts:(t + 1) * ts, :]
        return x, _pre_norm(x, ng_ref[0:1, :], mod_ref[1:2, :], mod_ref[0:1, :]).astype(BF16)

    def epilogue(t, x, acc):
        o_ref[t * ts:(t + 1) * ts, :] = _post_norm(x, acc, ng_ref[1:2, :], mod_ref[2:3, :])

    def project(t, g):
        return (_dot(hs[t], win_ref[:, g * blk:(g + 1) * blk]),
                _dot(hs[t], win_ref[:, d + g * blk:d + (g + 1) * blk]))

    def mix(g, gate_br, xb):
        cols = slice(g * blk, (g + 1) * blk)
        tail = tail_ref[:, cols]
        tail_ref[:, cols] = xb[ts - n_tail * SUBLANES:, :]
        xc = cb_ref[:, cols] + _delay(xb, tail, 3) * cw_ref[0:1, cols]
        xc = xc + _delay(xb, tail, 2) * cw_ref[1:2, cols]
        xc = xc + _delay(xb, tail, 1) * cw_ref[2:3, cols]
        xc = xc + xb * cw_ref[3:4, cols]
        p = _dot(xc.astype(BF16), wg_ref[g])
        r = jax.nn.sigmoid(p[:, :blk] + ba_ref[:, cols])
        i_g = jax.nn.sigmoid(p[:, blk:] + bx_ref[:, cols])
        nl = -lam_ref[:, cols]
        softplus_nl = jnp.maximum(nl, 0.0) + jnp.log1p(jnp.exp(-jnp.abs(nl)))
        log_a = r * ((-RG_C) * softplus_nl)
        a = jnp.exp(log_a)
        m2 = jnp.tanh(log_a) * (-1.0 - a * a)
        u = jnp.where(m2 > 0.0, m2 * lax.rsqrt(m2), 0.0) * (i_g * xc)
        hs_g, after = _stream_scan(a, u, hc_ref[SUBLANES - 1:SUBLANES, cols])
        hc_ref[:, cols] = after
        return _gelu_times(gate_br, hs_g).astype(BF16)

    items = [(t, g) for t in range(n_tiles) for g in range(RG_BLOCKS)]
    xs, hs = {}, {}
    xs[0], hs[0] = prologue(0)

    def out_project(t):
        epilogue(t, xs.pop(t), _dot(y_ref[t % 2], wout_ref[...]))

    cur = project(0, 0)
    due = None
    for idx, (t, g) in enumerate(items):
        if g == MIXER_NEXT_NORM_ITEM and t + 1 < n_tiles:
            xs[t + 1], hs[t + 1] = prologue(t + 1)
        nxt = project(*items[idx + 1]) if idx + 1 < len(items) else None
        if due is not None:
            out_project(due)
            due = None
        y_ref[t % 2, :, g * blk:(g + 1) * blk] = mix(g, *cur)
        if g == RG_BLOCKS - 1:
            due = t
        cur = nxt
    out_project(due)


def _rglru_layer(x, mod, ng, params, layer, mixer_index, natural_in):
    w_in, conv_w, conv_b, wa, ba, wx, bx, lam, w_out = params
    bsz, s, d = x.shape
    ts = ROW_TILE
    n_tiles = TILES_PER_STEP
    blk = d // RG_BLOCKS
    tile = pl.BlockSpec((None, n_tiles * ts, d), lambda b, i: (b, i, 0))
    x_tile, gather_scratch = tile, []
    if natural_in:
        x = x.reshape(bsz, s // ts, SUBLANES, TILE_VREGS, d)
        x_tile = pl.BlockSpec(memory_space=pl.ANY)
        gather_scratch = [pltpu.VMEM((2, n_tiles * ts, d), F32), pltpu.SemaphoreType.DMA((2,))]
    row = lambda v: v[:, None, :]
    mix_spec = lambda p: _layer_spec(p.shape, mixer_index)
    return pl.pallas_call(
        functools.partial(_rglru_kernel, natural_in),
        grid=(bsz, s // (n_tiles * ts)),
        in_specs=[
            x_tile,
            pl.BlockSpec((None, None, 6, d), lambda b, i: (layer, b, 0, 0)),
            _layer_spec(ng.shape, layer),
            mix_spec(w_in), mix_spec(conv_w), mix_spec(row(conv_b)), mix_spec(row(ba)), mix_spec(row(bx)),
            mix_spec(row(lam)), mix_spec(wa), mix_spec(wx), mix_spec(w_out),
        ],
        out_specs=tile,
        out_shape=jax.ShapeDtypeStruct((bsz, s, d), F32),
        scratch_shapes=[
            pltpu.VMEM(w_in.shape[1:], BF16),
            pltpu.VMEM((RG_BLOCKS, blk, 2 * blk), BF16),
            pltpu.VMEM(w_out.shape[1:], BF16),
            pltpu.VMEM(((RG_CONV - 1) * SUBLANES, d), F32),
            pltpu.VMEM((SUBLANES, d), F32),
            pltpu.VMEM((2, ts, d), BF16),
        ] + gather_scratch,
        compiler_params=_params("arbitrary", "arbitrary"),
        name="rglru_layer",
    )(x, mod, ng, w_in, conv_w, row(conv_b), row(ba), row(bx), row(lam), wa, wx, w_out)


def _ffn_kernel(natural_out, x_ref, mod_ref, ng_ref, wup_ref, cw_ref, cb_ref, wdn32_ref, o_ref, wdn_ref,
                tail_ref, act_ref):
    ts = ROW_TILE
    n_tiles = x_ref.shape[0] // ts
    f = wdn_ref.shape[0]
    fc = FFN_COL_CHUNK
    n_chunks = f // fc
    n_tail = FFN_CONV - 1

    @pl.when(_first_step())
    def _():
        wdn_ref[...] = wdn32_ref[...].astype(BF16)

    @pl.when(pl.program_id(1) == 0)
    def _():
        tail_ref[...] = jnp.zeros_like(tail_ref)

    def prologue(t):
        x = x_ref[t * ts:(t + 1) * ts, :]
        return x, _pre_norm(x, ng_ref[2:3, :], mod_ref[4:5, :], mod_ref[3:4, :]).astype(BF16)

    def epilogue(t, x, acc):
        out = _post_norm(x, acc, ng_ref[3:4, :], mod_ref[5:6, :])
        if natural_out:
            _store_natural_order(o_ref.at[t], out)
        else:
            o_ref[t * ts:(t + 1) * ts, :] = out

    def conv_cols(up, c0):
        cols = slice(c0, c0 + fc)
        tail = tail_ref[:, cols]
        tail_ref[:, cols] = up[ts - n_tail * SUBLANES:, :]
        y = cb_ref[:, cols] + _delay(up, tail, 2) * cw_ref[0:1, cols]
        y = y + _delay(up, tail, 1) * cw_ref[1:2, cols]
        return y + up * cw_ref[2:3, cols]

    items = [(t, c) for t in range(n_tiles) for c in range(n_chunks)]
    xs, hs = {}, {}
    xs[0], hs[0] = prologue(0)

    def up_pair(t, c):
        return (_dot(hs[t], wup_ref[:, c * fc:(c + 1) * fc]), _dot(hs[t], wup_ref[:, f + c * fc:f + (c + 1) * fc]))

    def down(t):
        epilogue(t, xs.pop(t), _dot(act_ref[t % 2], wdn_ref[...]))

    ups = up_pair(0, 0)
    due = None
    for idx, (t, c) in enumerate(items):
        if c == n_chunks - FFN_NEXT_NORM_LEAD and t + 1 < n_tiles:
            xs[t + 1], hs[t + 1] = prologue(t + 1)
        nxt = up_pair(*items[idx + 1]) if idx + 1 < len(items) else None
        if due is not None:
            down(due)
            due = None
        g = conv_cols(ups[0], c * fc)
        val = conv_cols(ups[1], f + c * fc)
        act_ref[t % 2, :, c * fc:(c + 1) * fc] = _gelu_times(g, val).astype(BF16)
        if c == n_chunks - 1:
            due = t
        ups = nxt
    down(due)


def _ffn_layer(x, mod, ng, w_up, conv_w, conv_b, w_down, layer, natural_out):
    bsz, s, d = x.shape
    ts = ROW_TILE
    n_tiles = TILES_PER_STEP
    rows = pl.BlockSpec((None, n_tiles * ts, d), lambda b, i: (b, i, 0))
    out_tile, out_shape = rows, (bsz, s, d)
    if natural_out:
        out_tile = pl.BlockSpec((None, n_tiles, SUBLANES, TILE_VREGS, d), lambda b, i: (b, i, 0, 0, 0))
        out_shape = (bsz, s // ts, SUBLANES, TILE_VREGS, d)
    out = pl.pallas_call(
        functools.partial(_ffn_kernel, natural_out),
        grid=(bsz, s // (n_tiles * ts)),
        in_specs=[
            rows,
            pl.BlockSpec((None, None, 6, d), lambda b, i: (layer, b, 0, 0)),
            _layer_spec(ng.shape, layer),
            _layer_spec(w_up.shape, layer),
            _layer_spec(conv_w.shape, layer),
            _layer_spec(conv_b.shape, layer),
            _layer_spec(w_down.shape, layer),
        ],
        out_specs=out_tile,
        out_shape=jax.ShapeDtypeStruct(out_shape, F32),
        scratch_shapes=[
            pltpu.VMEM(w_down.shape[1:], BF16),
            pltpu.VMEM(((FFN_CONV - 1) * SUBLANES, w_up.shape[2]), F32),
            pltpu.VMEM((2, ts, w_down.shape[1]), BF16),
        ],
        compiler_params=_params("arbitrary", "arbitrary"),
        name="ffn_layer",
    )(x, mod, ng, w_up, conv_w, conv_b, w_down)
    return out.reshape(bsz, s, d)


_STREAM_LEVELS = tuple(TILE_VREGS << i for i in (2, 1, 0))
_VREG_LEVELS = tuple(TILE_VREGS >> i for i in range(1, 4))
_DIRECT_BLOCK = 4
_MASKED_STREAM_LEVELS = _STREAM_LEVELS[1:]
_EXCLUDE = 1e30
_GLA_PROJ_PIECES = 3
_GLA_OUT_PIECES = 2


def _tile_order_times(c):
    p = np.arange(c)
    return (p % SUBLANES) * (c // SUBLANES) + p // SUBLANES


def _gla_stream_masks(c):
    t = _tile_order_times(c)
    return np.stack([(t[:, None] // (2 * b) == t[None, :] // (2 * b)) for b in _MASKED_STREAM_LEVELS]
                    ).astype(np.float32)


def _vreg_level_rows(b):
    upper = [j for j in range(TILE_VREGS) if (j // b) % 2 == 1]
    lower = [j for j in range(TILE_VREGS) if (j // b) % 2 == 0]
    return upper, lower


def _gla_vreg_masks():
    out = []
    s = np.arange(SUBLANES)
    for b in _VREG_LEVELS:
        upper, lower = _vreg_level_rows(b)
        ju = np.repeat(np.array(upper), SUBLANES)[:, None]
        jl = np.repeat(np.array(lower), SUBLANES)[None, :]
        su = np.tile(s, len(upper))[:, None]
        sl = np.tile(s, len(lower))[None, :]
        out.append((su == sl) & (ju // (2 * b) == jl // (2 * b)))
    return np.stack(out).astype(np.float32)


def _cumsum_tile_order(g):
    rows = [_vreg(g, 0)]
    for j in range(1, TILE_VREGS):
        rows.append(rows[-1] + _vreg(g, j))
    total = rows[-1]
    sub = _sublane((SUBLANES, 1))
    incl = total
    for k in (1, 2, 4):
        incl = incl + jnp.where(sub >= k, pltpu.roll(incl, k, 0), 0.0)
    before = incl - total
    return jnp.concatenate([r + before for r in rows], axis=0)


def _gla_scores(qk, v32, gc, state, fill):
    c, dk = gc.shape
    vreg = _vreg
    q = qk[:, :dk] * (dk ** -0.5)
    k = qk[:, dk:]
    v = v32.astype(BF16)
    g_last = gc[c - 1:c, :]

    fill()
    o_inter = _dot((q * jnp.exp(gc)).astype(BF16), state.astype(BF16))

    sub = _sublane((SUBLANES, 1))
    g_end = vreg(gc, TILE_VREGS - 1)
    tile_rows = lambda a: jnp.concatenate([a] * TILE_VREGS, axis=0)
    p_stream = []
    for b in _STREAM_LEVELS:
        w = b // TILE_VREGS
        g_mid = None
        for first in range(0, SUBLANES, 2 * w):
            row = jnp.broadcast_to(g_end[first + w - 1:first + w, :], (SUBLANES, dk))
            g_mid = row if g_mid is None else jnp.where(sub >= first, row, g_mid)
        upper = (lax.shift_right_logical(sub, w.bit_length() - 1) & 1) == 1
        g_q = tile_rows(jnp.where(upper, g_mid, _EXCLUDE))
        g_k = tile_rows(jnp.where(upper, -_EXCLUDE, g_mid))
        q_b = (q * jnp.exp(gc - g_q)).astype(BF16)
        k_b = (k * jnp.exp(g_k - gc)).astype(BF16)
        p_stream.append(_dot_nt(q_b, k_b))
    fill()

    p_vreg = []
    for b in _VREG_LEVELS:
        upper_rows, _ = _vreg_level_rows(b)
        q_rows, k_rows = [], []
        for j in range(TILE_VREGS):
            jm = (j // (2 * b)) * (2 * b) + b - 1
            if j in upper_rows:
                q_rows.append(vreg(q, j) * jnp.exp(vreg(gc, j) - vreg(gc, jm)))
            elif j == jm:
                k_rows.append(vreg(k, j))
            else:
                k_rows.append(vreg(k, j) * jnp.exp(vreg(gc, jm) - vreg(gc, j)))
        p_vreg.append(_dot_nt(jnp.concatenate(q_rows, axis=0).astype(BF16),
                              jnp.concatenate(k_rows, axis=0).astype(BF16)))
    fill()

    k_rev = (k * jnp.exp(g_last - gc)).astype(BF16)
    upd = lax.dot_general(k_rev, v, (((0,), (0,)), ((), ())), preferred_element_type=F32)
    decay_rows = jnp.broadcast_to(jnp.exp(g_last), (dk, dk))
    decay_col = jnp.transpose(decay_rows)[:, 0:1]
    new_state = decay_col * state + upd

    direct = []
    for ju in range(TILE_VREGS):
        terms = vreg(v32, ju) * jnp.sum(vreg(q, ju) * vreg(k, ju), axis=-1, keepdims=True)
        for jl in range((ju // _DIRECT_BLOCK) * _DIRECT_BLOCK, ju):
            w = vreg(q, ju) * vreg(k, jl) * jnp.exp(vreg(gc, ju) - vreg(gc, jl))
            terms = terms + vreg(v32, jl) * jnp.sum(w, axis=-1, keepdims=True)
        direct.append(terms)
    return (o_inter, p_stream, p_vreg, direct), new_state


def _gla_combine(scores, v32, r, hn, sm_ref, vm_ref, fill):
    o, p_stream, p_vreg, direct = scores
    v = v32.astype(BF16)

    attn = None
    for b, p in zip(_STREAM_LEVELS, p_stream):
        if b in _MASKED_STREAM_LEVELS:
            p = p * sm_ref[_MASKED_STREAM_LEVELS.index(b)]
        attn = p if attn is None else attn + p
    o = o + _dot(attn.astype(BF16), v)
    fill()

    o_rows = [_vreg(o, j) + direct[j] for j in range(TILE_VREGS)]
    for li, (b, p) in enumerate(zip(_VREG_LEVELS, p_vreg)):
        upper_rows, _ = _vreg_level_rows(b)
        p = (p * vm_ref[li]).astype(BF16)
        v_low = jnp.concatenate(
            [v[j0 * SUBLANES:(j0 + b) * SUBLANES] for j0 in range(0, TILE_VREGS, 2 * b)], axis=0)
        o_up = _dot(p, v_low)
        if li == len(_VREG_LEVELS) // 2:
            fill()
        for idx, j in enumerate(upper_rows):
            o_rows[j] = o_rows[j] + _vreg(o_up, idx)
    o = jnp.concatenate(o_rows, axis=0)
    return (_rms(o, hn) * (r * jax.nn.sigmoid(r))).astype(BF16)


def _gla_kernel(x_ref, mod_ref, ng_ref, win32_ref, wz_ref, wal_ref, bal_ref, hn_ref, sm_ref, vm_ref,
                wout32_ref, o_ref, wh_ref, wout_ref, state_ref, y_ref):
    c = ROW_TILE
    d = x_ref.shape[1]
    n_tiles = x_ref.shape[0] // c
    dk, dv = state_ref.shape[1:]
    qk = GLA_HEADS * dk

    @pl.when(_first_step())
    def _():
        wout_ref[...] = wout32_ref[...].astype(BF16)
        for hh in range(GLA_HEADS):
            for dst, src, w in ((0, hh * dk, dk), (dk, qk + hh * dk, dk), (2 * dk, 2 * qk + hh * dv, dv),
                                (2 * dk + dv, 2 * qk + d + hh * dv, dv)):
                wh_ref[hh, :, dst:dst + w] = win32_ref[:, src:src + w].astype(BF16)

    @pl.when(pl.program_id(1) == 0)
    def _():
        state_ref[...] = jnp.zeros_like(state_ref)

    def prologue(t):
        x = x_ref[t * c:(t + 1) * c, :]
        h = _pre_norm(x, ng_ref[0:1, :], mod_ref[1:2, :], mod_ref[0:1, :]).astype(BF16)
        z = _dot(h, wz_ref[...])
        ga = _dot(z.astype(BF16), wal_ref[...]) + bal_ref[...]
        g = (jnp.minimum(ga, 0.0) - jnp.log1p(jnp.exp(-jnp.abs(ga)))) / GLA_TAU
        return x, h, _cumsum_tile_order(g)

    def epilogue(t, x, acc):
        o_ref[t * c:(t + 1) * c, :] = _post_norm(x, acc, ng_ref[1:2, :], mod_ref[2:3, :])

    items = [(t, hh) for t in range(n_tiles) for hh in range(GLA_HEADS)]
    xs, hs, gcs = {}, {}, {}
    xs[0], hs[0], gcs[0] = prologue(0)

    def project_piece(t, hh, j):
        w = 2 * dk if j == 0 else dv
        c0 = 0 if j == 0 else 2 * dk + (j - 1) * dv
        return _dot(hs[t], wh_ref[hh, :, c0:c0 + w])

    def out_piece(t, parts, j):
        w = d // _GLA_OUT_PIECES
        parts.append(_dot(y_ref[t % 2], wout_ref[:, j * w:(j + 1) * w]))
        if j == _GLA_OUT_PIECES - 1:
            epilogue(t, xs.pop(t), jnp.concatenate(parts, axis=1))

    cur = [project_piece(0, 0, j) for j in range(_GLA_PROJ_PIECES)]
    due = None
    for idx, (t, hh) in enumerate(items):
        if hh == MIXER_NEXT_NORM_ITEM and t + 1 < n_tiles:
            xs[t + 1], hs[t + 1], gcs[t + 1] = prologue(t + 1)
        nxt, pending = [], []
        if idx + 1 < len(items):
            pending += [lambda j=j, idx=idx, nxt=nxt: nxt.append(project_piece(*items[idx + 1], j))
                        for j in range(_GLA_PROJ_PIECES)]
        if due is not None:
            parts = []
            pending += [functools.partial(out_piece, due, parts, j) for j in range(_GLA_OUT_PIECES)]
            due = None
        pending.reverse()
        fill = lambda pending=pending: pending.pop()() if pending else None
        qk, v32, r = cur
        scores, state_ref[hh] = _gla_scores(qk, v32, gcs[t][:, hh * dk:(hh + 1) * dk], state_ref[hh], fill)
        y_ref[t % 2, :, hh * dv:(hh + 1) * dv] = _gla_combine(scores, v32, r, hn_ref[...], sm_ref, vm_ref, fill)
        while pending:
            fill()
        if hh == GLA_HEADS - 1:
            due = t
        cur = nxt
    parts = []
    for j in range(_GLA_OUT_PIECES):
        out_piece(due, parts, j)


def _gla_layer(x, mod, ng, params, layer, mixer_index):
    w_in, w_alpha, b_alpha, head_norm, w_out = params
    bsz, s, d = x.shape
    c = ROW_TILE
    rank, qk = w_alpha.shape[1:]
    dk, dv = qk // GLA_HEADS, d // GLA_HEADS
    n_main = 2 * qk + 2 * d
    w_z = jnp.pad(w_in[mixer_index, :, n_main:], ((0, 0), (0, GLA_RANK_PAD - rank))).astype(BF16)
    w_al = jnp.pad(w_alpha[mixer_index], ((0, GLA_RANK_PAD - rank), (0, 0))).astype(BF16)
    stream_masks = jnp.asarray(_gla_stream_masks(c), F32)
    vreg_masks = jnp.asarray(_gla_vreg_masks(), F32)
    n_tiles = TILES_PER_STEP
    tile = pl.BlockSpec((None, n_tiles * c, d), lambda b, i: (b, i, 0))
    row = lambda v: v[:, None, :]
    mix_spec = lambda p: _layer_spec(p.shape, mixer_index)
    return pl.pallas_call(
        _gla_kernel,
        grid=(bsz, s // (n_tiles * c)),
        in_specs=[
            tile,
            pl.BlockSpec((None, None, 6, d), lambda b, i: (layer, b, 0, 0)),
            _layer_spec(ng.shape, layer),
            mix_spec(w_in),
            _const_spec(w_z.shape),
            _const_spec(w_al.shape),
            mix_spec(row(b_alpha)),
            mix_spec(row(head_norm)),
            _const_spec(stream_masks.shape),
            _const_spec(vreg_masks.shape),
            mix_spec(w_out),
        ],
        out_specs=tile,
        out_shape=jax.ShapeDtypeStruct(x.shape, F32),
        scratch_shapes=[
            pltpu.VMEM((GLA_HEADS, d, 2 * dk + 2 * dv), BF16),
            pltpu.VMEM(w_out.shape[1:], BF16),
            pltpu.VMEM((GLA_HEADS, dk, dv), F32),
            pltpu.VMEM((2, c, d), BF16),
        ],
        compiler_params=_params("arbitrary", "arbitrary"),
        name="gla_layer",
    )(x, mod, ng, w_in, w_z, w_al, row(b_alpha), row(head_norm), stream_masks, vreg_masks, w_out)


def kernel(x, c, ada_w, ada_b, norm_g, ffn_w_up, ffn_conv_w, ffn_conv_b, ffn_w_down, rg_w_in, rg_conv_w,
           rg_conv_b, rg_wa, rg_ba, rg_wx, rg_bx, rg_lambda, rg_w_out, gla_w_in, gla_w_alpha, gla_b_alpha,
           gla_norm_g, gla_w_out):
    depth = ada_w.shape[0]
    assert depth >= 1 and x.shape[1] % (ROW_TILE * TILES_PER_STEP) == 0
    mod = _ada_modulation(c, ada_w, ada_b)
    ffn_w_up_bf = ffn_w_up.astype(BF16)
    ffn_conv_b3 = ffn_conv_b[:, None, :]
    rg_params = (rg_w_in, rg_conv_w, rg_conv_b, rg_wa, rg_ba, rg_wx, rg_bx, rg_lambda, rg_w_out)
    gla_params = (gla_w_in, gla_w_alpha, gla_b_alpha, gla_norm_g, gla_w_out)
    for i in range(depth):
        j = i // 2
        if i % 2 == 0:
            x = _rglru_layer(x, mod, norm_g, rg_params, i, j, natural_in=(i == 0))
        else:
            x = _gla_layer(x, mod, norm_g, gla_params, i, j)
        x = _ffn_layer(x, mod, norm_g, ffn_w_up_bf, ffn_conv_w, ffn_conv_b3, ffn_w_down, i,
                       natural_out=(i == depth - 1))
    return x
```

```python
import functools
import math

import jax
import jax.numpy as jnp
import numpy as np
from jax import lax
from jax.experimental import pallas as pl
from jax.experimental.pallas import tpu as pltpu

F32 = jnp.float32
BF16 = jnp.bfloat16

EPS = 1e-6
RG_BLOCKS = 4
RG_C = 8.0
RG_CONV = 4
FFN_CONV = 3
GLA_HEADS = 4
GLA_TAU = 16.0
GLA_RANK_PAD = 128
SUBLANES = 8

ROW_TILE = 256
TILE_VREGS = ROW_TILE // SUBLANES
TILES_PER_STEP = 4
ADA_COL_TILE = 3072
FFN_COL_CHUNK = 256
MIXER_NEXT_NORM_ITEM = 1
FFN_NEXT_NORM_LEAD = 4
VMEM_LIMIT = 56 * 1024 * 1024

_GELU_C0 = 2.0 * math.sqrt(2.0 / math.pi) * math.log2(math.e)
_GELU_C1 = _GELU_C0 * 0.044715


def _dot(a, b):
    return jnp.dot(a, b, preferred_element_type=F32)


def _dot_nt(a, b):
    return lax.dot_general(a, b, (((1,), (1,)), ((), ())), preferred_element_type=F32)


def _rms(x, g):
    return x * lax.rsqrt(jnp.mean(x * x, axis=-1, keepdims=True) + EPS) * g


def _pre_norm(x, g, scale, shift):
    return _rms(x, g * (1.0 + scale)) + shift


def _post_norm(x, y, g, gate):
    return x + _rms(y, gate * g)


def _gelu_times(x, v):
    neg_2z_log2e = x * (-_GELU_C0 - _GELU_C1 * (x * x))
    return (x * v) / (1.0 + jnp.exp2(neg_2z_log2e))


def _sublane(shape):
    return lax.broadcasted_iota(jnp.int32, shape, 0) % SUBLANES


def _vreg(a, j):
    return a[j * SUBLANES:(j + 1) * SUBLANES]


def _load_tile_order(ref):
    return jnp.concatenate([ref[:, j, :] for j in range(TILE_VREGS)], axis=0)


def _store_natural_order(ref, val):
    for j in range(TILE_VREGS):
        ref[:, j, :] = _vreg(val, j)


def _delay(x, tail, k):
    ts = x.shape[0]
    n_tail = tail.shape[0] // SUBLANES
    last = _sublane((SUBLANES, 1)) == SUBLANES - 1
    head = []
    for j in range(k):
        cur = _vreg(x, TILE_VREGS - k + j)
        prv = _vreg(tail, n_tail - k + j)
        head.append(pltpu.roll(jnp.where(last, prv, cur), 1, 0))
    return jnp.concatenate(head + [x[:ts - k * SUBLANES]], axis=0)


def _params(*sem):
    return pltpu.CompilerParams(dimension_semantics=sem, vmem_limit_bytes=VMEM_LIMIT)


def _const_spec(shape):
    n = len(shape)
    return pl.BlockSpec(shape, lambda *_: (0,) * n, pipeline_mode=pl.Buffered(1))


def _layer_spec(shape, layer):
    n = len(shape) - 1
    return pl.BlockSpec((None,) + tuple(shape[1:]), lambda *_: (layer,) + (0,) * n,
                        pipeline_mode=pl.Buffered(1))


def _ada_kernel(c_ref, w_ref, b_ref, o_ref):
    c = c_ref[...]
    c_act = (c * jax.nn.sigmoid(c)).astype(BF16)
    o_ref[...] = _dot(c_act, w_ref[...].astype(BF16)) + b_ref[...]


def _ada_modulation(c, ada_w, ada_b):
    depth, d, n = ada_w.shape
    bsz = c.shape[0]
    tn = ADA_COL_TILE
    out = pl.pallas_call(
        _ada_kernel,
        grid=(depth, n // tn),
        in_specs=[
            pl.BlockSpec((bsz, d), lambda l, j: (0, 0)),
            pl.BlockSpec((None, d, tn), lambda l, j: (l, 0, j)),
            pl.BlockSpec((None, 1, tn), lambda l, j: (l, 0, j)),
        ],
        out_specs=pl.BlockSpec((None, bsz, tn), lambda l, j: (l, 0, j)),
        out_shape=jax.ShapeDtypeStruct((depth, bsz, n), F32),
        compiler_params=_params("arbitrary", "arbitrary"),
        name="ada_mod",
    )(c, ada_w, ada_b.reshape(depth, 1, n))
    return out.reshape(depth, bsz, 6, d)


def _stream_scan(a, u, carry):
    hl = [_vreg(u, 0)]
    al = [_vreg(a, 0)]
    for j in range(1, TILE_VREGS):
        hl.append(_vreg(a, j) * hl[-1] + _vreg(u, j))
        al.append(_vreg(a, j) * al[-1])
    sub = _sublane((SUBLANES, 1))
    ea, eh = al[-1], hl[-1]
    for k in (1, 2, 4):
        keep = sub >= k
        ea_sh = jnp.where(keep, pltpu.roll(ea, k, 0), 1.0)
        eh_sh = jnp.where(keep, pltpu.roll(eh, k, 0), 0.0)
        eh = ea * eh_sh + eh
        ea = ea * ea_sh
    after = ea * carry + eh
    before = jnp.where(sub == 0, carry, pltpu.roll(after, 1, 0))
    h = jnp.concatenate([hl[j] + al[j] * before for j in range(TILE_VREGS)], axis=0)
    return h, after


def _first_step():
    return (pl.program_id(0) == 0) & (pl.program_id(1) == 0)


def _tile_order_gather(x_hbm, buf, sem, b, step, slot):
    n_tiles = buf.shape[1] // ROW_TILE
    return [
        pltpu.make_async_copy(x_hbm.at[b, step * n_tiles + t, :, j, :],
                              buf.at[slot, pl.ds(t * ROW_TILE + j * SUBLANES, SUBLANES), :], sem.at[slot])
        for t in range(n_tiles) for j in range(TILE_VREGS)]


def _rglru_kernel(natural_in, x_ref, mod_ref, ng_ref, win32_ref, cw_ref, cb_ref, ba_ref, bx_ref, lam_ref,
                  wa32_ref, wx32_ref, wout32_ref, o_ref, win_ref, wg_ref, wout_ref, tail_ref, hc_ref, y_ref,
                  *gather_scratch):
    ts = ROW_TILE
    d = o_ref.shape[1]
    n_tiles = o_ref.shape[0] // ts
    blk = d // RG_BLOCKS
    n_tail = RG_CONV - 1

    if natural_in:
        xbuf, sem = gather_scratch
        b, i = pl.program_id(0), pl.program_id(1)
        n_steps = pl.num_programs(1)
        linear = b * n_steps + i
        slot = linear % 2

        @pl.when(linear == 0)
        def _():
            for cp in _tile_order_gather(x_ref, xbuf, sem, b, i, slot):
                cp.start()

        @pl.when(linear + 1 < pl.num_programs(0) * n_steps)
        def _():
            wrap = i + 1 == n_steps
            for cp in _tile_order_gather(x_ref, xbuf, sem, jnp.where(wrap, b + 1, b), jnp.where(wrap, 0, i + 1),
                                         1 - slot):
                cp.start()

        for cp in _tile_order_gather(x_ref, xbuf, sem, b, i, slot):
            cp.wait()

    @pl.when(_first_step())
    def _():
        win_ref[...] = win32_ref[...].astype(BF16)
        wout_ref[...] = wout32_ref[...].astype(BF16)
        for g in range(RG_BLOCKS):
            wg_ref[g, :, :blk] = (0.5 * wa32_ref[g]).astype(BF16)
            wg_ref[g, :, blk:] = (0.5 * wx32_ref[g]).astype(BF16)

    @pl.when(pl.program_id(1) == 0)
    def _():
        tail_ref[...] = jnp.zeros_like(tail_ref)
        hc_ref[...] = jnp.zeros_like(hc_ref)

    def prologue(t):
        x = xbuf[slot, t * ts:(t + 1) * ts, :] if natural_in else x_ref[t * ts:(t + 1) * ts, :]
        return x, _pre_norm(x, ng_ref[0:1, :], mod_ref[1:2, :], mod_ref[0:1, :]).astype(BF16)

    def epilogue(t, x, acc):
        o_ref[t * ts:(t + 1) * ts, :] = _post_norm(x, acc, ng_ref[1:2, :], mod_ref[2:3, :])

    def project(t, g):
        return (_dot(hs[t], win_ref[:, g * blk:(g + 1) * blk]),
                _dot(hs[t], win_ref[:, d + g * blk:d + (g + 1) * blk]))

    def mix(g, gate_br, xb):
        cols = slice(g * blk, (g + 1) * blk)
        tail = tail_ref[:, cols]
        tail_ref[:, cols] = xb[ts - n_tail * SUBLANES:, :]
        xc = cb_ref[:, cols] + _delay(xb, tail, 3) * cw_ref[0:1, cols]
        xc = xc + _delay(xb, tail, 2) * cw_ref[1:2, cols]
        xc = xc + _delay(xb, tail, 1) * cw_ref[2:3, cols]
        xc = xc + xb * cw_ref[3:4, cols]
        p = _dot(xc.astype(BF16), wg_ref[g])
        i_g = 0.5 * jnp.tanh(p[:, blk:] + 0.5 * bx_ref[:, cols]) + 0.5
        nl = -lam_ref[:, cols]
        softplus_nl = jnp.maximum(nl, 0.0) + jnp.log1p(jnp.exp(-jnp.abs(nl)))
        half_rate = (-0.5 * RG_C) * softplus_nl
        log_a = jnp.tanh(p[:, :blk] + 0.5 * ba_ref[:, cols]) * half_rate + half_rate
        a = jnp.exp(log_a)
        m2 = jnp.tanh(log_a) * (-1.0 - a * a)
        u = jnp.where(m2 > 0.0, m2 * lax.rsqrt(m2), 0.0) * (i_g * xc)
        hs_g, after = _stream_scan(a, u, hc_ref[SUBLANES - 1:SUBLANES, cols])
        hc_ref[:, cols] = after
        return _gelu_times(gate_br, hs_g).astype(BF16)

    items = [(t, g) for t in range(n_tiles) for g in range(RG_BLOCKS)]
    xs, hs = {}, {}
    xs[0], hs[0] = prologue(0)

    def out_project(t):
        epilogue(t, xs.pop(t), _dot(y_ref[t % 2], wout_ref[...]))

    cur = project(0, 0)
    due = None
    for idx, (t, g) in enumerate(items):
        if g == MIXER_NEXT_NORM_ITEM and t + 1 < n_tiles:
            xs[t + 1], hs[t + 1] = prologue(t + 1)
        nxt = project(*items[idx + 1]) if idx + 1 < len(items) else None
        if due is not None:
            out_project(due)
            due = None
        y_ref[t % 2, :, g * blk:(g + 1) * blk] = mix(g, *cur)
        if g == RG_BLOCKS - 1:
            due = t
        cur = nxt
    out_project(due)


def _rglru_layer(x, mod, ng, params, layer, mixer_index, natural_in):
    w_in, conv_w, conv_b, wa, ba, wx, bx, lam, w_out = params
    bsz, s, d = x.shape
    ts = ROW_TILE
    n_tiles = TILES_PER_STEP
    blk = d // RG_BLOCKS
    tile = pl.BlockSpec((None, n_tiles * ts, d), lambda b, i: (b, i, 0))
    x_tile, gather_scratch = tile, []
    if natural_in:
        x = x.reshape(bsz, s // ts, SUBLANES, TILE_VREGS, d)
        x_tile = pl.BlockSpec(memory_space=pl.ANY)
        gather_scratch = [pltpu.VMEM((2, n_tiles * ts, d), F32), pltpu.SemaphoreType.DMA((2,))]
    row = lambda v: v[:, None, :]
    mix_spec = lambda p: _layer_spec(p.shape, mixer_index)
    return pl.pallas_call(
        functools.partial(_rglru_kernel, natural_in),
        grid=(bsz, s // (n_tiles * ts)),
        in_specs=[
            x_tile,
            pl.BlockSpec((None, None, 6, d), lambda b, i: (layer, b, 0, 0)),
            _layer_spec(ng.shape, layer),
            mix_spec(w_in), mix_spec(conv_w), mix_spec(row(conv_b)), mix_spec(row(ba)), mix_spec(row(bx)),
            mix_spec(row(lam)), mix_spec(wa), mix_spec(wx), mix_spec(w_out),
        ],
        out_specs=tile,
        out_shape=jax.ShapeDtypeStruct((bsz, s, d), F32),
        scratch_shapes=[
            pltpu.VMEM(w_in.shape[1:], BF16),
            pltpu.VMEM((RG_BLOCKS, blk, 2 * blk), BF16),
            pltpu.VMEM(w_out.shape[1:], BF16),
            pltpu.VMEM(((RG_CONV - 1) * SUBLANES, d), F32),
            pltpu.VMEM((SUBLANES, d), F32),
            pltpu.VMEM((2, ts, d), BF16),
        ] + gather_scratch,
        compiler_params=_params("arbitrary", "arbitrary"),
        name="rglru_layer",
    )(x, mod, ng, w_in, conv_w, row(conv_b), row(ba), row(bx), row(lam), wa, wx, w_out)


def _ffn_kernel(natural_out, x_ref, mod_ref, ng_ref, wup_ref, cw_ref, cb_ref, wdn32_ref, o_ref, wdn_ref,
                tail_ref, act_ref):
    ts = ROW_TILE
    n_tiles = x_ref.shape[0] // ts
    f = wdn_ref.shape[0]
    fc = FFN_COL_CHUNK
    n_chunks = f // fc
    n_tail = FFN_CONV - 1

    @pl.when(_first_step())
    def _():
        wdn_ref[...] = wdn32_ref[...].astype(BF16)

    @pl.when(pl.program_id(1) == 0)
    def _():
        tail_ref[...] = jnp.zeros_like(tail_ref)

    def prologue(t):
        x = x_ref[t * ts:(t + 1) * ts, :]
        return x, _pre_norm(x, ng_ref[2:3, :], mod_ref[4:5, :], mod_ref[3:4, :]).astype(BF16)

    def epilogue(t, x, acc):
        out = _post_norm(x, acc, ng_ref[3:4, :], mod_ref[5:6, :])
        if natural_out:
            _store_natural_order(o_ref.at[t], out)
        else:
            o_ref[t * ts:(t + 1) * ts, :] = out

    def conv_cols(up, c0):
        cols = slice(c0, c0 + fc)
        tail = tail_ref[:, cols]
        tail_ref[:, cols] = up[ts - n_tail * SUBLANES:, :]
        y = cb_ref[:, cols] + _delay(up, tail, 2) * cw_ref[0:1, cols]
        y = y + _delay(up, tail, 1) * cw_ref[1:2, cols]
        return y + up * cw_ref[2:3, cols]

    items = [(t, c) for t in range(n_tiles) for c in range(n_chunks)]
    xs, hs = {}, {}
    xs[0], hs[0] = prologue(0)

    def up_pair(t, c):
        return (_dot(hs[t], wup_ref[:, c * fc:(c + 1) * fc]), _dot(hs[t], wup_ref[:, f + c * fc:f + (c + 1) * fc]))

    def down(t):
        epilogue(t, xs.pop(t), _dot(act_ref[t % 2], wdn_ref[...]))

    ups = up_pair(0, 0)
    due = None
    for idx, (t, c) in enumerate(items):
        if c == n_chunks - FFN_NEXT_NORM_LEAD and t + 1 < n_tiles:
            xs[t + 1], hs[t + 1] = prologue(t + 1)
        nxt = up_pair(*items[idx + 1]) if idx + 1 < len(items) else None
        if due is not None:
            down(due)
            due = None
        g = conv_cols(ups[0], c * fc)
        val = conv_cols(ups[1], f + c * fc)
        act_ref[t % 2, :, c * fc:(c + 1) * fc] = _gelu_times(g, val).astype(BF16)
        if c == n_chunks - 1:
            due = t
        ups = nxt
    down(due)


def _ffn_layer(x, mod, ng, w_up, conv_w, conv_b, w_down, layer, natural_out):
    bsz, s, d = x.shape
    ts = ROW_TILE
    n_tiles = TILES_PER_STEP
    rows = pl.BlockSpec((None, n_tiles * ts, d), lambda b, i: (b, i, 0))
    out_tile, out_shape = rows, (bsz, s, d)
    if natural_out:
        out_tile = pl.BlockSpec((None, n_tiles, SUBLANES, TILE_VREGS, d), lambda b, i: (b, i, 0, 0, 0))
        out_shape = (bsz, s // ts, SUBLANES, TILE_VREGS, d)
    out = pl.pallas_call(
        functools.partial(_ffn_kernel, natural_out),
        grid=(bsz, s // (n_tiles * ts)),
        in_specs=[
            rows,
            pl.BlockSpec((None, None, 6, d), lambda b, i: (layer, b, 0, 0)),
            _layer_spec(ng.shape, layer),
            _layer_spec(w_up.shape, layer),
            _layer_spec(conv_w.shape, layer),
            _layer_spec(conv_b.shape, layer),
            _layer_spec(w_down.shape, layer),
        ],
        out_specs=out_tile,
        out_shape=jax.ShapeDtypeStruct(out_shape, F32),
        scratch_shapes=[
            pltpu.VMEM(w_down.shape[1:], BF16),
            pltpu.VMEM(((FFN_CONV - 1) * SUBLANES, w_up.shape[2]), F32),
            pltpu.VMEM((2, ts, w_down.shape[1]), BF16),
        ],
        compiler_params=_params("arbitrary", "arbitrary"),
        name="ffn_layer",
    )(x, mod, ng, w_up, conv_w, conv_b, w_down)
    return out.reshape(bsz, s, d)


_STREAM_LEVELS = tuple(TILE_VREGS << i for i in (2, 1, 0))
_VREG_LEVELS = tuple(TILE_VREGS >> i for i in range(1, 4))
_DIRECT_BLOCK = 4
_MASKED_STREAM_LEVELS = _STREAM_LEVELS[1:]
_EXCLUDE = 1e30
_GLA_PROJ_PIECES = 3
_GLA_OUT_PIECES = 2


def _tile_order_times(c):
    p = np.arange(c)
    return (p % SUBLANES) * (c // SUBLANES) + p // SUBLANES


def _gla_stream_masks(c):
    t = _tile_order_times(c)
    return np.stack([(t[:, None] // (2 * b) == t[None, :] // (2 * b)) for b in _MASKED_STREAM_LEVELS]
                    ).astype(np.float32)


def _vreg_level_rows(b):
    upper = [j for j in range(TILE_VREGS) if (j // b) % 2 == 1]
    lower = [j for j in range(TILE_VREGS) if (j // b) % 2 == 0]
    return upper, lower


def _gla_vreg_masks():
    out = []
    s = np.arange(SUBLANES)
    for b in _VREG_LEVELS:
        upper, lower = _vreg_level_rows(b)
        ju = np.repeat(np.array(upper), SUBLANES)[:, None]
        jl = np.repeat(np.array(lower), SUBLANES)[None, :]
        su = np.tile(s, len(upper))[:, None]
        sl = np.tile(s, len(lower))[None, :]
        out.append((su == sl) & (ju // (2 * b) == jl // (2 * b)))
    return np.stack(out).astype(np.float32)


def _cumsum_tile_order(g):
    rows = [_vreg(g, 0)]
    for j in range(1, TILE_VREGS):
        rows.append(rows[-1] + _vreg(g, j))
    total = rows[-1]
    sub = _sublane((SUBLANES, 1))
    incl = total
    for k in (1, 2, 4):
        incl = incl + jnp.where(sub >= k, pltpu.roll(incl, k, 0), 0.0)
    before = incl - total
    return jnp.concatenate([r + before for r in rows], axis=0)


def _gla_scores(qk, v32, gc, state, fill):
    c, dk = gc.shape
    vreg = _vreg
    q = qk[:, :dk] * (dk ** -0.5)
    k = qk[:, dk:]
    v = v32.astype(BF16)
    g_last = gc[c - 1:c, :]

    fill()
    o_inter = _dot((q * jnp.exp(gc)).astype(BF16), state.astype(BF16))

    sub = _sublane((SUBLANES, 1))
    g_end = vreg(gc, TILE_VREGS - 1)
    tile_rows = lambda a: jnp.concatenate([a] * TILE_VREGS, axis=0)
    p_stream = []
    for b in _STREAM_LEVELS:
        w = b // TILE_VREGS
        g_mid = None
        for first in range(0, SUBLANES, 2 * w):
            row = jnp.broadcast_to(g_end[first + w - 1:first + w, :], (SUBLANES, dk))
            g_mid = row if g_mid is None else jnp.where(sub >= first, row, g_mid)
        upper = (lax.shift_right_logical(sub, w.bit_length() - 1) & 1) == 1
        g_q = tile_rows(jnp.where(upper, g_mid, _EXCLUDE))
        g_k = tile_rows(jnp.where(upper, -_EXCLUDE, g_mid))
        q_b = (q * jnp.exp(gc - g_q)).astype(BF16)
        k_b = (k * jnp.exp(g_k - gc)).astype(BF16)
        p_stream.append(_dot_nt(q_b, k_b))
    fill()

    p_vreg = []
    for b in _VREG_LEVELS:
        upper_rows, _ = _vreg_level_rows(b)
        q_rows, k_rows = [], []
        for j in range(TILE_VREGS):
            jm = (j // (2 * b)) * (2 * b) + b - 1
            if j in upper_rows:
                q_rows.append(vreg(q, j) * jnp.exp(vreg(gc, j) - vreg(gc, jm)))
            elif j == jm:
                k_rows.append(vreg(k, j))
            else:
                k_rows.append(vreg(k, j) * jnp.exp(vreg(gc, jm) - vreg(gc, j)))
        p_vreg.append(_dot_nt(jnp.concatenate(q_rows, axis=0).astype(BF16),
                              jnp.concatenate(k_rows, axis=0).astype(BF16)))
    fill()

    k_rev = (k * jnp.exp(g_last - gc)).astype(BF16)
    upd = lax.dot_general(k_rev, v, (((0,), (0,)), ((), ())), preferred_element_type=F32)
    decay_rows = jnp.broadcast_to(jnp.exp(g_last), (dk, dk))
    decay_col = jnp.transpose(decay_rows)[:, 0:1]
    new_state = decay_col * state + upd

    direct = []
    for ju in range(TILE_VREGS):
        terms = vreg(v32, ju) * jnp.sum(vreg(q, ju) * vreg(k, ju), axis=-1, keepdims=True)
        for jl in range((ju // _DIRECT_BLOCK) * _DIRECT_BLOCK, ju):
            w = vreg(q, ju) * vreg(k, jl) * jnp.exp(vreg(gc, ju) - vreg(gc, jl))
            terms = terms + vreg(v32, jl) * jnp.sum(w, axis=-1, keepdims=True)
        direct.append(terms)
    return (o_inter, p_stream, p_vreg, direct), new_state


def _gla_combine(scores, v32, r, hn, sm_ref, vm_ref, fill):
    o, p_stream, p_vreg, direct = scores
    v = v32.astype(BF16)

    attn = None
    for b, p in zip(_STREAM_LEVELS, p_stream):
        if b in _MASKED_STREAM_LEVELS:
            p = p * sm_ref[_MASKED_STREAM_LEVELS.index(b)]
        attn = p if attn is None else attn + p
    o = o + _dot(attn.astype(BF16), v)
    fill()

    o_rows = [_vreg(o, j) + direct[j] for j in range(TILE_VREGS)]
    for li, (b, p) in enumerate(zip(_VREG_LEVELS, p_vreg)):
        upper_rows, _ = _vreg_level_rows(b)
        p = (p * vm_ref[li]).astype(BF16)
        v_low = jnp.concatenate(
            [v[j0 * SUBLANES:(j0 + b) * SUBLANES] for j0 in range(0, TILE_VREGS, 2 * b)], axis=0)
        o_up = _dot(p, v_low)
        if li == len(_VREG_LEVELS) // 2:
            fill()
        for idx, j in enumerate(upper_rows):
            o_rows[j] = o_rows[j] + _vreg(o_up, idx)
    o = jnp.concatenate(o_rows, axis=0)
    half_r = 0.5 * r
    return (_rms(o, hn) * (half_r * jnp.tanh(half_r) + half_r)).astype(BF16)


def _gla_kernel(x_ref, mod_ref, ng_ref, win32_ref, wz_ref, wal_ref, bal_ref, hn_ref, sm_ref, vm_ref,
                wout32_ref, o_ref, wh_ref, wout_ref, state_ref, y_ref):
    c = ROW_TILE
    d = x_ref.shape[1]
    n_tiles = x_ref.shape[0] // c
    dk, dv = state_ref.shape[1:]
    qk = GLA_HEADS * dk

    @pl.when(_first_step())
    def _():
        wout_ref[...] = wout32_ref[...].astype(BF16)
        for hh in range(GLA_HEADS):
            for dst, src, w in ((0, hh * dk, dk), (dk, qk + hh * dk, dk), (2 * dk, 2 * qk + hh * dv, dv),
                                (2 * dk + dv, 2 * qk + d + hh * dv, dv)):
                wh_ref[hh, :, dst:dst + w] = win32_ref[:, src:src + w].astype(BF16)

    @pl.when(pl.program_id(1) == 0)
    def _():
        state_ref[...] = jnp.zeros_like(state_ref)

    def prologue(t):
        x = x_ref[t * c:(t + 1) * c, :]
        h = _pre_norm(x, ng_ref[0:1, :], mod_ref[1:2, :], mod_ref[0:1, :]).astype(BF16)
        z = _dot(h, wz_ref[...])
        ga = _dot(z.astype(BF16), wal_ref[...]) + bal_ref[...]
        g = (jnp.minimum(ga, 0.0) - jnp.log1p(jnp.exp(-jnp.abs(ga)))) / GLA_TAU
        return x, h, _cumsum_tile_order(g)

    def epilogue(t, x, acc):
        o_ref[t * c:(t + 1) * c, :] = _post_norm(x, acc, ng_ref[1:2, :], mod_ref[2:3, :])

    items = [(t, hh) for t in range(n_tiles) for hh in range(GLA_HEADS)]
    xs, hs, gcs = {}, {}, {}
    xs[0], hs[0], gcs[0] = prologue(0)

    def project_piece(t, hh, j):
        w = 2 * dk if j == 0 else dv
        c0 = 0 if j == 0 else 2 * dk + (j - 1) * dv
        return _dot(hs[t], wh_ref[hh, :, c0:c0 + w])

    def out_piece(t, parts, j):
        w = d // _GLA_OUT_PIECES
        parts.append(_dot(y_ref[t % 2], wout_ref[:, j * w:(j + 1) * w]))
        if j == _GLA_OUT_PIECES - 1:
            epilogue(t, xs.pop(t), jnp.concatenate(parts, axis=1))

    cur = [project_piece(0, 0, j) for j in range(_GLA_PROJ_PIECES)]
    due = None
    for idx, (t, hh) in enumerate(items):
        if hh == MIXER_NEXT_NORM_ITEM and t + 1 < n_tiles:
            xs[t + 1], hs[t + 1], gcs[t + 1] = prologue(t + 1)
        nxt, pending = [], []
        if idx + 1 < len(items):
            pending += [lambda j=j, idx=idx, nxt=nxt: nxt.append(project_piece(*items[idx + 1], j))
                        for j in range(_GLA_PROJ_PIECES)]
        if due is not None:
            parts = []
            pending += [functools.partial(out_piece, due, parts, j) for j in range(_GLA_OUT_PIECES)]
            due = None
        pending.reverse()
        fill = lambda pending=pending: pending.pop()() if pending else None
        qk, v32, r = cur
        scores, state_ref[hh] = _gla_scores(qk, v32, gcs[t][:, hh * dk:(hh + 1) * dk], state_ref[hh], fill)
        y_ref[t % 2, :, hh * dv:(hh + 1) * dv] = _gla_combine(scores, v32, r, hn_ref[...], sm_ref, vm_ref, fill)
        while pending:
            fill()
        if hh == GLA_HEADS - 1:
            due = t
        cur = nxt
    parts = []
    for j in range(_GLA_OUT_PIECES):
        out_piece(due, parts, j)


def _gla_layer(x, mod, ng, params, layer, mixer_index):
    w_in, w_alpha, b_alpha, head_norm, w_out = params
    bsz, s, d = x.shape
    c = ROW_TILE
    rank, qk = w_alpha.shape[1:]
    dk, dv = qk // GLA_HEADS, d // GLA_HEADS
    n_main = 2 * qk + 2 * d
    w_z = jnp.pad(w_in[mixer_index, :, n_main:], ((0, 0), (0, GLA_RANK_PAD - rank))).astype(BF16)
    w_al = jnp.pad(w_alpha[mixer_index], ((0, GLA_RANK_PAD - rank), (0, 0))).astype(BF16)
    stream_masks = jnp.asarray(_gla_stream_masks(c), F32)
    vreg_masks = jnp.asarray(_gla_vreg_masks(), F32)
    n_tiles = TILES_PER_STEP
    tile = pl.BlockSpec((None, n_tiles * c, d), lambda b, i: (b, i, 0))
    row = lambda v: v[:, None, :]
    mix_spec = lambda p: _layer_spec(p.shape, mixer_index)
    return pl.pallas_call(
        _gla_kernel,
        grid=(bsz, s // (n_tiles * c)),
        in_specs=[
            tile,
            pl.BlockSpec((None, None, 6, d), lambda b, i: (layer, b, 0, 0)),
            _layer_spec(ng.shape, layer),
            mix_spec(w_in),
            _const_spec(w_z.shape),
            _const_spec(w_al.shape),
            mix_spec(row(b_alpha)),
            mix_spec(row(head_norm)),
            _const_spec(stream_masks.shape),
            _const_spec(vreg_masks.shape),
            mix_spec(w_out),
        ],
        out_specs=tile,
        out_shape=jax.ShapeDtypeStruct(x.shape, F32),
        scratch_shapes=[
            pltpu.VMEM((GLA_HEADS, d, 2 * dk + 2 * dv), BF16),
            pltpu.VMEM(w_out.shape[1:], BF16),
            pltpu.VMEM((GLA_HEADS, dk, dv), F32),
            pltpu.VMEM((2, c, d), BF16),
        ],
        compiler_params=_params("arbitrary", "arbitrary"),
        name="gla_layer",
    )(x, mod, ng, w_in, w_z, w_al, row(b_alpha), row(head_norm), stream_masks, vreg_masks, w_out)


def kernel(x, c, ada_w, ada_b, norm_g, ffn_w_up, ffn_conv_w, ffn_conv_b, ffn_w_down, rg_w_in, rg_conv_w,
           rg_conv_b, rg_wa, rg_ba, rg_wx, rg_bx, rg_lambda, rg_w_out, gla_w_in, gla_w_alpha, gla_b_alpha,
           gla_norm_g, gla_w_out):
    depth = ada_w.shape[0]
    assert depth >= 1 and x.shape[1] % (ROW_TILE * TILES_PER_STEP) == 0
    mod = _ada_modulation(c, ada_w, ada_b)
    ffn_w_up_bf = ffn_w_up.astype(BF16)
    ffn_conv_b3 = ffn_conv_b[:, None, :]
    rg_params = (rg_w_in, rg_conv_w, rg_conv_b, rg_wa, rg_ba, rg_wx, rg_bx, rg_lambda, rg_w_out)
    gla_params = (gla_w_in, gla_w_alpha, gla_b_alpha, gla_norm_g, gla_w_out)
    for i in range(depth):
        j = i // 2
        if i % 2 == 0:
            x = _rglru_layer(x, mod, norm_g, rg_params, i, j, natural_in=(i == 0))
        else:
            x = _gla_layer(x, mod, norm_g, gla_params, i, j)
        x = _ffn_layer(x, mod, norm_g, ffn_w_up_bf, ffn_conv_w, ffn_conv_b3, ffn_w_down, i,
                       natural_out=(i == depth - 1))
    return x
```

```python
import functools
import math

import jax
import jax.numpy as jnp
import numpy as np
from jax import lax
from jax.experimental import pallas as pl
from jax.experimental.pallas import tpu as pltpu

F32 = jnp.float32
BF16 = jnp.bfloat16

EPS = 1e-6
RG_BLOCKS = 4
RG_C = 8.0
RG_CONV = 4
FFN_CONV = 3
GLA_HEADS = 4
GLA_TAU = 16.0
GLA_RANK_PAD = 128
SUBLANES = 8

ROW_TILE = 256
TILE_VREGS = ROW_TILE // SUBLANES
TILES_PER_STEP = 4
ADA_COL_TILE = 3072
FFN_COL_CHUNK = 256
MIXER_NEXT_NORM_ITEM = 1
FFN_NEXT_NORM_LEAD = 4
VMEM_LIMIT = 56 * 1024 * 1024

_GELU_C0 = 2.0 * math.sqrt(2.0 / math.pi) * math.log2(math.e)
_GELU_C1 = _GELU_C0 * 0.044715


def _dot(a, b):
    return jnp.dot(a, b, preferred_element_type=F32)


def _dot_nt(a, b):
    return lax.dot_general(a, b, (((1,), (1,)), ((), ())), preferred_element_type=F32)


def _rms(x, g):
    return x * lax.rsqrt(jnp.mean(x * x, axis=-1, keepdims=True) + EPS) * g


def _pre_norm(x, g, scale, shift):
    return _rms(x, g * (1.0 + scale)) + shift


def _post_norm(x, y, g, gate):
    return x + _rms(y, gate * g)


def _sigmoid(x):
    return 0.5 * jnp.tanh(0.5 * x) + 0.5


def _gelu_times(x, v):
    neg_2z_log2e = x * (-_GELU_C0 - _GELU_C1 * (x * x))
    return (x * v) / (1.0 + jnp.exp2(neg_2z_log2e))


def _sublane(shape):
    return lax.broadcasted_iota(jnp.int32, shape, 0) % SUBLANES


def _vreg(a, j):
    return a[j * SUBLANES:(j + 1) * SUBLANES]


def _load_tile_order(ref):
    return jnp.concatenate([ref[:, j, :] for j in range(TILE_VREGS)], axis=0)


def _store_natural_order(ref, val):
    for j in range(TILE_VREGS):
        ref[:, j, :] = _vreg(val, j)


def _delay(x, tail, k):
    ts = x.shape[0]
    n_tail = tail.shape[0] // SUBLANES
    last = _sublane((SUBLANES, 1)) == SUBLANES - 1
    head = []
    for j in range(k):
        cur = _vreg(x, TILE_VREGS - k + j)
        prv = _vreg(tail, n_tail - k + j)
        head.append(pltpu.roll(jnp.where(last, prv, cur), 1, 0))
    return jnp.concatenate(head + [x[:ts - k * SUBLANES]], axis=0)


def _params(*sem):
    return pltpu.CompilerParams(dimension_semantics=sem, vmem_limit_bytes=VMEM_LIMIT)


def _const_spec(shape):
    n = len(shape)
    return pl.BlockSpec(shape, lambda *_: (0,) * n, pipeline_mode=pl.Buffered(1))


def _layer_spec(shape, layer):
    n = len(shape) - 1
    return pl.BlockSpec((None,) + tuple(shape[1:]), lambda *_: (layer,) + (0,) * n,
                        pipeline_mode=pl.Buffered(1))


def _ada_kernel(c_ref, w_ref, b_ref, o_ref):
    c = c_ref[...]
    c_act = (c * jax.nn.sigmoid(c)).astype(BF16)
    o_ref[...] = _dot(c_act, w_ref[...].astype(BF16)) + b_ref[...]


def _ada_modulation(c, ada_w, ada_b):
    depth, d, n = ada_w.shape
    bsz = c.shape[0]
    tn = ADA_COL_TILE
    out = pl.pallas_call(
        _ada_kernel,
        grid=(depth, n // tn),
        in_specs=[
            pl.BlockSpec((bsz, d), lambda l, j: (0, 0)),
            pl.BlockSpec((None, d, tn), lambda l, j: (l, 0, j)),
            pl.BlockSpec((None, 1, tn), lambda l, j: (l, 0, j)),
        ],
        out_specs=pl.BlockSpec((None, bsz, tn), lambda l, j: (l, 0, j)),
        out_shape=jax.ShapeDtypeStruct((depth, bsz, n), F32),
        compiler_params=_params("arbitrary", "arbitrary"),
        name="ada_mod",
    )(c, ada_w, ada_b.reshape(depth, 1, n))
    return out.reshape(depth, bsz, 6, d)


def _stream_scan(a, u, carry):
    hl = [_vreg(u, 0)]
    al = [_vreg(a, 0)]
    for j in range(1, TILE_VREGS):
        hl.append(_vreg(a, j) * hl[-1] + _vreg(u, j))
        al.append(_vreg(a, j) * al[-1])
    sub = _sublane((SUBLANES, 1))
    ea, eh = al[-1], hl[-1]
    for k in (1, 2, 4):
        keep = sub >= k
        ea_sh = jnp.where(keep, pltpu.roll(ea, k, 0), 1.0)
        eh_sh = jnp.where(keep, pltpu.roll(eh, k, 0), 0.0)
        eh = ea * eh_sh + eh
        ea = ea * ea_sh
    after = ea * carry + eh
    before = jnp.where(sub == 0, carry, pltpu.roll(after, 1, 0))
    h = jnp.concatenate([hl[j] + al[j] * before for j in range(TILE_VREGS)], axis=0)
    return h, after


def _first_step():
    return (pl.program_id(0) == 0) & (pl.program_id(1) == 0)


def _tile_order_gather(x_hbm, buf, sem, b, step, slot):
    n_tiles = buf.shape[1] // ROW_TILE
    return [
        pltpu.make_async_copy(x_hbm.at[b, step * n_tiles + t, :, j, :],
                              buf.at[slot, pl.ds(t * ROW_TILE + j * SUBLANES, SUBLANES), :], sem.at[slot])
        for t in range(n_tiles) for j in range(TILE_VREGS)]


def _rglru_kernel(natural_in, x_ref, mod_ref, ng_ref, win32_ref, cw_ref, cb_ref, ba_ref, bx_ref, lam_ref,
                  wa32_ref, wx32_ref, wout32_ref, o_ref, win_ref, wg_ref, wout_ref, tail_ref, hc_ref, y_ref,
                  *gather_scratch):
    ts = ROW_TILE
    d = o_ref.shape[1]
    n_tiles = o_ref.shape[0] // ts
    blk = d // RG_BLOCKS
    n_tail = RG_CONV - 1

    if natural_in:
        xbuf, sem = gather_scratch
        b, i = pl.program_id(0), pl.program_id(1)
        n_steps = pl.num_programs(1)
        linear = b * n_steps + i
        slot = linear % 2

        @pl.when(linear == 0)
        def _():
            for n, cp in enumerate(_tile_order_gather(x_ref, xbuf, sem, b, i, slot)):
                cp.start(priority=n % 2)

        @pl.when(linear + 1 < pl.num_programs(0) * n_steps)
        def _():
            wrap = i + 1 == n_steps
            for n, cp in enumerate(_tile_order_gather(x_ref, xbuf, sem, jnp.where(wrap, b + 1, b),
                                                      jnp.where(wrap, 0, i + 1), 1 - slot)):
                cp.start(priority=n % 2)

        for cp in _tile_order_gather(x_ref, xbuf, sem, b, i, slot):
            cp.wait()

    @pl.when(_first_step())
    def _():
        win_ref[...] = win32_ref[...].astype(BF16)
        wout_ref[...] = wout32_ref[...].astype(BF16)
        for g in range(RG_BLOCKS):
            wg_ref[g, :, :blk] = wa32_ref[g].astype(BF16)
            wg_ref[g, :, blk:] = wx32_ref[g].astype(BF16)

    @pl.when(pl.program_id(1) == 0)
    def _():
        tail_ref[...] = jnp.zeros_like(tail_ref)
        hc_ref[...] = jnp.zeros_like(hc_ref)

    def prologue(t):
        x = xbuf[slot, t * ts:(t + 1) * ts, :] if natural_in else x_ref[t * ts:(t + 1) * ts, :]
        return x, _pre_norm(x, ng_ref[0:1, :], mod_ref[1:2, :], mod_ref[0:1, :]).astype(BF16)

    def epilogue(t, x, acc):
        o_ref[t * ts:(t + 1) * ts, :] = _post_norm(x, acc, ng_ref[1:2, :], mod_ref[2:3, :])

    def project(t, g):
        return (_dot(hs[t], win_ref[:, g * blk:(g + 1) * blk]),
                _dot(hs[t], win_ref[:, d + g * blk:d + (g + 1) * blk]))

    def mix(g, gate_br, xb):
        cols = slice(g * blk, (g + 1) * blk)
        tail = tail_ref[:, cols]
        tail_ref[:, cols] = xb[ts - n_tail * SUBLANES:, :]
        xc = cb_ref[:, cols] + _delay(xb, tail, 3) * cw_ref[0:1, cols]
        xc = xc + _delay(xb, tail, 2) * cw_ref[1:2, cols]
        xc = xc + _delay(xb, tail, 1) * cw_ref[2:3, cols]
        xc = xc + xb * cw_ref[3:4, cols]
        p = _dot(xc.astype(BF16), wg_ref[g])
        i_g = _sigmoid(p[:, blk:] + bx_ref[:, cols])
        nl = -lam_ref[:, cols]
        softplus_nl = jnp.maximum(nl, 0.0) + jnp.log1p(jnp.exp(-jnp.abs(nl)))
        half_rate = (-0.5 * RG_C) * softplus_nl
        log_a = jnp.tanh(0.5 * (p[:, :blk] + ba_ref[:, cols])) * half_rate + half_rate
        a = jnp.exp(log_a)
        m2 = jnp.tanh(log_a) * (-1.0 - a * a)
        u = jnp.where(m2 > 0.0, m2 * lax.rsqrt(m2), 0.0) * (i_g * xc)
        hs_g, after = _stream_scan(a, u, hc_ref[SUBLANES - 1:SUBLANES, cols])
        hc_ref[:, cols] = after
        return _gelu_times(gate_br, hs_g).astype(BF16)

    items = [(t, g) for t in range(n_tiles) for g in range(RG_BLOCKS)]
    xs, hs = {}, {}
    xs[0], hs[0] = prologue(0)

    def out_project(t):
        epilogue(t, xs.pop(t), _dot(y_ref[t % 2], wout_ref[...]))

    cur = project(0, 0)
    due = None
    for idx, (t, g) in enumerate(items):
        if g == MIXER_NEXT_NORM_ITEM and t + 1 < n_tiles:
            xs[t + 1], hs[t + 1] = prologue(t + 1)
        nxt = project(*items[idx + 1]) if idx + 1 < len(items) else None
        if due is not None:
            out_project(due)
            due = None
        y_ref[t % 2, :, g * blk:(g + 1) * blk] = mix(g, *cur)
        if g == RG_BLOCKS - 1:
            due = t
        cur = nxt
    out_project(due)


def _rglru_layer(x, mod, ng, params, layer, mixer_index, natural_in):
    w_in, conv_w, conv_b, wa, ba, wx, bx, lam, w_out = params
    bsz, s, d = x.shape
    ts = ROW_TILE
    n_tiles = TILES_PER_STEP
    blk = d // RG_BLOCKS
    tile = pl.BlockSpec((None, n_tiles * ts, d), lambda b, i: (b, i, 0))
    x_tile, gather_scratch = tile, []
    if natural_in:
        x = x.reshape(bsz, s // ts, SUBLANES, TILE_VREGS, d)
        x_tile = pl.BlockSpec(memory_space=pl.ANY)
        gather_scratch = [pltpu.VMEM((2, n_tiles * ts, d), F32), pltpu.SemaphoreType.DMA((2,))]
    row = lambda v: v[:, None, :]
    mix_spec = lambda p: _layer_spec(p.shape, mixer_index)
    return pl.pallas_call(
        functools.partial(_rglru_kernel, natural_in),
        grid=(bsz, s // (n_tiles * ts)),
        in_specs=[
            x_tile,
            pl.BlockSpec((None, None, 6, d), lambda b, i: (layer, b, 0, 0)),
            _layer_spec(ng.shape, layer),
            mix_spec(w_in), mix_spec(conv_w), mix_spec(row(conv_b)), mix_spec(row(ba)), mix_spec(row(bx)),
            mix_spec(row(lam)), mix_spec(wa), mix_spec(wx), mix_spec(w_out),
        ],
        out_specs=tile,
        out_shape=jax.ShapeDtypeStruct((bsz, s, d), F32),
        scratch_shapes=[
            pltpu.VMEM(w_in.shape[1:], BF16),
            pltpu.VMEM((RG_BLOCKS, blk, 2 * blk), BF16),
            pltpu.VMEM(w_out.shape[1:], BF16),
            pltpu.VMEM(((RG_CONV - 1) * SUBLANES, d), F32),
            pltpu.VMEM((SUBLANES, d), F32),
            pltpu.VMEM((2, ts, d), BF16),
        ] + gather_scratch,
        compiler_params=_params("arbitrary", "arbitrary"),
        name="rglru_layer",
    )(x, mod, ng, w_in, conv_w, row(conv_b), row(ba), row(bx), row(lam), wa, wx, w_out)


def _ffn_kernel(natural_out, x_ref, mod_ref, ng_ref, wup_ref, cw_ref, cb_ref, wdn32_ref, o_ref, wdn_ref,
                tail_ref, act_ref):
    ts = ROW_TILE
    n_tiles = x_ref.shape[0] // ts
    f = wdn_ref.shape[0]
    fc = FFN_COL_CHUNK
    n_chunks = f // fc
    n_tail = FFN_CONV - 1

    @pl.when(_first_step())
    def _():
        wdn_ref[...] = wdn32_ref[...].astype(BF16)

    @pl.when(pl.program_id(1) == 0)
    def _():
        tail_ref[...] = jnp.zeros_like(tail_ref)

    def prologue(t):
        x = x_ref[t * ts:(t + 1) * ts, :]
        return x, _pre_norm(x, ng_ref[2:3, :], mod_ref[4:5, :], mod_ref[3:4, :]).astype(BF16)

    def epilogue(t, x, acc):
        out = _post_norm(x, acc, ng_ref[3:4, :], mod_ref[5:6, :])
        if natural_out:
            _store_natural_order(o_ref.at[t], out)
        else:
            o_ref[t * ts:(t + 1) * ts, :] = out

    def conv_cols(up, c0):
        cols = slice(c0, c0 + fc)
        tail = tail_ref[:, cols]
        tail_ref[:, cols] = up[ts - n_tail * SUBLANES:, :]
        y = cb_ref[:, cols] + _delay(up, tail, 2) * cw_ref[0:1, cols]
        y = y + _delay(up, tail, 1) * cw_ref[1:2, cols]
        return y + up * cw_ref[2:3, cols]

    items = [(t, c) for t in range(n_tiles) for c in range(n_chunks)]
    xs, hs = {}, {}
    xs[0], hs[0] = prologue(0)

    def up_pair(t, c):
        return (_dot(hs[t], wup_ref[:, c * fc:(c + 1) * fc]), _dot(hs[t], wup_ref[:, f + c * fc:f + (c + 1) * fc]))

    def down(t):
        epilogue(t, xs.pop(t), _dot(act_ref[t % 2], wdn_ref[...]))

    ups = up_pair(0, 0)
    due = None
    for idx, (t, c) in enumerate(items):
        if c == n_chunks - FFN_NEXT_NORM_LEAD and t + 1 < n_tiles:
            xs[t + 1], hs[t + 1] = prologue(t + 1)
        nxt = up_pair(*items[idx + 1]) if idx + 1 < len(items) else None
        if due is not None:
            down(due)
            due = None
        g = conv_cols(ups[0], c * fc)
        val = conv_cols(ups[1], f + c * fc)
        act_ref[t % 2, :, c * fc:(c + 1) * fc] = _gelu_times(g, val).astype(BF16)
        if c == n_chunks - 1:
            due = t
        ups = nxt
    down(due)


def _ffn_layer(x, mod, ng, w_up, conv_w, conv_b, w_down, layer, natural_out):
    bsz, s, d = x.shape
    ts = ROW_TILE
    n_tiles = TILES_PER_STEP
    rows = pl.BlockSpec((None, n_tiles * ts, d), lambda b, i: (b, i, 0))
    out_tile, out_shape = rows, (bsz, s, d)
    if natural_out:
        out_tile = pl.BlockSpec((None, n_tiles, SUBLANES, TILE_VREGS, d), lambda b, i: (b, i, 0, 0, 0))
        out_shape = (bsz, s // ts, SUBLANES, TILE_VREGS, d)
    out = pl.pallas_call(
        functools.partial(_ffn_kernel, natural_out),
        grid=(bsz, s // (n_tiles * ts)),
        in_specs=[
            rows,
            pl.BlockSpec((None, None, 6, d), lambda b, i: (layer, b, 0, 0)),
            _layer_spec(ng.shape, layer),
            _layer_spec(w_up.shape, layer),
            _layer_spec(conv_w.shape, layer),
            _layer_spec(conv_b.shape, layer),
            _layer_spec(w_down.shape, layer),
        ],
        out_specs=out_tile,
        out_shape=jax.ShapeDtypeStruct(out_shape, F32),
        scratch_shapes=[
            pltpu.VMEM(w_down.shape[1:], BF16),
            pltpu.VMEM(((FFN_CONV - 1) * SUBLANES, w_up.shape[2]), F32),
            pltpu.VMEM((2, ts, w_down.shape[1]), BF16),
        ],
        compiler_params=_params("arbitrary", "arbitrary"),
        name="ffn_layer",
    )(x, mod, ng, w_up, conv_w, conv_b, w_down)
    return out.reshape(bsz, s, d)


_STREAM_LEVELS = tuple(TILE_VREGS << i for i in (2, 1, 0))
_VREG_LEVELS = tuple(TILE_VREGS >> i for i in range(1, 4))
_DIRECT_BLOCK = 4
_MASKED_STREAM_LEVELS = _STREAM_LEVELS[1:]
_EXCLUDE = 1e30
_GLA_PROJ_PIECES = 3
_GLA_OUT_PIECES = 2


def _tile_order_times(c):
    p = np.arange(c)
    return (p % SUBLANES) * (c // SUBLANES) + p // SUBLANES


def _gla_stream_masks(c):
    t = _tile_order_times(c)
    return np.stack([(t[:, None] // (2 * b) == t[None, :] // (2 * b)) for b in _MASKED_STREAM_LEVELS]
                    ).astype(np.float32)


def _vreg_level_rows(b):
    upper = [j for j in range(TILE_VREGS) if (j // b) % 2 == 1]
    lower = [j for j in range(TILE_VREGS) if (j // b) % 2 == 0]
    return upper, lower


def _gla_vreg_masks():
    out = []
    s = np.arange(SUBLANES)
    for b in _VREG_LEVELS:
        upper, lower = _vreg_level_rows(b)
        ju = np.repeat(np.array(upper), SUBLANES)[:, None]
        jl = np.repeat(np.array(lower), SUBLANES)[None, :]
        su = np.tile(s, len(upper))[:, None]
        sl = np.tile(s, len(lower))[None, :]
        out.append((su == sl) & (ju // (2 * b) == jl // (2 * b)))
    return np.stack(out).astype(np.float32)


def _cumsum_tile_order(g):
    rows = [_vreg(g, 0)]
    for j in range(1, TILE_VREGS):
        rows.append(rows[-1] + _vreg(g, j))
    total = rows[-1]
    sub = _sublane((SUBLANES, 1))
    incl = total
    for k in (1, 2, 4):
        incl = incl + jnp.where(sub >= k, pltpu.roll(incl, k, 0), 0.0)
    before = incl - total
    return jnp.concatenate([r + before for r in rows], axis=0)


def _gla_scores(qk, v32, gc, state, fill):
    c, dk = gc.shape
    vreg = _vreg
    q = qk[:, :dk] * (dk ** -0.5)
    k = qk[:, dk:]
    v = v32.astype(BF16)
    g_last = gc[c - 1:c, :]

    fill()
    o_inter = _dot((q * jnp.exp(gc)).astype(BF16), state.astype(BF16))

    sub = _sublane((SUBLANES, 1))
    g_end = vreg(gc, TILE_VREGS - 1)
    tile_rows = lambda a: jnp.concatenate([a] * TILE_VREGS, axis=0)
    p_stream = []
    for b in _STREAM_LEVELS:
        w = b // TILE_VREGS
        g_mid = None
        for first in range(0, SUBLANES, 2 * w):
            row = jnp.broadcast_to(g_end[first + w - 1:first + w, :], (SUBLANES, dk))
            g_mid = row if g_mid is None else jnp.where(sub >= first, row, g_mid)
        upper = (lax.shift_right_logical(sub, w.bit_length() - 1) & 1) == 1
        g_q = tile_rows(jnp.where(upper, g_mid, _EXCLUDE))
        g_k = tile_rows(jnp.where(upper, -_EXCLUDE, g_mid))
        q_b = (q * jnp.exp(gc - g_q)).astype(BF16)
        k_b = (k * jnp.exp(g_k - gc)).astype(BF16)
        p_stream.append(_dot_nt(q_b, k_b))
    fill()

    p_vreg = []
    for b in _VREG_LEVELS:
        upper_rows, _ = _vreg_level_rows(b)
        q_rows, k_rows = [], []
        for j in range(TILE_VREGS):
            jm = (j // (2 * b)) * (2 * b) + b - 1
            if j in upper_rows:
                q_rows.append(vreg(q, j) * jnp.exp(vreg(gc, j) - vreg(gc, jm)))
            elif j == jm:
                k_rows.append(vreg(k, j))
            else:
                k_rows.append(vreg(k, j) * jnp.exp(vreg(gc, jm) - vreg(gc, j)))
        p_vreg.append(_dot_nt(jnp.concatenate(q_rows, axis=0).astype(BF16),
                              jnp.concatenate(k_rows, axis=0).astype(BF16)))
    fill()

    k_rev = (k * jnp.exp(g_last - gc)).astype(BF16)
    upd = lax.dot_general(k_rev, v, (((0,), (0,)), ((), ())), preferred_element_type=F32)
    decay_rows = jnp.broadcast_to(jnp.exp(g_last), (dk, dk))
    decay_col = jnp.transpose(decay_rows)[:, 0:1]
    new_state = decay_col * state + upd

    direct = []
    for ju in range(TILE_VREGS):
        terms = vreg(v32, ju) * jnp.sum(vreg(q, ju) * vreg(k, ju), axis=-1, keepdims=True)
        for jl in range((ju // _DIRECT_BLOCK) * _DIRECT_BLOCK, ju):
            w = vreg(q, ju) * vreg(k, jl) * jnp.exp(vreg(gc, ju) - vreg(gc, jl))
            terms = terms + vreg(v32, jl) * jnp.sum(w, axis=-1, keepdims=True)
        direct.append(terms)
    return (o_inter, p_stream, p_vreg, direct), new_state


def _gla_combine(scores, v32, r, hn, sm_ref, vm_ref, fill):
    o, p_stream, p_vreg, direct = scores
    v = v32.astype(BF16)

    attn = None
    for b, p in zip(_STREAM_LEVELS, p_stream):
        if b in _MASKED_STREAM_LEVELS:
            p = p * sm_ref[_MASKED_STREAM_LEVELS.index(b)]
        attn = p if attn is None else attn + p
    o = o + _dot(attn.astype(BF16), v)
    fill()

    o_rows = [_vreg(o, j) + direct[j] for j in range(TILE_VREGS)]
    for li, (b, p) in enumerate(zip(_VREG_LEVELS, p_vreg)):
        upper_rows, _ = _vreg_level_rows(b)
        p = (p * vm_ref[li]).astype(BF16)
        v_low = jnp.concatenate(
            [v[j0 * SUBLANES:(j0 + b) * SUBLANES] for j0 in range(0, TILE_VREGS, 2 * b)], axis=0)
        o_up = _dot(p, v_low)
        if li == len(_VREG_LEVELS) // 2:
            fill()
        for idx, j in enumerate(upper_rows):
            o_rows[j] = o_rows[j] + _vreg(o_up, idx)
    o = jnp.concatenate(o_rows, axis=0)
    half_r = 0.5 * r
    return (_rms(o, hn) * (half_r * jnp.tanh(half_r) + half_r)).astype(BF16)


def _gla_kernel(x_ref, mod_ref, ng_ref, win32_ref, wz_ref, wal_ref, bal_ref, hn_ref, sm_ref, vm_ref,
                wout32_ref, o_ref, wh_ref, wout_ref, state_ref, y_ref):
    c = ROW_TILE
    d = x_ref.shape[1]
    n_tiles = x_ref.shape[0] // c
    dk, dv = state_ref.shape[1:]
    qk = GLA_HEADS * dk

    @pl.when(_first_step())
    def _():
        wout_ref[...] = wout32_ref[...].astype(BF16)
        for hh in range(GLA_HEADS):
            for dst, src, w in ((0, hh * dk, dk), (dk, qk + hh * dk, dk), (2 * dk, 2 * qk + hh * dv, dv),
                                (2 * dk + dv, 2 * qk + d + hh * dv, dv)):
                wh_ref[hh, :, dst:dst + w] = win32_ref[:, src:src + w].astype(BF16)

    @pl.when(pl.program_id(1) == 0)
    def _():
        state_ref[...] = jnp.zeros_like(state_ref)

    def prologue(t):
        x = x_ref[t * c:(t + 1) * c, :]
        h = _pre_norm(x, ng_ref[0:1, :], mod_ref[1:2, :], mod_ref[0:1, :]).astype(BF16)
        z = _dot(h, wz_ref[...])
        ga = _dot(z.astype(BF16), wal_ref[...]) + bal_ref[...]
        g = (jnp.minimum(ga, 0.0) - jnp.log1p(jnp.exp(-jnp.abs(ga)))) / GLA_TAU
        return x, h, _cumsum_tile_order(g)

    def epilogue(t, x, acc):
        o_ref[t * c:(t + 1) * c, :] = _post_norm(x, acc, ng_ref[1:2, :], mod_ref[2:3, :])

    items = [(t, hh) for t in range(n_tiles) for hh in range(GLA_HEADS)]
    xs, hs, gcs = {}, {}, {}
    xs[0], hs[0], gcs[0] = prologue(0)

    def project_piece(t, hh, j):
        w = 2 * dk if j == 0 else dv
        c0 = 0 if j == 0 else 2 * dk + (j - 1) * dv
        return _dot(hs[t], wh_ref[hh, :, c0:c0 + w])

    def out_piece(t, parts, j):
        w = d // _GLA_OUT_PIECES
        parts.append(_dot(y_ref[t % 2], wout_ref[:, j * w:(j + 1) * w]))
        if j == _GLA_OUT_PIECES - 1:
            epilogue(t, xs.pop(t), jnp.concatenate(parts, axis=1))

    cur = [project_piece(0, 0, j) for j in range(_GLA_PROJ_PIECES)]
    due = None
    for idx, (t, hh) in enumerate(items):
        if hh == MIXER_NEXT_NORM_ITEM and t + 1 < n_tiles:
            xs[t + 1], hs[t + 1], gcs[t + 1] = prologue(t + 1)
        nxt, pending = [], []
        if idx + 1 < len(items):
            pending += [lambda j=j, idx=idx, nxt=nxt: nxt.append(project_piece(*items[idx + 1], j))
                        for j in range(_GLA_PROJ_PIECES)]
        if due is not None:
            parts = []
            pending += [functools.partial(out_piece, due, parts, j) for j in range(_GLA_OUT_PIECES)]
            due = None
        pending.reverse()
        fill = lambda pending=pending: pending.pop()() if pending else None
        qk, v32, r = cur
        scores, state_ref[hh] = _gla_scores(qk, v32, gcs[t][:, hh * dk:(hh + 1) * dk], state_ref[hh], fill)
        y_ref[t % 2, :, hh * dv:(hh + 1) * dv] = _gla_combine(scores, v32, r, hn_ref[...], sm_ref, vm_ref, fill)
        while pending:
            fill()
        if hh == GLA_HEADS - 1:
            due = t
        cur = nxt
    parts = []
    for j in range(_GLA_OUT_PIECES):
        out_piece(due, parts, j)


def _gla_layer(x, mod, ng, params, layer, mixer_index):
    w_in, w_alpha, b_alpha, head_norm, w_out = params
    bsz, s, d = x.shape
    c = ROW_TILE
    rank, qk = w_alpha.shape[1:]
    dk, dv = qk // GLA_HEADS, d // GLA_HEADS
    n_main = 2 * qk + 2 * d
    w_z = jnp.pad(w_in[mixer_index, :, n_main:], ((0, 0), (0, GLA_RANK_PAD - rank))).astype(BF16)
    w_al = jnp.pad(w_alpha[mixer_index], ((0, GLA_RANK_PAD - rank), (0, 0))).astype(BF16)
    stream_masks = jnp.asarray(_gla_stream_masks(c), F32)
    vreg_masks = jnp.asarray(_gla_vreg_masks(), F32)
    n_tiles = TILES_PER_STEP
    tile = pl.BlockSpec((None, n_tiles * c, d), lambda b, i: (b, i, 0))
    row = lambda v: v[:, None, :]
    mix_spec = lambda p: _layer_spec(p.shape, mixer_index)
    return pl.pallas_call(
        _gla_kernel,
        grid=(bsz, s // (n_tiles * c)),
        in_specs=[
            tile,
            pl.BlockSpec((None, None, 6, d), lambda b, i: (layer, b, 0, 0)),
            _layer_spec(ng.shape, layer),
            mix_spec(w_in),
            _const_spec(w_z.shape),
            _const_spec(w_al.shape),
            mix_spec(row(b_alpha)),
            mix_spec(row(head_norm)),
            _const_spec(stream_masks.shape),
            _const_spec(vreg_masks.shape),
            mix_spec(w_out),
        ],
        out_specs=tile,
        out_shape=jax.ShapeDtypeStruct(x.shape, F32),
        scratch_shapes=[
            pltpu.VMEM((GLA_HEADS, d, 2 * dk + 2 * dv), BF16),
            pltpu.VMEM(w_out.shape[1:], BF16),
            pltpu.VMEM((GLA_HEADS, dk, dv), F32),
            pltpu.VMEM((2, c, d), BF16),
        ],
        compiler_params=_params("arbitrary", "arbitrary"),
        name="gla_layer",
    )(x, mod, ng, w_in, w_z, w_al, row(b_alpha), row(head_norm), stream_masks, vreg_masks, w_out)


def kernel(x, c, ada_w, ada_b, norm_g, ffn_w_up, ffn_conv_w, ffn_conv_b, ffn_w_down, rg_w_in, rg_conv_w,
           rg_conv_b, rg_wa, rg_ba, rg_wx, rg_bx, rg_lambda, rg_w_out, gla_w_in, gla_w_alpha, gla_b_alpha,
           gla_norm_g, gla_w_out):
    depth = ada_w.shape[0]
    assert depth >= 1 and x.shape[1] % (ROW_TILE * TILES_PER_STEP) == 0
    mod = _ada_modulation(c, ada_w, ada_b)
    ffn_w_up_bf = ffn_w_up.astype(BF16)
    ffn_conv_b3 = ffn_conv_b[:, None, :]
    rg_params = (rg_w_in, rg_conv_w, rg_conv_b, rg_wa, rg_ba, rg_wx, rg_bx, rg_lambda, rg_w_out)
    gla_params = (gla_w_in, gla_w_alpha, gla_b_alpha, gla_norm_g, gla_w_out)
    for i in range(depth):
        j = i // 2
        if i % 2 == 0:
            x = _rglru_layer(x, mod, norm_g, rg_params, i, j, natural_in=(i == 0))
        else:
            x = _gla_layer(x, mod, norm_g, gla_params, i, j)
        x = _ffn_layer(x, mod, norm_g, ffn_w_up_bf, ffn_conv_w, ffn_conv_b3, ffn_w_down, i,
                       natural_out=(i == depth - 1))
    return x
```

```python
import functools
import math

import jax
import jax.numpy as jnp
import numpy as np
from jax import lax
from jax.experimental import pallas as pl
from jax.experimental.pallas import tpu as pltpu

F32 = jnp.float32
BF16 = jnp.bfloat16

EPS = 1e-6
RG_BLOCKS = 4
RG_C = 8.0
RG_CONV = 4
FFN_CONV = 3
GLA_HEADS = 4
GLA_TAU = 16.0
GLA_RANK_PAD = 128
SUBLANES = 8

ROW_TILE = 256
TILE_VREGS = ROW_TILE // SUBLANES
TILES_PER_STEP = 4
ADA_COL_TILE = 3072
FFN_COL_CHUNK = 256
MIXER_NEXT_NORM_ITEM = 1
FFN_NEXT_NORM_LEAD = 4
VMEM_LIMIT = 56 * 1024 * 1024

_GELU_C0 = 2.0 * math.sqrt(2.0 / math.pi) * math.log2(math.e)
_GELU_C1 = _GELU_C0 * 0.044715


def _dot(a, b):
    return jnp.dot(a, b, preferred_element_type=F32)


def _dot_nt(a, b):
    return lax.dot_general(a, b, (((1,), (1,)), ((), ())), preferred_element_type=F32)


def _rms(x, g):
    return x * lax.rsqrt(jnp.mean(x * x, axis=-1, keepdims=True) + EPS) * g


def _pre_norm(x, g, scale, shift):
    return _rms(x, g * (1.0 + scale)) + shift


def _post_norm(x, y, g, gate):
    return x + _rms(y, gate * g)


def _sigmoid(x):
    return 0.5 * jnp.tanh(0.5 * x) + 0.5


def _gelu_times(x, v):
    neg_2z_log2e = x * (-_GELU_C0 - _GELU_C1 * (x * x))
    return (x * v) / (1.0 + jnp.exp2(neg_2z_log2e))


def _sublane(shape):
    return lax.broadcasted_iota(jnp.int32, shape, 0) % SUBLANES


def _vreg(a, j):
    return a[j * SUBLANES:(j + 1) * SUBLANES]


def _load_tile_order(ref):
    return jnp.concatenate([ref[:, j, :] for j in range(TILE_VREGS)], axis=0)


def _store_natural_order(ref, val):
    for j in range(TILE_VREGS):
        ref[:, j, :] = _vreg(val, j)


def _delay(x, tail, k):
    ts = x.shape[0]
    n_tail = tail.shape[0] // SUBLANES
    last = _sublane((SUBLANES, 1)) == SUBLANES - 1
    head = []
    for j in range(k):
        cur = _vreg(x, TILE_VREGS - k + j)
        prv = _vreg(tail, n_tail - k + j)
        head.append(pltpu.roll(jnp.where(last, prv, cur), 1, 0))
    return jnp.concatenate(head + [x[:ts - k * SUBLANES]], axis=0)


def _params(*sem):
    return pltpu.CompilerParams(dimension_semantics=sem, vmem_limit_bytes=VMEM_LIMIT)


def _const_spec(shape):
    n = len(shape)
    return pl.BlockSpec(shape, lambda *_: (0,) * n, pipeline_mode=pl.Buffered(1))


def _layer_spec(shape, layer):
    n = len(shape) - 1
    return pl.BlockSpec((None,) + tuple(shape[1:]), lambda *_: (layer,) + (0,) * n,
                        pipeline_mode=pl.Buffered(1))


def _ada_kernel(c_ref, w_ref, b_ref, o_ref):
    c = c_ref[...]
    c_act = (c * jax.nn.sigmoid(c)).astype(BF16)
    o_ref[...] = _dot(c_act, w_ref[...].astype(BF16)) + b_ref[...]


def _ada_modulation(c, ada_w, ada_b):
    depth, d, n = ada_w.shape
    bsz = c.shape[0]
    tn = ADA_COL_TILE
    out = pl.pallas_call(
        _ada_kernel,
        grid=(depth, n // tn),
        in_specs=[
            pl.BlockSpec((bsz, d), lambda l, j: (0, 0)),
            pl.BlockSpec((None, d, tn), lambda l, j: (l, 0, j)),
            pl.BlockSpec((None, 1, tn), lambda l, j: (l, 0, j)),
        ],
        out_specs=pl.BlockSpec((None, bsz, tn), lambda l, j: (l, 0, j)),
        out_shape=jax.ShapeDtypeStruct((depth, bsz, n), F32),
        compiler_params=_params("arbitrary", "arbitrary"),
        name="ada_mod",
    )(c, ada_w, ada_b.reshape(depth, 1, n))
    return out.reshape(depth, bsz, 6, d)


def _stream_scan(a, u, carry):
    hl = [_vreg(u, 0)]
    al = [_vreg(a, 0)]
    for j in range(1, TILE_VREGS):
        hl.append(_vreg(a, j) * hl[-1] + _vreg(u, j))
        al.append(_vreg(a, j) * al[-1])
    sub = _sublane((SUBLANES, 1))
    ea, eh = al[-1], hl[-1]
    for k in (1, 2, 4):
        keep = sub >= k
        ea_sh = jnp.where(keep, pltpu.roll(ea, k, 0), 1.0)
        eh_sh = jnp.where(keep, pltpu.roll(eh, k, 0), 0.0)
        eh = ea * eh_sh + eh
        ea = ea * ea_sh
    after = ea * carry + eh
    before = jnp.where(sub == 0, carry, pltpu.roll(after, 1, 0))
    h = jnp.concatenate([hl[j] + al[j] * before for j in range(TILE_VREGS)], axis=0)
    return h, after


def _first_step():
    return (pl.program_id(0) == 0) & (pl.program_id(1) == 0)


def _tile_order_gather(x_hbm, buf, sem, b, step, slot):
    n_tiles = buf.shape[1] // ROW_TILE
    return [
        pltpu.make_async_copy(x_hbm.at[b, step * n_tiles + t, :, j, :],
                              buf.at[slot, pl.ds(t * ROW_TILE + j * SUBLANES, SUBLANES), :], sem.at[slot])
        for t in range(n_tiles) for j in range(TILE_VREGS)]


def _rglru_kernel(natural_in, x_ref, mod_ref, ng_ref, win32_ref, cw_ref, cb_ref, ba_ref, bx_ref, lam_ref,
                  wa32_ref, wx32_ref, wout32_ref, o_ref, win_ref, wg_ref, wout_ref, tail_ref, hc_ref, y_ref,
                  *gather_scratch):
    ts = ROW_TILE
    d = o_ref.shape[1]
    n_tiles = o_ref.shape[0] // ts
    blk = d // RG_BLOCKS
    n_tail = RG_CONV - 1

    if natural_in:
        xbuf, sem = gather_scratch
        b, i = pl.program_id(0), pl.program_id(1)
        n_steps = pl.num_programs(1)
        linear = b * n_steps + i
        slot = linear % 2

        @pl.when(linear == 0)
        def _():
            for cp in _tile_order_gather(x_ref, xbuf, sem, b, i, slot):
                cp.start()

        @pl.when(linear + 1 < pl.num_programs(0) * n_steps)
        def _():
            wrap = i + 1 == n_steps
            for cp in _tile_order_gather(x_ref, xbuf, sem, jnp.where(wrap, b + 1, b), jnp.where(wrap, 0, i + 1),
                                         1 - slot):
                cp.start()

        for cp in _tile_order_gather(x_ref, xbuf, sem, b, i, slot):
            cp.wait()

    @pl.when(_first_step())
    def _():
        win_ref[...] = win32_ref[...].astype(BF16)
        wout_ref[...] = wout32_ref[...].astype(BF16)
        for g in range(RG_BLOCKS):
            wg_ref[g, :, :blk] = wa32_ref[g].astype(BF16)
            wg_ref[g, :, blk:] = wx32_ref[g].astype(BF16)

    @pl.when(pl.program_id(1) == 0)
    def _():
        tail_ref[...] = jnp.zeros_like(tail_ref)
        hc_ref[...] = jnp.zeros_like(hc_ref)

    def prologue(t):
        x = xbuf[slot, t * ts:(t + 1) * ts, :] if natural_in else x_ref[t * ts:(t + 1) * ts, :]
        return x, _pre_norm(x, ng_ref[0:1, :], mod_ref[1:2, :], mod_ref[0:1, :]).astype(BF16)

    def epilogue(t, x, acc):
        o_ref[t * ts:(t + 1) * ts, :] = _post_norm(x, acc, ng_ref[1:2, :], mod_ref[2:3, :])

    def project(t, g):
        return (_dot(hs[t], win_ref[:, g * blk:(g + 1) * blk]),
                _dot(hs[t], win_ref[:, d + g * blk:d + (g + 1) * blk]))

    def mix(g, gate_br, xb):
        cols = slice(g * blk, (g + 1) * blk)
        tail = tail_ref[:, cols]
        tail_ref[:, cols] = xb[ts - n_tail * SUBLANES:, :]
        xc = cb_ref[:, cols] + _delay(xb, tail, 3) * cw_ref[0:1, cols]
        xc = xc + _delay(xb, tail, 2) * cw_ref[1:2, cols]
        xc = xc + _delay(xb, tail, 1) * cw_ref[2:3, cols]
        xc = xc + xb * cw_ref[3:4, cols]
        p = _dot(xc.astype(BF16), wg_ref[g])
        i_g = _sigmoid(p[:, blk:] + bx_ref[:, cols])
        nl = -lam_ref[:, cols]
        softplus_nl = jnp.maximum(nl, 0.0) + jnp.log1p(jnp.exp(-jnp.abs(nl)))
        half_rate = (-0.5 * RG_C) * softplus_nl
        log_a = jnp.tanh(0.5 * (p[:, :blk] + ba_ref[:, cols])) * half_rate + half_rate
        a = jnp.exp(log_a)
        m2 = jnp.tanh(log_a) * (-1.0 - a * a)
        u = jnp.where(m2 > 0.0, m2 * lax.rsqrt(m2), 0.0) * (i_g * xc)
        hs_g, after = _stream_scan(a, u, hc_ref[SUBLANES - 1:SUBLANES, cols])
        hc_ref[:, cols] = after
        return _gelu_times(gate_br, hs_g).astype(BF16)

    items = [(t, g) for t in range(n_tiles) for g in range(RG_BLOCKS)]
    xs, hs = {}, {}
    xs[0], hs[0] = prologue(0)

    def out_project(t):
        epilogue(t, xs.pop(t), _dot(y_ref[t % 2], wout_ref[...]))

    cur = project(0, 0)
    due = None
    for idx, (t, g) in enumerate(items):
        if g == MIXER_NEXT_NORM_ITEM and t + 1 < n_tiles:
            xs[t + 1], hs[t + 1] = prologue(t + 1)
        nxt = project(*items[idx + 1]) if idx + 1 < len(items) else None
        if due is not None:
            out_project(due)
            due = None
        y_ref[t % 2, :, g * blk:(g + 1) * blk] = mix(g, *cur)
        if g == RG_BLOCKS - 1:
            due = t
        cur = nxt
    out_project(due)


def _rglru_layer(x, mod, ng, params, layer, mixer_index, natural_in):
    w_in, conv_w, conv_b, wa, ba, wx, bx, lam, w_out = params
    bsz, s, d = x.shape
    ts = ROW_TILE
    n_tiles = TILES_PER_STEP
    blk = d // RG_BLOCKS
    tile = pl.BlockSpec((None, n_tiles * ts, d), lambda b, i: (b, i, 0))
    x_tile, gather_scratch = tile, []
    if natural_in:
        x = x.reshape(bsz, s // ts, SUBLANES, TILE_VREGS, d)
        x_tile = pl.BlockSpec(memory_space=pl.ANY)
        gather_scratch = [pltpu.VMEM((2, n_tiles * ts, d), F32), pltpu.SemaphoreType.DMA((2,))]
    row = lambda v: v[:, None, :]
    mix_spec = lambda p: _layer_spec(p.shape, mixer_index)
    return pl.pallas_call(
        functools.partial(_rglru_kernel, natural_in),
        grid=(bsz, s // (n_tiles * ts)),
        in_specs=[
            x_tile,
            pl.BlockSpec((None, None, 6, d), lambda b, i: (layer, b, 0, 0)),
            _layer_spec(ng.shape, layer),
            mix_spec(w_in), mix_spec(conv_w), mix_spec(row(conv_b)), mix_spec(row(ba)), mix_spec(row(bx)),
            mix_spec(row(lam)), mix_spec(wa), mix_spec(wx), mix_spec(w_out),
        ],
        out_specs=tile,
        out_shape=jax.ShapeDtypeStruct((bsz, s, d), F32),
        scratch_shapes=[
            pltpu.VMEM(w_in.shape[1:], BF16),
            pltpu.VMEM((RG_BLOCKS, blk, 2 * blk), BF16),
            pltpu.VMEM(w_out.shape[1:], BF16),
            pltpu.VMEM(((RG_CONV - 1) * SUBLANES, d), F32),
            pltpu.VMEM((SUBLANES, d), F32),
            pltpu.VMEM((2, ts, d), BF16),
        ] + gather_scratch,
        compiler_params=_params("arbitrary", "arbitrary"),
        name="rglru_layer",
    )(x, mod, ng, w_in, conv_w, row(conv_b), row(ba), row(bx), row(lam), wa, wx, w_out)


def _ffn_kernel(natural_out, x_ref, mod_ref, ng_ref, wup_ref, cw_ref, cb_ref, wdn32_ref, o_ref, wdn_ref,
                tail_ref, act_ref):
    ts = ROW_TILE
    n_tiles = x_ref.shape[0] // ts
    f = wdn_ref.shape[0]
    fc = FFN_COL_CHUNK
    n_chunks = f // fc
    n_tail = FFN_CONV - 1

    @pl.when(_first_step())
    def _():
        wdn_ref[...] = wdn32_ref[...].astype(BF16)

    @pl.when(pl.program_id(1) == 0)
    def _():
        tail_ref[...] = jnp.zeros_like(tail_ref)

    def prologue(t):
        x = x_ref[t * ts:(t + 1) * ts, :]
        return x, _pre_norm(x, ng_ref[2:3, :], mod_ref[4:5, :], mod_ref[3:4, :]).astype(BF16)

    def epilogue(t, x, acc):
        out = _post_norm(x, acc, ng_ref[3:4, :], mod_ref[5:6, :])
        if natural_out:
            _store_natural_order(o_ref.at[t], out)
        else:
            o_ref[t * ts:(t + 1) * ts, :] = out

    def conv_cols(up, c0):
        cols = slice(c0, c0 + fc)
        tail = tail_ref[:, cols]
        tail_ref[:, cols] = up[ts - n_tail * SUBLANES:, :]
        y = cb_ref[:, cols] + _delay(up, tail, 2) * cw_ref[0:1, cols]
        y = y + _delay(up, tail, 1) * cw_ref[1:2, cols]
        return y + up * cw_ref[2:3, cols]

    items = [(t, c) for t in range(n_tiles) for c in range(n_chunks)]
    xs, hs = {}, {}
    xs[0], hs[0] = prologue(0)

    def up_pair(t, c):
        return (_dot(hs[t], wup_ref[:, c * fc:(c + 1) * fc]), _dot(hs[t], wup_ref[:, f + c * fc:f + (c + 1) * fc]))

    def down(t):
        epilogue(t, xs.pop(t), _dot(act_ref[t % 2], wdn_ref[...]))

    ups = up_pair(0, 0)
    due = None
    for idx, (t, c) in enumerate(items):
        if c == n_chunks - FFN_NEXT_NORM_LEAD and t + 1 < n_tiles:
            xs[t + 1], hs[t + 1] = prologue(t + 1)
        nxt = up_pair(*items[idx + 1]) if idx + 1 < len(items) else None
        if due is not None:
            down(due)
            due = None
        g = conv_cols(ups[0], c * fc)
        val = conv_cols(ups[1], f + c * fc)
        act_ref[t % 2, :, c * fc:(c + 1) * fc] = _gelu_times(g, val).astype(BF16)
        if c == n_chunks - 1:
            due = t
        ups = nxt
    down(due)


def _ffn_layer(x, mod, ng, w_up, conv_w, conv_b, w_down, layer, natural_out):
    bsz, s, d = x.shape
    ts = ROW_TILE
    n_tiles = TILES_PER_STEP // 2 if natural_out else TILES_PER_STEP
    rows = pl.BlockSpec((None, n_tiles * ts, d), lambda b, i: (b, i, 0))
    out_tile, out_shape = rows, (bsz, s, d)
    if natural_out:
        out_tile = pl.BlockSpec((None, n_tiles, SUBLANES, TILE_VREGS, d), lambda b, i: (b, i, 0, 0, 0))
        out_shape = (bsz, s // ts, SUBLANES, TILE_VREGS, d)
    out = pl.pallas_call(
        functools.partial(_ffn_kernel, natural_out),
        grid=(bsz, s // (n_tiles * ts)),
        in_specs=[
            rows,
            pl.BlockSpec((None, None, 6, d), lambda b, i: (layer, b, 0, 0)),
            _layer_spec(ng.shape, layer),
            _layer_spec(w_up.shape, layer),
            _layer_spec(conv_w.shape, layer),
            _layer_spec(conv_b.shape, layer),
            _layer_spec(w_down.shape, layer),
        ],
        out_specs=out_tile,
        out_shape=jax.ShapeDtypeStruct(out_shape, F32),
        scratch_shapes=[
            pltpu.VMEM(w_down.shape[1:], BF16),
            pltpu.VMEM(((FFN_CONV - 1) * SUBLANES, w_up.shape[2]), F32),
            pltpu.VMEM((2, ts, w_down.shape[1]), BF16),
        ],
        compiler_params=_params("arbitrary", "arbitrary"),
        name="ffn_layer",
    )(x, mod, ng, w_up, conv_w, conv_b, w_down)
    return out.reshape(bsz, s, d)


_STREAM_LEVELS = tuple(TILE_VREGS << i for i in (2, 1, 0))
_VREG_LEVELS = tuple(TILE_VREGS >> i for i in range(1, 4))
_DIRECT_BLOCK = 4
_MASKED_STREAM_LEVELS = _STREAM_LEVELS[1:]
_EXCLUDE = 1e30
_GLA_PROJ_PIECES = 3
_GLA_OUT_PIECES = 2


def _tile_order_times(c):
    p = np.arange(c)
    return (p % SUBLANES) * (c // SUBLANES) + p // SUBLANES


def _gla_stream_masks(c):
    t = _tile_order_times(c)
    return np.stack([(t[:, None] // (2 * b) == t[None, :] // (2 * b)) for b in _MASKED_STREAM_LEVELS]
                    ).astype(np.float32)


def _vreg_level_rows(b):
    upper = [j for j in range(TILE_VREGS) if (j // b) % 2 == 1]
    lower = [j for j in range(TILE_VREGS) if (j // b) % 2 == 0]
    return upper, lower


def _gla_vreg_masks():
    out = []
    s = np.arange(SUBLANES)
    for b in _VREG_LEVELS:
        upper, lower = _vreg_level_rows(b)
        ju = np.repeat(np.array(upper), SUBLANES)[:, None]
        jl = np.repeat(np.array(lower), SUBLANES)[None, :]
        su = np.tile(s, len(upper))[:, None]
        sl = np.tile(s, len(lower))[None, :]
        out.append((su == sl) & (ju // (2 * b) == jl // (2 * b)))
    return np.stack(out).astype(np.float32)


def _cumsum_tile_order(g):
    rows = [_vreg(g, 0)]
    for j in range(1, TILE_VREGS):
        rows.append(rows[-1] + _vreg(g, j))
    total = rows[-1]
    sub = _sublane((SUBLANES, 1))
    incl = total
    for k in (1, 2, 4):
        incl = incl + jnp.where(sub >= k, pltpu.roll(incl, k, 0), 0.0)
    before = incl - total
    return jnp.concatenate([r + before for r in rows], axis=0)


def _gla_scores(qk, v32, gc, state, fill):
    c, dk = gc.shape
    vreg = _vreg
    q = qk[:, :dk] * (dk ** -0.5)
    k = qk[:, dk:]
    v = v32.astype(BF16)
    g_last = gc[c - 1:c, :]

    fill()
    o_inter = _dot((q * jnp.exp(gc)).astype(BF16), state.astype(BF16))

    sub = _sublane((SUBLANES, 1))
    g_end = vreg(gc, TILE_VREGS - 1)
    tile_rows = lambda a: jnp.concatenate([a] * TILE_VREGS, axis=0)
    p_stream = []
    for b in _STREAM_LEVELS:
        w = b // TILE_VREGS
        g_mid = None
        for first in range(0, SUBLANES, 2 * w):
            row = jnp.broadcast_to(g_end[first + w - 1:first + w, :], (SUBLANES, dk))
            g_mid = row if g_mid is None else jnp.where(sub >= first, row, g_mid)
        upper = (lax.shift_right_logical(sub, w.bit_length() - 1) & 1) == 1
        g_q = tile_rows(jnp.where(upper, g_mid, _EXCLUDE))
        g_k = tile_rows(jnp.where(upper, -_EXCLUDE, g_mid))
        q_b = (q * jnp.exp(gc - g_q)).astype(BF16)
        k_b = (k * jnp.exp(g_k - gc)).astype(BF16)
        p_stream.append(_dot_nt(q_b, k_b))
    fill()

    p_vreg = []
    for b in _VREG_LEVELS:
        upper_rows, _ = _vreg_level_rows(b)
        q_rows, k_rows = [], []
        for j in range(TILE_VREGS):
            jm = (j // (2 * b)) * (2 * b) + b - 1
            if j in upper_rows:
                q_rows.append(vreg(q, j) * jnp.exp(vreg(gc, j) - vreg(gc, jm)))
            elif j == jm:
                k_rows.append(vreg(k, j))
            else:
                k_rows.append(vreg(k, j) * jnp.exp(vreg(gc, jm) - vreg(gc, j)))
        p_vreg.append(_dot_nt(jnp.concatenate(q_rows, axis=0).astype(BF16),
                              jnp.concatenate(k_rows, axis=0).astype(BF16)))
    fill()

    k_rev = (k * jnp.exp(g_last - gc)).astype(BF16)
    upd = lax.dot_general(k_rev, v, (((0,), (0,)), ((), ())), preferred_element_type=F32)
    decay_rows = jnp.broadcast_to(jnp.exp(g_last), (dk, dk))
    decay_col = jnp.transpose(decay_rows)[:, 0:1]
    new_state = decay_col * state + upd

    direct = []
    for ju in range(TILE_VREGS):
        terms = vreg(v32, ju) * jnp.sum(vreg(q, ju) * vreg(k, ju), axis=-1, keepdims=True)
        for jl in range((ju // _DIRECT_BLOCK) * _DIRECT_BLOCK, ju):
            w = vreg(q, ju) * vreg(k, jl) * jnp.exp(vreg(gc, ju) - vreg(gc, jl))
            terms = terms + vreg(v32, jl) * jnp.sum(w, axis=-1, keepdims=True)
        direct.append(terms)
    return (o_inter, p_stream, p_vreg, direct), new_state


def _gla_combine(scores, v32, r, hn, sm_ref, vm_ref, fill):
    o, p_stream, p_vreg, direct = scores
    v = v32.astype(BF16)

    attn = None
    for b, p in zip(_STREAM_LEVELS, p_stream):
        if b in _MASKED_STREAM_LEVELS:
            p = p * sm_ref[_MASKED_STREAM_LEVELS.index(b)]
        attn = p if attn is None else attn + p
    o = o + _dot(attn.astype(BF16), v)
    fill()

    o_rows = [_vreg(o, j) + direct[j] for j in range(TILE_VREGS)]
    for li, (b, p) in enumerate(zip(_VREG_LEVELS, p_vreg)):
        upper_rows, _ = _vreg_level_rows(b)
        p = (p * vm_ref[li]).astype(BF16)
        v_low = jnp.concatenate(
            [v[j0 * SUBLANES:(j0 + b) * SUBLANES] for j0 in range(0, TILE_VREGS, 2 * b)], axis=0)
        o_up = _dot(p, v_low)
        if li == len(_VREG_LEVELS) // 2:
            fill()
        for idx, j in enumerate(upper_rows):
            o_rows[j] = o_rows[j] + _vreg(o_up, idx)
    o = jnp.concatenate(o_rows, axis=0)
    half_r = 0.5 * r
    return (_rms(o, hn) * (half_r * jnp.tanh(half_r) + half_r)).astype(BF16)


def _gla_kernel(x_ref, mod_ref, ng_ref, win32_ref, wz_ref, wal_ref, bal_ref, hn_ref, sm_ref, vm_ref,
                wout32_ref, o_ref, wh_ref, wout_ref, state_ref, y_ref):
    c = ROW_TILE
    d = x_ref.shape[1]
    n_tiles = x_ref.shape[0] // c
    dk, dv = state_ref.shape[1:]
    qk = GLA_HEADS * dk

    @pl.when(_first_step())
    def _():
        wout_ref[...] = wout32_ref[...].astype(BF16)
        for hh in range(GLA_HEADS):
            for dst, src, w in ((0, hh * dk, dk), (dk, qk + hh * dk, dk), (2 * dk, 2 * qk + hh * dv, dv),
                                (2 * dk + dv, 2 * qk + d + hh * dv, dv)):
                wh_ref[hh, :, dst:dst + w] = win32_ref[:, src:src + w].astype(BF16)

    @pl.when(pl.program_id(1) == 0)
    def _():
        state_ref[...] = jnp.zeros_like(state_ref)

    def prologue(t):
        x = x_ref[t * c:(t + 1) * c, :]
        h = _pre_norm(x, ng_ref[0:1, :], mod_ref[1:2, :], mod_ref[0:1, :]).astype(BF16)
        z = _dot(h, wz_ref[...])
        ga = _dot(z.astype(BF16), wal_ref[...]) + bal_ref[...]
        g = (jnp.minimum(ga, 0.0) - jnp.log1p(jnp.exp(-jnp.abs(ga)))) / GLA_TAU
        return x, h, _cumsum_tile_order(g)

    def epilogue(t, x, acc):
        o_ref[t * c:(t + 1) * c, :] = _post_norm(x, acc, ng_ref[1:2, :], mod_ref[2:3, :])

    items = [(t, hh) for t in range(n_tiles) for hh in range(GLA_HEADS)]
    xs, hs, gcs = {}, {}, {}
    xs[0], hs[0], gcs[0] = prologue(0)

    def project_piece(t, hh, j):
        w = 2 * dk if j == 0 else dv
        c0 = 0 if j == 0 else 2 * dk + (j - 1) * dv
        return _dot(hs[t], wh_ref[hh, :, c0:c0 + w])

    def out_piece(t, parts, j):
        w = d // _GLA_OUT_PIECES
        parts.append(_dot(y_ref[t % 2], wout_ref[:, j * w:(j + 1) * w]))
        if j == _GLA_OUT_PIECES - 1:
            epilogue(t, xs.pop(t), jnp.concatenate(parts, axis=1))

    cur = [project_piece(0, 0, j) for j in range(_GLA_PROJ_PIECES)]
    due = None
    for idx, (t, hh) in enumerate(items):
        if hh == MIXER_NEXT_NORM_ITEM and t + 1 < n_tiles:
            xs[t + 1], hs[t + 1], gcs[t + 1] = prologue(t + 1)
        nxt, pending = [], []
        if idx + 1 < len(items):
            pending += [lambda j=j, idx=idx, nxt=nxt: nxt.append(project_piece(*items[idx + 1], j))
                        for j in range(_GLA_PROJ_PIECES)]
        if due is not None:
            parts = []
            pending += [functools.partial(out_piece, due, parts, j) for j in range(_GLA_OUT_PIECES)]
            due = None
        pending.reverse()
        fill = lambda pending=pending: pending.pop()() if pending else None
        qk, v32, r = cur
        scores, state_ref[hh] = _gla_scores(qk, v32, gcs[t][:, hh * dk:(hh + 1) * dk], state_ref[hh], fill)
        y_ref[t % 2, :, hh * dv:(hh + 1) * dv] = _gla_combine(scores, v32, r, hn_ref[...], sm_ref, vm_ref, fill)
        while pending:
            fill()
        if hh == GLA_HEADS - 1:
            due = t
        cur = nxt
    parts = []
    for j in range(_GLA_OUT_PIECES):
        out_piece(due, parts, j)


def _gla_layer(x, mod, ng, params, layer, mixer_index):
    w_in, w_alpha, b_alpha, head_norm, w_out = params
    bsz, s, d = x.shape
    c = ROW_TILE
    rank, qk = w_alpha.shape[1:]
    dk, dv = qk // GLA_HEADS, d // GLA_HEADS
    n_main = 2 * qk + 2 * d
    w_z = jnp.pad(w_in[mixer_index, :, n_main:], ((0, 0), (0, GLA_RANK_PAD - rank))).astype(BF16)
    w_al = jnp.pad(w_alpha[mixer_index], ((0, GLA_RANK_PAD - rank), (0, 0))).astype(BF16)
    stream_masks = jnp.asarray(_gla_stream_masks(c), F32)
    vreg_masks = jnp.asarray(_gla_vreg_masks(), F32)
    n_tiles = TILES_PER_STEP
    tile = pl.BlockSpec((None, n_tiles * c, d), lambda b, i: (b, i, 0))
    row = lambda v: v[:, None, :]
    mix_spec = lambda p: _layer_spec(p.shape, mixer_index)
    return pl.pallas_call(
        _gla_kernel,
        grid=(bsz, s // (n_tiles * c)),
        in_specs=[
            tile,
            pl.BlockSpec((None, None, 6, d), lambda b, i: (layer, b, 0, 0)),
            _layer_spec(ng.shape, layer),
            mix_spec(w_in),
            _const_spec(w_z.shape),
            _const_spec(w_al.shape),
            mix_spec(row(b_alpha)),
            mix_spec(row(head_norm)),
            _const_spec(stream_masks.shape),
            _const_spec(vreg_masks.shape),
            mix_spec(w_out),
        ],
        out_specs=tile,
        out_shape=jax.ShapeDtypeStruct(x.shape, F32),
        scratch_shapes=[
            pltpu.VMEM((GLA_HEADS, d, 2 * dk + 2 * dv), BF16),
            pltpu.VMEM(w_out.shape[1:], BF16),
            pltpu.VMEM((GLA_HEADS, dk, dv), F32),
            pltpu.VMEM((2, c, d), BF16),
        ],
        compiler_params=_params("arbitrary", "arbitrary"),
        name="gla_layer",
    )(x, mod, ng, w_in, w_z, w_al, row(b_alpha), row(head_norm), stream_masks, vreg_masks, w_out)


def kernel(x, c, ada_w, ada_b, norm_g, ffn_w_up, ffn_conv_w, ffn_conv_b, ffn_w_down, rg_w_in, rg_conv_w,
           rg_conv_b, rg_wa, rg_ba, rg_wx, rg_bx, rg_lambda, rg_w_out, gla_w_in, gla_w_alpha, gla_b_alpha,
           gla_norm_g, gla_w_out):
    depth = ada_w.shape[0]
    assert depth >= 1 and x.shape[1] % (ROW_TILE * TILES_PER_STEP) == 0
    mod = _ada_modulation(c, ada_w, ada_b)
    ffn_w_up_bf = ffn_w_up.astype(BF16)
    ffn_conv_b3 = ffn_conv_b[:, None, :]
    rg_params = (rg_w_in, rg_conv_w, rg_conv_b, rg_wa, rg_ba, rg_wx, rg_bx, rg_lambda, rg_w_out)
    gla_params = (gla_w_in, gla_w_alpha, gla_b_alpha, gla_norm_g, gla_w_out)
    for i in range(depth):
        j = i // 2
        if i % 2 == 0:
            x = _rglru_layer(x, mod, norm_g, rg_params, i, j, natural_in=(i == 0))
        else:
            x = _gla_layer(x, mod, norm_g, gla_params, i, j)
        x = _ffn_layer(x, mod, norm_g, ffn_w_up_bf, ffn_conv_w, ffn_conv_b3, ffn_w_down, i,
                       natural_out=(i == depth - 1))
    return x
```
